```python
import math
import jax, jax.numpy as jnp
from jax import lax
import numpy as np

D_MODEL = 1024
BATCH = 8
SEQ = 2048
DEPTH = 4
DEC_BATCH = 128
DEC_SEQ = 8
PAST_LEN = 8192
PAGE_SIZE = 128

N_MIXERS = 2
N_ATTN_LAYERS = (DEPTH + 1) // 2
N_GLA_LAYERS = DEPTH // 2
EPS = 1e-6
WINDOW = 128
HEAD_DIM = 64
N_HEADS = D_MODEL // HEAD_DIM
N_KV_HEADS = 4
GROUP = N_HEADS // N_KV_HEADS
Q_DIM = N_HEADS * HEAD_DIM
KV_DIM = N_KV_HEADS * HEAD_DIM
GLA_HEADS = 4
GLA_DK = D_MODEL // 2
GLA_DV = D_MODEL
GLA_DK_HEAD = GLA_DK // GLA_HEADS
GLA_DV_HEAD = GLA_DV // GLA_HEADS
GLA_GATE_RANK = 16
GLA_GATE_NORMALIZER = 16.0
GLA_CHUNK = 64
GLA_IN_DIM = 2 * GLA_DK + 2 * GLA_DV + GLA_GATE_RANK
D_FF = -(-(8 * D_MODEL) // (3 * 256)) * 256

kernel_name = 'swa_gla_hybrid_decode_step'


def rmsnorm(x, g):
    xf = x.astype(jnp.float32)
    r = lax.rsqrt(jnp.mean(xf * xf, axis=-1, keepdims=True) + EPS)
    return (xf * r * g.astype(jnp.float32)).astype(x.dtype)


def alibi_slopes(n):
    return jnp.exp2(-8.0 * jnp.arange(1, n + 1, dtype=jnp.float32) / n)


def swa_attend(q, k, v, q_pos, k_pos, sinks):
    s = jnp.einsum('bnqkgd,bnskd->bnkgqs', q, k, preferred_element_type=jnp.float32) * (HEAD_DIM ** -0.5)
    dist = q_pos[:, :, None] - k_pos[:, None, :]
    valid = (dist >= 0) & (dist <= WINDOW) & (k_pos[:, None, :] >= 0)
    slopes = alibi_slopes(N_HEADS).reshape(N_KV_HEADS, GROUP)
    s = s - slopes[None, None, :, :, None, None] * dist.astype(jnp.float32)[None, :, None, None, :, :]
    s = jnp.where(valid[None, :, None, None, :, :], s, -jnp.inf)
    sink = sinks.astype(jnp.float32).reshape(N_KV_HEADS, GROUP)[None, None, :, :, None, None]
    m = jnp.maximum(jnp.max(s, axis=-1, keepdims=True), sink)
    p = jnp.exp(s - m)
    denom = jnp.sum(p, axis=-1, keepdims=True) + jnp.exp(sink - m)
    return jnp.einsum('bnkgqs,bnskd->bnqkgd', (p / denom).astype(v.dtype), v)


def swa_project(h, w_in):
    proj = h @ w_in
    return jnp.split(proj, [Q_DIM, Q_DIM + KV_DIM], axis=-1)


def swa_prompt(h, w_in, w_out, sinks):
    B, L, _ = h.shape
    q, k, v = swa_project(h, w_in)
    nb = L // WINDOW
    q = q.reshape(B, nb, WINDOW, N_KV_HEADS, GROUP, HEAD_DIM)
    k = k.reshape(B, L, N_KV_HEADS, HEAD_DIM)
    v = v.reshape(B, L, N_KV_HEADS, HEAD_DIM)
    pad = jnp.zeros((B, WINDOW, N_KV_HEADS, HEAD_DIM), k.dtype)
    kp = jnp.concatenate([pad, k], axis=1).reshape(B, nb + 1, WINDOW, N_KV_HEADS, HEAD_DIM)
    vp = jnp.concatenate([pad, v], axis=1).reshape(B, nb + 1, WINDOW, N_KV_HEADS, HEAD_DIM)
    k_ctx = jnp.concatenate([kp[:, :-1], kp[:, 1:]], axis=2)
    v_ctx = jnp.concatenate([vp[:, :-1], vp[:, 1:]], axis=2)
    blk = jnp.arange(nb, dtype=jnp.int32)[:, None] * WINDOW
    q_pos = blk + jnp.arange(WINDOW, dtype=jnp.int32)[None]
    k_pos = blk - WINDOW + jnp.arange(2 * WINDOW, dtype=jnp.int32)[None]
    o = swa_attend(q, k_ctx, v_ctx, q_pos, k_pos, sinks).reshape(B, L, Q_DIM)
    return o @ w_out, k[:, L - WINDOW:], v[:, L - WINDOW:]


def swa_sample(h, k_buf, v_buf, w_in, w_out, sinks):
    B, L, _ = h.shape
    q, k, v = swa_project(h, w_in)
    q = q.reshape(B, 1, L, N_KV_HEADS, GROUP, HEAD_DIM)
    k_ctx = jnp.concatenate([k_buf.astype(k.dtype), k.reshape(B, L, N_KV_HEADS, HEAD_DIM)], axis=1)
    v_ctx = jnp.concatenate([v_buf.astype(v.dtype), v.reshape(B, L, N_KV_HEADS, HEAD_DIM)], axis=1)
    q_pos = (PAST_LEN + jnp.arange(L, dtype=jnp.int32))[None]
    k_pos = (PAST_LEN - WINDOW + jnp.arange(WINDOW + L, dtype=jnp.int32))[None]
    o = swa_attend(q, k_ctx[:, None], v_ctx[:, None], q_pos, k_pos, sinks).reshape(B, L, Q_DIM)
    return o @ w_out, k_ctx[:, L:], v_ctx[:, L:]


def gla_core(q, k, v, log_a, s0):
    B, L, H, DK = q.shape
    DV = v.shape[-1]
    c = math.gcd(L, GLA_CHUNK)
    n = L // c
    q = (q.astype(jnp.float32) * (DK ** -0.5)).reshape(B, n, c, H, DK)
    k = k.astype(jnp.float32).reshape(B, n, c, H, DK)
    v = v.astype(jnp.float32).reshape(B, n, c, H, DV)
    b = jnp.cumsum(log_a.astype(jnp.float32).reshape(B, n, c, H, DK), axis=2)
    b_ref = b[:, :, c // 2:c // 2 + 1]
    b_end = b[:, :, -1:]
    a_mat = jnp.einsum('bnthd,bnshd->bnhts', q * jnp.exp(b - b_ref), k * jnp.exp(b_ref - b))
    causal = jnp.tril(jnp.ones((c, c), dtype=bool))
    a_mat = jnp.where(causal, a_mat, 0.0)
    o_intra = jnp.einsum('bnhts,bnshv->bnthv', a_mat, v)
    q_in = jnp.moveaxis(q * jnp.exp(b), 1, 0)
    k_end = jnp.moveaxis(k * jnp.exp(b_end - b), 1, 0)
    v_s = jnp.moveaxis(v, 1, 0)
    dec = jnp.moveaxis(jnp.exp(b_end[:, :, 0]), 1, 0)

    def step(S, inp):
        qc, kc, vc, dc = inp
        o = jnp.einsum('bthd,bhdv->bthv', qc, S)
        S = dc[..., None] * S + jnp.einsum('bshd,bshv->bhdv', kc, vc)
        return S, o

    s_fin, o_inter = lax.scan(step, s0.astype(jnp.float32), (q_in, k_end, v_s, dec))
    o = o_intra + jnp.moveaxis(o_inter, 0, 1)
    return o.reshape(B, L, H, DV), s_fin


def gla_mixer(h, s0, w_in, w_gate_up, b_gate, norm_g, w_out):
    B, L, _ = h.shape
    q, k, v, r, gd = jnp.split(h @ w_in, [GLA_DK, 2 * GLA_DK, 2 * GLA_DK + GLA_DV, 2 * GLA_DK + 2 * GLA_DV], axis=-1)
    log_a = jax.nn.log_sigmoid((gd @ w_gate_up + b_gate).astype(jnp.float32)) / GLA_GATE_NORMALIZER
    o, s_fin = gla_core(q.reshape(B, L, GLA_HEADS, GLA_DK_HEAD),
                        k.reshape(B, L, GLA_HEADS, GLA_DK_HEAD),
                        v.reshape(B, L, GLA_HEADS, GLA_DV_HEAD),
                        log_a.reshape(B, L, GLA_HEADS, GLA_DK_HEAD), s0)
    o = rmsnorm(o, norm_g).reshape(B, L, GLA_DV) * jax.nn.silu(r.astype(jnp.float32))
    return o.astype(h.dtype) @ w_out, s_fin


def swiglu_ffn(h, w_in, w_out):
    g, u = jnp.split(h @ w_in, [D_FF], axis=-1)
    return (jax.nn.silu(g) * u) @ w_out


def setup_inputs(seed: int = 0) -> dict:
    key = jax.random.key(seed)
    ks = jax.random.split(key, 18)
    f32 = jnp.float32
    nrm = lambda k, shape, scale: jax.random.normal(k, shape, f32) * scale
    return {
        'x_prompt': nrm(ks[0], (BATCH, SEQ, D_MODEL), 1.0),
        'x_sample': nrm(ks[1], (DEC_BATCH, DEC_SEQ, D_MODEL), 1.0),
        'cache_k': nrm(ks[2], (N_ATTN_LAYERS, DEC_BATCH, WINDOW, N_KV_HEADS, HEAD_DIM), 1.0),
        'cache_v': nrm(ks[3], (N_ATTN_LAYERS, DEC_BATCH, WINDOW, N_KV_HEADS, HEAD_DIM), 1.0),
        'state_gla': nrm(ks[4], (N_GLA_LAYERS, DEC_BATCH, GLA_HEADS, GLA_DK_HEAD, GLA_DV_HEAD), 1.0),
        'norm_mix': 1.0 + nrm(ks[5], (DEPTH, D_MODEL), 0.05),
        'norm_ffn': 1.0 + nrm(ks[6], (DEPTH, D_MODEL), 0.05),
        'norm_final': 1.0 + nrm(ks[7], (D_MODEL,), 0.05),
        'w_attn_in': nrm(ks[8], (N_ATTN_LAYERS, D_MODEL, Q_DIM + 2 * KV_DIM), D_MODEL ** -0.5),
        'w_attn_out': nrm(ks[9], (N_ATTN_LAYERS, Q_DIM, D_MODEL), Q_DIM ** -0.5),
        'attn_sinks': nrm(ks[10], (N_ATTN_LAYERS, N_HEADS), 0.5),
        'w_gla_in': nrm(ks[11], (N_GLA_LAYERS, D_MODEL, GLA_IN_DIM), D_MODEL ** -0.5),
        'w_gla_gate_up': nrm(ks[12], (N_GLA_LAYERS, GLA_GATE_RANK, GLA_DK), GLA_GATE_RANK ** -0.5),
        'b_gla_gate': nrm(ks[13], (N_GLA_LAYERS, GLA_DK), 0.1),
        'gla_out_norm': 1.0 + nrm(ks[14], (N_GLA_LAYERS, GLA_DV_HEAD), 0.05),
        'w_gla_out': nrm(ks[15], (N_GLA_LAYERS, GLA_DV, D_MODEL), GLA_DV ** -0.5),
        'w_ffn_in': nrm(ks[16], (DEPTH, D_MODEL, 2 * D_FF), D_MODEL ** -0.5),
        'w_ffn_out': nrm(ks[17], (DEPTH, D_FF, D_MODEL), D_FF ** -0.5),
    }


def reference(x_prompt, x_sample, cache_k, cache_v, state_gla, norm_mix, norm_ffn, norm_final,
              w_attn_in, w_attn_out, attn_sinks, w_gla_in, w_gla_gate_up, b_gla_gate, gla_out_norm,
              w_gla_out, w_ffn_in, w_ffn_out):
    yp, ys = x_prompt, x_sample
    kp_l, vp_l, sp_l, ks_l, vs_l, ss_l = [], [], [], [], [], []
    for i in range(DEPTH):
        j = i // N_MIXERS
        hp = rmsnorm(yp, norm_mix[i])
        hs = rmsnorm(ys, norm_mix[i])
        if i % N_MIXERS == 0:
            op, kp, vp = swa_prompt(hp, w_attn_in[j], w_attn_out[j], attn_sinks[j])
            os_, kn, vn = swa_sample(hs, cache_k[j], cache_v[j], w_attn_in[j], w_attn_out[j], attn_sinks[j])
            kp_l.append(kp); vp_l.append(vp); ks_l.append(kn); vs_l.append(vn)
        else:
            s0 = jnp.zeros((yp.shape[0], GLA_HEADS, GLA_DK_HEAD, GLA_DV_HEAD), jnp.float32)
            op, sp = gla_mixer(hp, s0, w_gla_in[j], w_gla_gate_up[j], b_gla_gate[j], gla_out_norm[j], w_gla_out[j])
            os_, sn = gla_mixer(hs, state_gla[j], w_gla_in[j], w_gla_gate_up[j], b_gla_gate[j], gla_out_norm[j], w_gla_out[j])
            sp_l.append(sp); ss_l.append(sn)
        yp = yp + op
        ys = ys + os_
        yp = yp + swiglu_ffn(rmsnorm(yp, norm_ffn[i]), w_ffn_in[i], w_ffn_out[i])
        ys = ys + swiglu_ffn(rmsnorm(ys, norm_ffn[i]), w_ffn_in[i], w_ffn_out[i])
    y_prompt = rmsnorm(yp, norm_final)
    y_sample = rmsnorm(ys, norm_final)
    return (y_prompt, y_sample, jnp.stack(kp_l), jnp.stack(vp_l), jnp.stack(sp_l),
            jnp.stack(ks_l), jnp.stack(vs_l), jnp.stack(ss_l))
```

```python
import functools

import jax
import jax.numpy as jnp
from jax import lax
from jax.experimental import pallas as pl
from jax.experimental.pallas import tpu as pltpu

F32 = jnp.float32
BF16 = jnp.bfloat16

D_MODEL = 1024
EPS = 1e-6
WINDOW = 128
HEAD_DIM = 64
N_HEADS = 16
N_KV_HEADS = 4
GROUP = 4
Q_DIM = 1024
KV_DIM = 256
GLA_HEADS = 4
GLA_DK = 512
GLA_DV = 1024
GLA_DK_HEAD = 128
GLA_DV_HEAD = 256
GLA_GATE_RANK = 16
GLA_GATE_NORMALIZER = 16.0
GLA_CHUNK = 64
D_FF = 2816
PAST_LEN = 8192
MASKED = -1e30

VMEM_LIMIT_BYTES = 52 * 1024 * 1024
TOKEN_TILE = 1024
FFN_TILE = 256

NT_DIMS = (((1,), (1,)), ((), ()))
TN_DIMS = (((0,), (0,)), ((), ()))


def _params(*sem):
    return pltpu.CompilerParams(dimension_semantics=sem, vmem_limit_bytes=VMEM_LIMIT_BYTES)


def _rms(x, g):
    r = lax.rsqrt(jnp.mean(x * x, axis=-1, keepdims=True) + EPS)
    return x * r * g


def _silu(x):
    return x / (1.0 + jnp.exp(-x))


def _norm_matmul_kernel(n, x_ref, g_ref, *refs):
    h = _rms(x_ref[...], g_ref[...]).astype(BF16)
    for w_ref, o_ref in zip(refs[:n], refs[n:]):
        o_ref[...] = jnp.dot(h, w_ref[...], preferred_element_type=F32)


def _norm_matmul(x, g, ws, tm):
    t = x.shape[0]
    in_specs = [pl.BlockSpec((tm, D_MODEL), lambda i: (i, 0)),
                pl.BlockSpec((1, D_MODEL), lambda i: (0, 0))]
    in_specs += [pl.BlockSpec(w.shape, lambda i: (0, 0)) for w in ws]
    out_specs = [pl.BlockSpec((tm, w.shape[1]), lambda i: (i, 0)) for w in ws]
    out_shape = [jax.ShapeDtypeStruct((t, w.shape[1]), F32) for w in ws]
    return pl.pallas_call(
        functools.partial(_norm_matmul_kernel, len(ws)),
        grid=(t // tm,), in_specs=in_specs, out_specs=out_specs, out_shape=out_shape,
        compiler_params=_params("parallel"), name="norm_matmul",
    )(x, g.reshape(1, D_MODEL), *ws)


def _ffn_kernel(final_norm, x_ref, a_ref, wm_ref, g_ref, wg_ref, wu_ref, wo_ref, gf_ref,
                o_ref, y_ref, h_ref, acc_ref):
    j = pl.program_id(1)

    @pl.when(j == 0)
    def _():
        y = x_ref[...] + jnp.dot(a_ref[...], wm_ref[...], preferred_element_type=F32)
        y_ref[...] = y
        h_ref[...] = _rms(y, g_ref[...]).astype(BF16)
        acc_ref[...] = jnp.zeros_like(acc_ref)

    h = h_ref[...]
    gate = jnp.dot(h, wg_ref[...], preferred_element_type=F32)
    up = jnp.dot(h, wu_ref[...], preferred_element_type=F32)
    act = (_silu(gate) * up).astype(BF16)
    acc_ref[...] += jnp.dot(act, wo_ref[...], preferred_element_type=F32)

    @pl.when(j == pl.num_programs(1) - 1)
    def _():
        y = y_ref[...] + acc_ref[...]
        if final_norm:
            y = _rms(y, gf_ref[...])
        o_ref[...] = y


def _mix_out_ffn(x, a, w_mix, g, w_in, w_out, g_final, final_norm, tm):
    t = x.shape[0]
    nf = D_FF // FFN_TILE
    row = lambda i, j: (i, 0)
    const = lambda i, j: (0, 0)
    return pl.pallas_call(
        functools.partial(_ffn_kernel, final_norm),
        grid=(t // tm, nf),
        in_specs=[pl.BlockSpec((tm, D_MODEL), row),
                  pl.BlockSpec((tm, D_MODEL), row),
                  pl.BlockSpec((D_MODEL, D_MODEL), const),
                  pl.BlockSpec((1, D_MODEL), const),
                  pl.BlockSpec((D_MODEL, FFN_TILE), lambda i, j: (0, j)),
                  pl.BlockSpec((D_MODEL, FFN_TILE), lambda i, j: (0, j + nf)),
                  pl.BlockSpec((FFN_TILE, D_MODEL), lambda i, j: (j, 0)),
                  pl.BlockSpec((1, D_MODEL), const)],
        out_specs=pl.BlockSpec((tm, D_MODEL), row),
        out_shape=jax.ShapeDtypeStruct((t, D_MODEL), F32),
        scratch_shapes=[pltpu.VMEM((tm, D_MODEL), F32),
                        pltpu.VMEM((tm, D_MODEL), BF16),
                        pltpu.VMEM((tm, D_MODEL), F32)],
        compiler_params=_params("parallel", "arbitrary"), name="mix_out_ffn",
    )(x, a, w_mix, g.reshape(1, D_MODEL), w_in, w_in, w_out, g_final.reshape(1, D_MODEL))


def _softmax_pv(scores, sink, values):
    m = sink
    for s in scores:
        m = jnp.maximum(m, jnp.max(s, axis=-1, keepdims=True))
    denom = jnp.exp(sink - m)
    out = None
    for s, v in zip(scores, values):
        p = jnp.exp(s - m)
        denom = denom + jnp.sum(p, axis=-1, keepdims=True)
        pv = jnp.dot(p.astype(BF16), v, preferred_element_type=F32)
        out = pv if out is None else out + pv
    return out / denom


def _attn_prompt_kernel(sink_ref, q_ref, kp_ref, kc_ref, vp_ref, vc_ref, bias_ref, o_ref):
    i = pl.program_id(1)
    sel = jnp.minimum(i, 1)
    q = (q_ref[0] * (HEAD_DIM ** -0.5)).astype(BF16)
    k = jnp.concatenate([kp_ref[0], kc_ref[0]], axis=0).astype(BF16)
    v = jnp.concatenate([vp_ref[0], vc_ref[0]], axis=0).astype(BF16)
    for h in range(N_HEADS):
        kv = h // GROUP
        qh = q[:, h * HEAD_DIM:(h + 1) * HEAD_DIM]
        kh = k[:, kv * HEAD_DIM:(kv + 1) * HEAD_DIM]
        vh = v[:, kv * HEAD_DIM:(kv + 1) * HEAD_DIM]
        s = lax.dot_general(qh, kh, NT_DIMS, preferred_element_type=F32) + bias_ref[sel, h]
        o = _softmax_pv([s], sink_ref[h], [vh])
        o_ref[0, :, h * HEAD_DIM:(h + 1) * HEAD_DIM] = o.astype(BF16)


def _attn_prompt(qkv, sinks, bias):
    b, l, _ = qkv.shape
    nb = l // WINDOW
    kcol, vcol = Q_DIM // KV_DIM, Q_DIM // KV_DIM + 1
    prev = lambda i: jnp.maximum(i - 1, 0)
    return pl.pallas_call(
        _attn_prompt_kernel,
        grid=(b, nb),
        in_specs=[pl.BlockSpec(memory_space=pltpu.SMEM),
                  pl.BlockSpec((1, WINDOW, Q_DIM), lambda b, i: (b, i, 0)),
                  pl.BlockSpec((1, WINDOW, KV_DIM), lambda b, i: (b, prev(i), kcol)),
                  pl.BlockSpec((1, WINDOW, KV_DIM), lambda b, i: (b, i, kcol)),
                  pl.BlockSpec((1, WINDOW, KV_DIM), lambda b, i: (b, prev(i), vcol)),
                  pl.BlockSpec((1, WINDOW, KV_DIM), lambda b, i: (b, i, vcol)),
                  pl.BlockSpec(bias.shape, lambda b, i: (0, 0, 0, 0))],
        out_specs=pl.BlockSpec((1, WINDOW, Q_DIM), lambda b, i: (b, i, 0)),
        out_shape=jax.ShapeDtypeStruct((b, l, Q_DIM), BF16),
        compiler_params=_params("parallel", "parallel"), name="attn_prompt",
    )(sinks, qkv, qkv, qkv, qkv, qkv, bias)


def _attn_sample_kernel(sink_ref, q_ref, kn_ref, vn_ref, kc_ref, vc_ref, bc_ref, bn_ref,
                        o_ref, ko_ref, vo_ref):
    l = q_ref.shape[1]
    q = (q_ref[0] * (HEAD_DIM ** -0.5)).astype(BF16)
    k_new, v_new = kn_ref[0], vn_ref[0]
    k_old, v_old = kc_ref[0], vc_ref[0]
    ko_ref[0, :WINDOW - l, :] = k_old[l:, :]
    ko_ref[0, WINDOW - l:, :] = k_new
    vo_ref[0, :WINDOW - l, :] = v_old[l:, :]
    vo_ref[0, WINDOW - l:, :] = v_new
    k_old, v_old = k_old.astype(BF16), v_old.astype(BF16)
    k_new, v_new = k_new.astype(BF16), v_new.astype(BF16)
    for h in range(N_HEADS):
        kv = h // GROUP
        hs = slice(h * HEAD_DIM, (h + 1) * HEAD_DIM)
        ks = slice(kv * HEAD_DIM, (kv + 1) * HEAD_DIM)
        qh = q[:, hs]
        s_old = lax.dot_general(qh, k_old[:, ks], NT_DIMS, preferred_element_type=F32) + bc_ref[h]
        s_new = lax.dot_general(qh, k_new[:, ks], NT_DIMS, preferred_element_type=F32) + bn_ref[h]
        o = _softmax_pv([s_old, s_new], sink_ref[h], [v_old[:, ks], v_new[:, ks]])
        o_ref[0, :, hs] = o.astype(BF16)


def _attn_sample(qkv, cache_k, cache_v, sinks, bias_old, bias_new):
    b, l, _ = qkv.shape
    kcol, vcol = Q_DIM // KV_DIM, Q_DIM // KV_DIM + 1
    cache_spec = pl.BlockSpec((1, WINDOW, KV_DIM), lambda b: (b, 0, 0))
    return pl.pallas_call(
        _attn_sample_kernel,
        grid=(b,),
        in_specs=[pl.BlockSpec(memory_space=pltpu.SMEM),
                  pl.BlockSpec((1, l, Q_DIM), lambda b: (b, 0, 0)),
                  pl.BlockSpec((1, l, KV_DIM), lambda b: (b, 0, kcol)),
                  pl.BlockSpec((1, l, KV_DIM), lambda b: (b, 0, vcol)),
                  cache_spec, cache_spec,
                  pl.BlockSpec(bias_old.shape, lambda b: (0, 0, 0)),
                  pl.BlockSpec(bias_new.shape, lambda b: (0, 0, 0))],
        out_specs=[pl.BlockSpec((1, l, Q_DIM), lambda b: (b, 0, 0)), cache_spec, cache_spec],
        out_shape=[jax.ShapeDtypeStruct((b, l, Q_DIM), BF16),
                   jax.ShapeDtypeStruct(cache_k.shape, F32),
                   jax.ShapeDtypeStruct(cache_v.shape, F32)],
        compiler_params=_params("parallel"), name="attn_sample",
    )(sinks, qkv, qkv, qkv, cache_k, cache_v, bias_old, bias_new)


def _attn_bias_tables(l_sample):
    slopes = jnp.exp2(-8.0 * jnp.arange(1, N_HEADS + 1, dtype=F32) / N_HEADS)[:, None, None]
    t = jnp.arange(WINDOW, dtype=jnp.int32)[:, None]
    s = jnp.arange(2 * WINDOW, dtype=jnp.int32)[None, :]
    dist = t + WINDOW - s
    valid = (dist >= 0) & (dist <= WINDOW)
    later = jnp.where(valid[None], -slopes * dist.astype(F32)[None], MASKED)
    first = jnp.where((s >= WINDOW)[None], later, MASKED)
    prompt = jnp.stack([first, later])
    tq = jnp.arange(l_sample, dtype=jnp.int32)[:, None]
    sk = jnp.arange(WINDOW + l_sample, dtype=jnp.int32)[None, :]
    dist = tq + WINDOW - sk
    valid = (dist >= 0) & (dist <= WINDOW)
    sample = jnp.where(valid[None], -slopes * dist.astype(F32)[None], MASKED)
    return prompt, sample[:, :, :WINDOW], sample[:, :, WINDOW:]


def _log_sigmoid(x):
    return jnp.minimum(x, 0.0) - jnp.log(1.0 + jnp.exp(-jnp.abs(x)))


def _cumsum_rows(x, tril):
    out = None
    for _ in range(3):
        part = x.astype(BF16)
        x = x - part.astype(F32)
        term = jnp.dot(tril, part, preferred_element_type=F32)
        out = term if out is None else out + term
    return out


def _gla_chunk(q, k, v, gd, wg, bg, tril, causal, state_qk):
    c = q.shape[0]
    pre = jnp.dot(gd.astype(BF16), wg, preferred_element_type=F32) + bg
    log_a = _log_sigmoid(pre) / GLA_GATE_NORMALIZER
    b = _cumsum_rows(log_a, tril)
    b_mid = b[c // 2:c // 2 + 1, :]
    b_end = b[c - 1:c, :]
    q = q * (GLA_DK_HEAD ** -0.5)
    qa = (q * jnp.exp(b - b_mid)).astype(BF16)
    ka = (k * jnp.exp(b_mid - b)).astype(BF16)
    a = lax.dot_general(qa, ka, NT_DIMS, preferred_element_type=F32)
    a = jnp.where(causal, a, 0.0)
    o = jnp.dot(a.astype(BF16), v.astype(BF16), preferred_element_type=F32)
    o = o + state_qk((q * jnp.exp(b)).astype(BF16))
    k_end = k * jnp.exp(b_end - b)
    return o, k_end, jnp.exp(b_end)


def _gla_gate_out(o, r, gn):
    return (_rms(o, gn) * _silu(r)).astype(BF16)


def _tri_masks(c):
    row = lax.broadcasted_iota(jnp.int32, (c, c), 0)
    col = lax.broadcasted_iota(jnp.int32, (c, c), 1)
    causal = row >= col
    return jnp.where(causal, 1.0, 0.0).astype(BF16), causal


def _gla_prompt_kernel(q_ref, k_ref, v_ref, r_ref, gd_ref, wg_ref, bg_ref, gn_ref,
                       o_ref, s_ref, st_ref):
    c = GLA_CHUNK
    n = q_ref.shape[1] // c
    tril, causal = _tri_masks(c)
    wg = wg_ref[...].astype(BF16)
    bg = bg_ref[...]
    gn = gn_ref[...]
    st_ref[...] = jnp.zeros_like(st_ref)

    def body(i, carry):
        sl = pl.ds(pl.multiple_of(i * c, c), c)
        st = st_ref[...]
        st_bf = st.astype(BF16)
        v = v_ref[0, sl, :]
        o, k_end, dec = _gla_chunk(
            q_ref[0, sl, :], k_ref[0, sl, :], v, gd_ref[0, sl, :], wg, bg, tril, causal,
            lambda qs: lax.dot_general(qs, st_bf, NT_DIMS, preferred_element_type=F32))
        upd = lax.dot_general(v.astype(BF16), k_end.astype(BF16), TN_DIMS,
                              preferred_element_type=F32)
        st_ref[...] = st * dec + upd
        o_ref[0, sl, :] = _gla_gate_out(o, r_ref[0, sl, :], gn)
        return carry

    lax.fori_loop(0, n, body, 0)
    s_ref[0, 0] = st_ref[...].T


def _gla_prompt(proj, gd, w_gate, b_gate, g_norm):
    b, l, _ = proj.shape
    kcol = GLA_DK // GLA_DK_HEAD
    vcol = 2 * GLA_DK // GLA_DV_HEAD
    rcol = vcol + GLA_DV // GLA_DV_HEAD
    return pl.pallas_call(
        _gla_prompt_kernel,
        grid=(b, GLA_HEADS),
        in_specs=[pl.BlockSpec((1, l, GLA_DK_HEAD), lambda b, h: (b, 0, h)),
                  pl.BlockSpec((1, l, GLA_DK_HEAD), lambda b, h: (b, 0, kcol + h)),
                  pl.BlockSpec((1, l, GLA_DV_HEAD), lambda b, h: (b, 0, vcol + h)),
                  pl.BlockSpec((1, l, GLA_DV_HEAD), lambda b, h: (b, 0, rcol + h)),
                  pl.BlockSpec((1, l, GLA_GATE_RANK), lambda b, h: (b, 0, 0)),
                  pl.BlockSpec((GLA_GATE_RANK, GLA_DK_HEAD), lambda b, h: (0, h)),
                  pl.BlockSpec((1, GLA_DK_HEAD), lambda b, h: (0, h)),
                  pl.BlockSpec((1, GLA_DV_HEAD), lambda b, h: (0, 0))],
        out_specs=[pl.BlockSpec((1, l, GLA_DV_HEAD), lambda b, h: (b, 0, h)),
                   pl.BlockSpec((1, 1, GLA_DK_HEAD, GLA_DV_HEAD), lambda b, h: (b, h, 0, 0))],
        out_shape=[jax.ShapeDtypeStruct((b, l, GLA_DV), BF16),
                   jax.ShapeDtypeStruct((b, GLA_HEADS, GLA_DK_HEAD, GLA_DV_HEAD), F32)],
        scratch_shapes=[pltpu.VMEM((GLA_DV_HEAD, GLA_DK_HEAD), F32)],
        compiler_params=_params("parallel", "parallel"), name="gla_prompt",
    )(proj, proj, proj, proj, gd, w_gate, b_gate.reshape(1, GLA_DK), g_norm.reshape(1, GLA_DV_HEAD))


def _gla_sample_kernel(q_ref, k_ref, v_ref, r_ref, gd_ref, s0_ref, wg_ref, bg_ref, gn_ref,
                       o_ref, s_ref):
    c = q_ref.shape[1]
    tril, causal = _tri_masks(c)
    gd = gd_ref[0]
    gn = gn_ref[...]
    for h in range(GLA_HEADS):
        ks = slice(h * GLA_DK_HEAD, (h + 1) * GLA_DK_HEAD)
        vs = slice(h * GLA_DV_HEAD, (h + 1) * GLA_DV_HEAD)
        s0 = s0_ref[0, h]
        s0_bf = s0.astype(BF16)
        v = v_ref[0, :, vs]
        o, k_end, dec = _gla_chunk(
            q_ref[0, :, ks], k_ref[0, :, ks], v, gd, wg_ref[:, ks].astype(BF16), bg_ref[:, ks],
            tril, causal, lambda qs: jnp.dot(qs, s0_bf, preferred_element_type=F32))
        upd = jnp.dot(k_end.T.astype(BF16), v.astype(BF16), preferred_element_type=F32)
        dec_col = jnp.broadcast_to(dec, (c, GLA_DK_HEAD)).T[:, 0:1]
        s_ref[0, h] = s0 * dec_col + upd
        o_ref[0, :, vs] = _gla_gate_out(o, r_ref[0, :, vs], gn)


def _gla_sample(proj, gd, state, w_gate, b_gate, g_norm):
    b, l, _ = proj.shape
    state_spec = pl.BlockSpec((1, GLA_HEADS, GLA_DK_HEAD, GLA_DV_HEAD), lambda b: (b, 0, 0, 0))
    return pl.pallas_call(
        _gla_sample_kernel,
        grid=(b,),
        in_specs=[pl.BlockSpec((1, l, GLA_DK), lambda b: (b, 0, 0)),
                  pl.BlockSpec((1, l, GLA_DK), lambda b: (b, 0, 1)),
                  pl.BlockSpec((1, l, GLA_DV), lambda b: (b, 0, 1)),
                  pl.BlockSpec((1, l, GLA_DV), lambda b: (b, 0, 2)),
                  pl.BlockSpec((1, l, GLA_GATE_RANK), lambda b: (b, 0, 0)),
                  state_spec,
                  pl.BlockSpec((GLA_GATE_RANK, GLA_DK), lambda b: (0, 0)),
                  pl.BlockSpec((1, GLA_DK), lambda b: (0, 0)),
                  pl.BlockSpec((1, GLA_DV_HEAD), lambda b: (0, 0))],
        out_specs=[pl.BlockSpec((1, l, GLA_DV), lambda b: (b, 0, 0)), state_spec],
        out_shape=[jax.ShapeDtypeStruct((b, l, GLA_DV), BF16),
                   jax.ShapeDtypeStruct(state.shape, F32)],
        compiler_params=_params("parallel"), name="gla_sample",
    )(proj, proj, proj, proj, gd, state, w_gate, b_gate.reshape(1, GLA_DK),
      g_norm.reshape(1, GLA_DV_HEAD))


def kernel(x_prompt, x_sample, cache_k, cache_v, state_gla, norm_mix, norm_ffn, norm_final,
           w_attn_in, w_attn_out, attn_sinks, w_gla_in, w_gla_gate_up, b_gla_gate, gla_out_norm,
           w_gla_out, w_ffn_in, w_ffn_out):
    bp, lp, _ = x_prompt.shape
    bs, ls, _ = x_sample.shape
    depth = norm_mix.shape[0]
    yp = x_prompt.reshape(bp * lp, D_MODEL)
    ys = x_sample.reshape(bs * ls, D_MODEL)
    tp = min(TOKEN_TILE, bp * lp)
    ts = min(TOKEN_TILE, bs * ls)
    bias_p, bias_old, bias_new = _attn_bias_tables(ls)
    n_main = 2 * GLA_DK + 2 * GLA_DV

    kp_l, vp_l, sp_l, ks_l, vs_l, ss_l = [], [], [], [], [], []
    for i in range(depth):
        j = i // 2
        if i % 2 == 0:
            w_in = [w_attn_in[j].astype(BF16)]
            w_mix = w_attn_out[j].astype(BF16)
            (qkv_p,) = _norm_matmul(yp, norm_mix[i], w_in, tp)
            (qkv_s,) = _norm_matmul(ys, norm_mix[i], w_in, ts)
            qkv_p = qkv_p.reshape(bp, lp, Q_DIM + 2 * KV_DIM)
            qkv_s = qkv_s.reshape(bs, ls, Q_DIM + 2 * KV_DIM)
            op = _attn_prompt(qkv_p, attn_sinks[j], bias_p)
            os_, kn, vn = _attn_sample(
                qkv_s, cache_k[j].reshape(bs, WINDOW, KV_DIM), cache_v[j].reshape(bs, WINDOW, KV_DIM),
                attn_sinks[j], bias_old, bias_new)
            kv_shape = (bp, WINDOW, N_KV_HEADS, HEAD_DIM)
            kp_l.append(qkv_p[:, lp - WINDOW:, Q_DIM:Q_DIM + KV_DIM].reshape(kv_shape))
            vp_l.append(qkv_p[:, lp - WINDOW:, Q_DIM + KV_DIM:].reshape(kv_shape))
            ks_l.append(kn.reshape(bs, WINDOW, N_KV_HEADS, HEAD_DIM))
            vs_l.append(vn.reshape(bs, WINDOW, N_KV_HEADS, HEAD_DIM))
        else:
            w_all = w_gla_in[j].astype(BF16)
            w_in = [w_all[:, :n_main], w_all[:, n_main:]]
            w_mix = w_gla_out[j].astype(BF16)
            proj_p, gd_p = _norm_matmul(yp, norm_mix[i], w_in, tp)
            proj_s, gd_s = _norm_matmul(ys, norm_mix[i], w_in, ts)
            op, sp = _gla_prompt(proj_p.reshape(bp, lp, n_main), gd_p.reshape(bp, lp, GLA_GATE_RANK),
                                 w_gla_gate_up[j], b_gla_gate[j], gla_out_norm[j])
            os_, sn = _gla_sample(proj_s.reshape(bs, ls, n_main), gd_s.reshape(bs, ls, GLA_GATE_RANK),
                                  state_gla[j], w_gla_gate_up[j], b_gla_gate[j], gla_out_norm[j])
            sp_l.append(sp)
            ss_l.append(sn)
        w_fi = w_ffn_in[i].astype(BF16)
        w_fo = w_ffn_out[i].astype(BF16)
        last = i == depth - 1
        yp = _mix_out_ffn(yp, op.reshape(bp * lp, D_MODEL), w_mix, norm_ffn[i], w_fi, w_fo,
                          norm_final, last, tp)
        ys = _mix_out_ffn(ys, os_.reshape(bs * ls, D_MODEL), w_mix, norm_ffn[i], w_fi, w_fo,
                          norm_final, last, ts)
    return (yp.reshape(bp, lp, D_MODEL), ys.reshape(bs, ls, D_MODEL),
            jnp.stack(kp_l), jnp.stack(vp_l), jnp.stack(sp_l),
            jnp.stack(ks_l), jnp.stack(vs_l), jnp.stack(ss_l))
```

```python
import functools

import jax
import jax.numpy as jnp
from jax import lax
from jax.experimental import pallas as pl
from jax.experimental.pallas import tpu as pltpu

F32 = jnp.float32
BF16 = jnp.bfloat16

D_MODEL = 1024
EPS = 1e-6
WINDOW = 128
HEAD_DIM = 64
N_HEADS = 16
N_KV_HEADS = 4
GROUP = 4
Q_DIM = 1024
KV_DIM = 256
GLA_HEADS = 4
GLA_DK = 512
GLA_DV = 1024
GLA_DK_HEAD = 128
GLA_DV_HEAD = 256
GLA_GATE_RANK = 16
GLA_GATE_NORMALIZER = 16.0
GLA_CHUNK = 64
D_FF = 2816
MASKED = -1e30

VMEM_LIMIT_BYTES = 52 * 1024 * 1024
LANES = 128
TOKEN_TILE = 1024
FFN_TILE = 256
GLA_GROUP = 4
SAMPLE_BATCH_TILE = 8
NEW_KEY_PAD = 16

NT_DIMS = (((1,), (1,)), ((), ()))
TN_DIMS = (((0,), (0,)), ((), ()))


def _params(*sem):
    return pltpu.CompilerParams(dimension_semantics=sem, vmem_limit_bytes=VMEM_LIMIT_BYTES)


def _rms(x, g):
    r = lax.rsqrt(jnp.mean(x * x, axis=-1, keepdims=True) + EPS)
    return x * r * g


def _silu(x):
    return x / (1.0 + jnp.exp(-x))


def _norm_matmul_kernel(n, x_ref, g_ref, *refs):
    h = _rms(x_ref[...], g_ref[...]).astype(BF16)
    for w_ref, o_ref in zip(refs[:n], refs[n:]):
        o_ref[...] = jnp.dot(h, w_ref[...], preferred_element_type=F32)


def _norm_matmul(x, g, ws, layers, tm):
    t = x.shape[0]
    lg, lw = layers
    in_specs = [pl.BlockSpec((tm, D_MODEL), lambda i: (i, 0)),
                pl.BlockSpec((None, 1, D_MODEL), lambda i: (lg, 0, 0))]
    in_specs += [pl.BlockSpec((None,) + w.shape[1:], lambda i: (lw, 0, 0)) for w in ws]
    out_specs = [pl.BlockSpec((tm, w.shape[2]), lambda i: (i, 0)) for w in ws]
    out_shape = [jax.ShapeDtypeStruct((t, w.shape[2]), F32) for w in ws]
    return pl.pallas_call(
        functools.partial(_norm_matmul_kernel, len(ws)),
        grid=(t // tm,), in_specs=in_specs, out_specs=out_specs, out_shape=out_shape,
        compiler_params=_params("parallel"), name="norm_matmul",
    )(x, g, *ws)


def _ffn_kernel(final_norm, x_ref, a_ref, wm_ref, g_ref, wg_ref, wu_ref, wo_ref, gf_ref,
                o_ref, y_ref, h_ref, acc_ref):
    j = pl.program_id(1)

    @pl.when(j == 0)
    def _():
        y = x_ref[...] + jnp.dot(a_ref[...], wm_ref[...], preferred_element_type=F32)
        y_ref[...] = y
        h_ref[...] = _rms(y, g_ref[...]).astype(BF16)
        acc_ref[...] = jnp.zeros_like(acc_ref)

    h = h_ref[...]
    gate = jnp.dot(h, wg_ref[...], preferred_element_type=F32)
    up = jnp.dot(h, wu_ref[...], preferred_element_type=F32)
    act = (_silu(gate) * up).astype(BF16)
    acc_ref[...] += jnp.dot(act, wo_ref[...], preferred_element_type=F32)

    @pl.when(j == pl.num_programs(1) - 1)
    def _():
        y = y_ref[...] + acc_ref[...]
        if final_norm:
            y = _rms(y, gf_ref[...])
        o_ref[...] = y


def _mix_out_ffn(x, a, w_mix, g, w_in, w_out, g_final, layers, final_norm, tm):
    t = x.shape[0]
    lm, lf = layers
    nf = D_FF // FFN_TILE
    row = lambda i, j: (i, 0)
    return pl.pallas_call(
        functools.partial(_ffn_kernel, final_norm),
        grid=(t // tm, nf),
        in_specs=[pl.BlockSpec((tm, D_MODEL), row),
                  pl.BlockSpec((tm, D_MODEL), row),
                  pl.BlockSpec((None, D_MODEL, D_MODEL), lambda i, j: (lm, 0, 0)),
                  pl.BlockSpec((None, 1, D_MODEL), lambda i, j: (lf, 0, 0)),
                  pl.BlockSpec((None, D_MODEL, FFN_TILE), lambda i, j: (lf, 0, j)),
                  pl.BlockSpec((None, D_MODEL, FFN_TILE), lambda i, j: (lf, 0, j + nf)),
                  pl.BlockSpec((None, FFN_TILE, D_MODEL), lambda i, j: (lf, j, 0)),
                  pl.BlockSpec((1, D_MODEL), lambda i, j: (0, 0))],
        out_specs=pl.BlockSpec((tm, D_MODEL), row),
        out_shape=jax.ShapeDtypeStruct((t, D_MODEL), F32),
        scratch_shapes=[pltpu.VMEM((tm, D_MODEL), F32),
                        pltpu.VMEM((tm, D_MODEL), BF16),
                        pltpu.VMEM((tm, D_MODEL), F32)],
        compiler_params=_params("parallel", "arbitrary"), name="mix_out_ffn",
    )(x, a, w_mix, g, w_in, w_in, w_out, g_final)


def _softmax_pv(scores, sink, values):
    m = sink
    for s in scores:
        m = jnp.maximum(m, jnp.max(s, axis=-1, keepdims=True))
    denom = jnp.exp(sink - m)
    out = None
    for s, v in zip(scores, values):
        p = jnp.exp(s - m)
        denom = denom + jnp.sum(p, axis=-1, keepdims=True)
        pv = jnp.dot(p.astype(BF16), v, preferred_element_type=F32)
        out = pv if out is None else out + pv
    return out / denom


def _attn_prompt_kernel(sink_ref, q_ref, kp_ref, kc_ref, vp_ref, vc_ref, bias_ref, o_ref):
    i = pl.program_id(1)
    sel = jnp.minimum(i, 1)
    q = (q_ref[0] * (HEAD_DIM ** -0.5)).astype(BF16)
    k = jnp.concatenate([kp_ref[0], kc_ref[0]], axis=0).astype(BF16)
    v = jnp.concatenate([vp_ref[0], vc_ref[0]], axis=0).astype(BF16)
    for h in range(N_HEADS):
        kv = h // GROUP
        qh = q[:, h * HEAD_DIM:(h + 1) * HEAD_DIM]
        kh = k[:, kv * HEAD_DIM:(kv + 1) * HEAD_DIM]
        vh = v[:, kv * HEAD_DIM:(kv + 1) * HEAD_DIM]
        s = lax.dot_general(qh, kh, NT_DIMS, preferred_element_type=F32) + bias_ref[sel, h]
        o = _softmax_pv([s], sink_ref[h], [vh])
        o_ref[0, :, h * HEAD_DIM:(h + 1) * HEAD_DIM] = o.astype(BF16)


def _attn_prompt(qkv, sinks, bias):
    b, l, _ = qkv.shape
    nb = l // WINDOW
    kcol, vcol = Q_DIM // KV_DIM, Q_DIM // KV_DIM + 1
    prev = lambda i: jnp.maximum(i - 1, 0)
    return pl.pallas_call(
        _attn_prompt_kernel,
        grid=(b, nb),
        in_specs=[pl.BlockSpec(memory_space=pltpu.SMEM),
                  pl.BlockSpec((1, WINDOW, Q_DIM), lambda b, i: (b, i, 0)),
                  pl.BlockSpec((1, WINDOW, KV_DIM), lambda b, i: (b, prev(i), kcol)),
                  pl.BlockSpec((1, WINDOW, KV_DIM), lambda b, i: (b, i, kcol)),
                  pl.BlockSpec((1, WINDOW, KV_DIM), lambda b, i: (b, prev(i), vcol)),
                  pl.BlockSpec((1, WINDOW, KV_DIM), lambda b, i: (b, i, vcol)),
                  pl.BlockSpec(bias.shape, lambda b, i: (0, 0, 0, 0))],
        out_specs=pl.BlockSpec((1, WINDOW, Q_DIM), lambda b, i: (b, i, 0)),
        out_shape=jax.ShapeDtypeStruct((b, l, Q_DIM), BF16),
        compiler_params=_params("parallel", "parallel"), name="attn_prompt",
    )(sinks, qkv, qkv, qkv, qkv, qkv, bias)


def _head_lane_block(kv):
    return (kv * HEAD_DIM) // LANES, ((kv * HEAD_DIM) % LANES) // HEAD_DIM


def _attn_sample_kernel(q_ref, kn_ref, vn_ref, kt_ref, vt_ref, bo_ref, bn_ref, sink_ref,
                        o_ref, kto_ref, vto_ref):
    bt, l, _ = q_ref.shape
    low = lax.broadcasted_iota(jnp.int32, (l, LANES), 1) < HEAD_DIM
    keep = lax.broadcasted_iota(jnp.int32, (KV_DIM, WINDOW), 1) < WINDOW - l
    zero_block = jnp.zeros((l, LANES), F32)
    pad_rows = jnp.zeros((WINDOW - l, KV_DIM), F32)
    pad_keys = jnp.zeros((NEW_KEY_PAD - l, KV_DIM), F32)
    bias_old = bo_ref[...]
    bias_new = bn_ref[...]
    sink = sink_ref[...]

    def body(b, carry):
        q = q_ref[b] * (HEAD_DIM ** -0.5)
        pieces = []
        for h in range(N_HEADS):
            blk, half = _head_lane_block(h // GROUP)
            src = q[:, (h // 2) * LANES:(h // 2 + 1) * LANES]
            if h % 2 != half:
                src = pltpu.roll(src, HEAD_DIM, axis=1)
            src = jnp.where(low if half == 0 else jnp.logical_not(low), src, 0.0)
            pieces.append(jnp.concatenate(
                [src, zero_block] if blk == 0 else [zero_block, src], axis=1))
        qt = jnp.concatenate(pieces, axis=0).astype(BF16)

        k_old, v_old = kt_ref[b], vt_ref[b]
        k_new, v_new = kn_ref[b], vn_ref[b]
        k_pad = jnp.concatenate([k_new, pad_keys], axis=0).astype(BF16)
        v_pad = jnp.concatenate([v_new, pad_keys], axis=0).astype(BF16)
        s_old = jnp.dot(qt, k_old.astype(BF16), preferred_element_type=F32) + bias_old
        s_new = lax.dot_general(qt, k_pad, NT_DIMS, preferred_element_type=F32) + bias_new
        m = jnp.maximum(sink, jnp.maximum(jnp.max(s_old, axis=-1, keepdims=True),
                                          jnp.max(s_new, axis=-1, keepdims=True)))
        p_old = jnp.exp(s_old - m)
        p_new = jnp.exp(s_new - m)
        denom = (jnp.exp(sink - m) + jnp.sum(p_old, axis=-1, keepdims=True)
                 + jnp.sum(p_new, axis=-1, keepdims=True))
        o = lax.dot_general(p_old.astype(BF16), v_old.astype(BF16), NT_DIMS,
                            preferred_element_type=F32)
        o = o + jnp.dot(p_new.astype(BF16), v_pad, preferred_element_type=F32)
        o = o / denom
        cols = []
        for c in range(N_HEADS // 2):
            parts = []
            for h in (2 * c, 2 * c + 1):
                blk, half = _head_lane_block(h // GROUP)
                piece = o[h * l:(h + 1) * l, blk * LANES:(blk + 1) * LANES]
                if h % 2 != half:
                    piece = pltpu.roll(piece, HEAD_DIM, axis=1)
                parts.append(piece)
            cols.append(jnp.where(low, parts[0], parts[1]))
        o_ref[b] = jnp.concatenate(cols, axis=1).astype(BF16)

        for old, new, out_ref in ((k_old, k_new, kto_ref), (v_old, v_new, vto_ref)):
            shifted = pltpu.roll(old, WINDOW - l, axis=1)
            placed = jnp.concatenate([pad_rows, new], axis=0).T
            out_ref[b] = jnp.where(keep, shifted, placed)
        return carry

    lax.fori_loop(0, bt, body, 0)


def _attn_sample(qkv, cache_kt, cache_vt, layer, bias_old, bias_new, sink_col):
    b, l, _ = qkv.shape
    bt = min(SAMPLE_BATCH_TILE, b)
    kcol, vcol = Q_DIM // KV_DIM, Q_DIM // KV_DIM + 1
    const = lambda b: (0, 0)
    return pl.pallas_call(
        _attn_sample_kernel,
        grid=(b // bt,),
        in_specs=[pl.BlockSpec((bt, l, Q_DIM), lambda b: (b, 0, 0)),
                  pl.BlockSpec((bt, l, KV_DIM), lambda b: (b, 0, kcol)),
                  pl.BlockSpec((bt, l, KV_DIM), lambda b: (b, 0, vcol)),
                  pl.BlockSpec((None, bt, KV_DIM, WINDOW), lambda b: (layer, b, 0, 0)),
                  pl.BlockSpec((None, bt, KV_DIM, WINDOW), lambda b: (layer, b, 0, 0)),
                  pl.BlockSpec(bias_old.shape, const),
                  pl.BlockSpec(bias_new.shape, const),
                  pl.BlockSpec(sink_col.shape, const)],
        out_specs=[pl.BlockSpec((bt, l, Q_DIM), lambda b: (b, 0, 0)),
                   pl.BlockSpec((bt, KV_DIM, WINDOW), lambda b: (b, 0, 0)),
                   pl.BlockSpec((bt, KV_DIM, WINDOW), lambda b: (b, 0, 0))],
        out_shape=[jax.ShapeDtypeStruct((b, l, Q_DIM), BF16),
                   jax.ShapeDtypeStruct((b, KV_DIM, WINDOW), F32),
                   jax.ShapeDtypeStruct((b, KV_DIM, WINDOW), F32)],
        compiler_params=_params("parallel"), name="attn_sample",
    )(qkv, qkv, qkv, cache_kt, cache_vt, bias_old, bias_new, sink_col)


def _attn_bias_tables(l_sample):
    slopes = jnp.exp2(-8.0 * jnp.arange(1, N_HEADS + 1, dtype=F32) / N_HEADS)[:, None, None]
    t = jnp.arange(WINDOW, dtype=jnp.int32)[:, None]
    s = jnp.arange(2 * WINDOW, dtype=jnp.int32)[None, :]
    dist = t + WINDOW - s
    valid = (dist >= 0) & (dist <= WINDOW)
    later = jnp.where(valid[None], -slopes * dist.astype(F32)[None], MASKED)
    first = jnp.where((s >= WINDOW)[None], later, MASKED)
    prompt = jnp.stack([first, later])
    tq = jnp.arange(l_sample, dtype=jnp.int32)[:, None]
    sk = jnp.arange(WINDOW + NEW_KEY_PAD, dtype=jnp.int32)[None, :]
    dist = tq + WINDOW - sk
    valid = (dist >= 0) & (dist <= WINDOW) & (sk < WINDOW + l_sample)
    sample = jnp.where(valid[None], -slopes * dist.astype(F32)[None], MASKED)
    sample = sample.reshape(N_HEADS * l_sample, WINDOW + NEW_KEY_PAD)
    return prompt, sample[:, :WINDOW], sample[:, WINDOW:]


def _log_sigmoid(x):
    return jnp.minimum(x, 0.0) - jnp.log(1.0 + jnp.exp(-jnp.abs(x)))


def _cumsum_matmul(x, tril):
    out = None
    for _ in range(3):
        part = x.astype(BF16)
        x = x - part.astype(F32)
        term = jnp.dot(tril, part, preferred_element_type=F32)
        out = term if out is None else out + term
    return out


def _gla_gate_out(o, r, gn):
    return (_rms(o, gn) * _silu(r)).astype(BF16)


def _gla_prompt_kernel(q_ref, k_ref, v_ref, r_ref, gd_ref, wg_ref, bg_ref, gn_ref,
                       o_ref, s_ref, st_ref):
    c = GLA_CHUNK
    rows = GLA_GROUP * c
    n_groups = q_ref.shape[1] // rows
    row = lax.broadcasted_iota(jnp.int32, (rows, rows), 0)
    col = lax.broadcasted_iota(jnp.int32, (rows, rows), 1)
    causal = (row >= col) & (col >= row - jnp.bitwise_and(row, c - 1))
    tril = jnp.where(causal, 1.0, 0.0).astype(BF16)
    wg = wg_ref[...].astype(BF16)
    bg = bg_ref[...]
    gn = gn_ref[...]
    st_ref[...] = jnp.zeros_like(st_ref)

    def body(g, carry):
        sl = pl.ds(pl.multiple_of(g * rows, rows), rows)
        q = q_ref[0, sl, :] * (GLA_DK_HEAD ** -0.5)
        k = k_ref[0, sl, :]
        v_bf = v_ref[0, sl, :].astype(BF16)
        pre = jnp.dot(gd_ref[0, sl, :].astype(BF16), wg, preferred_element_type=F32) + bg
        b = _cumsum_matmul(_log_sigmoid(pre) / GLA_GATE_NORMALIZER, tril)
        per_chunk = lambda r0: jnp.concatenate(
            [jnp.broadcast_to(b[j * c + r0:j * c + r0 + 1, :], (c, GLA_DK_HEAD))
             for j in range(GLA_GROUP)], axis=0)
        b_mid = per_chunk(c // 2)
        b_end = per_chunk(c - 1)
        qa = (q * jnp.exp(b - b_mid)).astype(BF16)
        ka = (k * jnp.exp(b_mid - b)).astype(BF16)
        a = lax.dot_general(qa, ka, NT_DIMS, preferred_element_type=F32)
        a = jnp.where(causal, a, 0.0).astype(BF16)
        o_intra = jnp.dot(a, v_bf, preferred_element_type=F32)
        qb = (q * jnp.exp(b)).astype(BF16)
        k_end = (k * jnp.exp(b_end - b)).astype(BF16)
        st = st_ref[...]
        o_inter = []
        for j in range(GLA_GROUP):
            rs = slice(j * c, (j + 1) * c)
            o_inter.append(lax.dot_general(qb[rs], st.astype(BF16), NT_DIMS,
                                           preferred_element_type=F32))
            upd = lax.dot_general(v_bf[rs], k_end[rs], TN_DIMS, preferred_element_type=F32)
            st = st * jnp.exp(b[(j + 1) * c - 1:(j + 1) * c, :]) + upd
        st_ref[...] = st
        o = o_intra + jnp.concatenate(o_inter, axis=0)
        o_ref[0, sl, :] = _gla_gate_out(o, r_ref[0, sl, :], gn)
        return carry

    lax.fori_loop(0, n_groups, body, 0)
    s_ref[0, 0] = st_ref[...].T


def _gla_prompt(proj, gd, w_gate, b_gate, g_norm, layer):
    b, l, _ = proj.shape
    kcol = GLA_DK // GLA_DK_HEAD
    vcol = 2 * GLA_DK // GLA_DV_HEAD
    rcol = vcol + GLA_DV // GLA_DV_HEAD
    return pl.pallas_call(
        _gla_prompt_kernel,
        grid=(b, GLA_HEADS),
        in_specs=[pl.BlockSpec((1, l, GLA_DK_HEAD), lambda b, h: (b, 0, h)),
                  pl.BlockSpec((1, l, GLA_DK_HEAD), lambda b, h: (b, 0, kcol + h)),
                  pl.BlockSpec((1, l, GLA_DV_HEAD), lambda b, h: (b, 0, vcol + h)),
                  pl.BlockSpec((1, l, GLA_DV_HEAD), lambda b, h: (b, 0, rcol + h)),
                  pl.BlockSpec((1, l, GLA_GATE_RANK), lambda b, h: (b, 0, 0)),
                  pl.BlockSpec((None, GLA_GATE_RANK, GLA_DK_HEAD), lambda b, h: (layer, 0, h)),
                  pl.BlockSpec((None, 1, GLA_DK_HEAD), lambda b, h: (layer, 0, h)),
                  pl.BlockSpec((None, 1, GLA_DV_HEAD), lambda b, h: (layer, 0, 0))],
        out_specs=[pl.BlockSpec((1, l, GLA_DV_HEAD), lambda b, h: (b, 0, h)),
                   pl.BlockSpec((1, 1, GLA_DK_HEAD, GLA_DV_HEAD), lambda b, h: (b, h, 0, 0))],
        out_shape=[jax.ShapeDtypeStruct((b, l, GLA_DV), BF16),
                   jax.ShapeDtypeStruct((b, GLA_HEADS, GLA_DK_HEAD, GLA_DV_HEAD), F32)],
        scratch_shapes=[pltpu.VMEM((GLA_DV_HEAD, GLA_DK_HEAD), F32)],
        compiler_params=_params("parallel", "parallel"), name="gla_prompt",
    )(proj, proj, proj, proj, gd, w_gate, b_gate, g_norm)


def _cumsum_sublanes(x):
    n = x.shape[0]
    row = lax.broadcasted_iota(jnp.int32, x.shape, 0)
    shift = 1
    while shift < n:
        x = x + jnp.where(row >= shift, pltpu.roll(x, shift, axis=0), 0.0)
        shift *= 2
    return x


def _gla_sample_kernel(q_ref, k_ref, v_ref, r_ref, gd_ref, s0_ref, wg_ref, bg_ref, gn_ref,
                       o_ref, s_ref):
    bt, c, _ = q_ref.shape
    row = lax.broadcasted_iota(jnp.int32, (c, c), 0)
    col = lax.broadcasted_iota(jnp.int32, (c, c), 1)
    causal = row >= col
    wg = wg_ref[...].astype(BF16)
    bg = bg_ref[...]
    gn = gn_ref[...]

    def body(i, carry):
        q = q_ref[i] * (GLA_DK_HEAD ** -0.5)
        k = k_ref[i]
        v = v_ref[i]
        r = r_ref[i]
        pre = jnp.dot(gd_ref[i].astype(BF16), wg, preferred_element_type=F32) + bg
        b = _cumsum_sublanes(_log_sigmoid(pre) / GLA_GATE_NORMALIZER)
        b_mid = b[c // 2:c // 2 + 1, :]
        b_end = b[c - 1:c, :]
        qa = q * jnp.exp(b - b_mid)
        ka = k * jnp.exp(b_mid - b)
        qb = (q * jnp.exp(b)).astype(BF16)
        k_end_t = (k * jnp.exp(b_end - b)).T
        dec_col = jnp.broadcast_to(jnp.exp(b_end), (c, GLA_DK)).T[:, 0:1]
        for h in range(GLA_HEADS):
            ks = slice(h * GLA_DK_HEAD, (h + 1) * GLA_DK_HEAD)
            vs = slice(h * GLA_DV_HEAD, (h + 1) * GLA_DV_HEAD)
            s0 = s0_ref[i, h]
            a = lax.dot_general(qa[:, ks], ka[:, ks], NT_DIMS, preferred_element_type=F32)
            a = jnp.where(causal, a, 0.0)
            o = jnp.dot(a, v[:, vs], preferred_element_type=F32)
            o = o + jnp.dot(qb[:, ks], s0.astype(BF16), preferred_element_type=F32)
            upd = jnp.dot(k_end_t[ks, :], v[:, vs], preferred_element_type=F32)
            s_ref[i, h] = s0 * dec_col[ks, :] + upd
            o_ref[i, :, vs] = _gla_gate_out(o, r[:, vs], gn)
        return carry

    lax.fori_loop(0, bt, body, 0)


def _gla_sample(proj, gd, state, w_gate, b_gate, g_norm, layer):
    b, l, _ = proj.shape
    bt = min(SAMPLE_BATCH_TILE, b)
    state_block = (bt, GLA_HEADS, GLA_DK_HEAD, GLA_DV_HEAD)
    return pl.pallas_call(
        _gla_sample_kernel,
        grid=(b // bt,),
        in_specs=[pl.BlockSpec((bt, l, GLA_DK), lambda b: (b, 0, 0)),
                  pl.BlockSpec((bt, l, GLA_DK), lambda b: (b, 0, 1)),
                  pl.BlockSpec((bt, l, GLA_DV), lambda b: (b, 0, 1)),
                  pl.BlockSpec((bt, l, GLA_DV), lambda b: (b, 0, 2)),
                  pl.BlockSpec((bt, l, GLA_GATE_RANK), lambda b: (b, 0, 0)),
                  pl.BlockSpec((None,) + state_block, lambda b: (layer, b, 0, 0, 0)),
                  pl.BlockSpec((None, GLA_GATE_RANK, GLA_DK), lambda b: (layer, 0, 0)),
                  pl.BlockSpec((None, 1, GLA_DK), lambda b: (layer, 0, 0)),
                  pl.BlockSpec((None, 1, GLA_DV_HEAD), lambda b: (layer, 0, 0))],
        out_specs=[pl.BlockSpec((bt, l, GLA_DV), lambda b: (b, 0, 0)),
                   pl.BlockSpec(state_block, lambda b: (b, 0, 0, 0))],
        out_shape=[jax.ShapeDtypeStruct((b, l, GLA_DV), BF16),
                   jax.ShapeDtypeStruct(state.shape[1:], F32)],
        compiler_params=_params("parallel"), name="gla_sample",
    )(proj, proj, proj, proj, gd, state, w_gate, b_gate, g_norm)


def kernel(x_prompt, x_sample, cache_k, cache_v, state_gla, norm_mix, norm_ffn, norm_final,
           w_attn_in, w_attn_out, attn_sinks, w_gla_in, w_gla_gate_up, b_gla_gate, gla_out_norm,
           w_gla_out, w_ffn_in, w_ffn_out):
    bp, lp, _ = x_prompt.shape
    bs, ls, _ = x_sample.shape
    depth = norm_mix.shape[0]
    n_attn = cache_k.shape[0]
    yp = x_prompt.reshape(bp * lp, D_MODEL)
    ys = x_sample.reshape(bs * ls, D_MODEL)
    tp = min(TOKEN_TILE, bp * lp)
    ts = min(TOKEN_TILE, bs * ls)
    bias_p, bias_old, bias_new = _attn_bias_tables(ls)
    n_main = 2 * GLA_DK + 2 * GLA_DV

    g_mix = norm_mix.reshape(depth, 1, D_MODEL)
    g_ffn = norm_ffn.reshape(depth, 1, D_MODEL)
    g_fin = norm_final.reshape(1, D_MODEL)
    w_attn_in_bf = w_attn_in.astype(BF16)
    w_attn_out_bf = w_attn_out.astype(BF16)
    w_gla_main_bf = w_gla_in[:, :, :n_main].astype(BF16)
    w_gla_gd_bf = w_gla_in[:, :, n_main:].astype(BF16)
    w_gla_out_bf = w_gla_out.astype(BF16)
    w_ffn_in_bf = w_ffn_in.astype(BF16)
    w_ffn_out_bf = w_ffn_out.astype(BF16)
    b_gate = b_gla_gate.reshape(-1, 1, GLA_DK)
    g_gla = gla_out_norm.reshape(-1, 1, GLA_DV_HEAD)
    cache_kt = jnp.transpose(cache_k, (0, 1, 3, 4, 2)).reshape(n_attn, bs, KV_DIM, WINDOW)
    cache_vt = jnp.transpose(cache_v, (0, 1, 3, 4, 2)).reshape(n_attn, bs, KV_DIM, WINDOW)

    kp_l, vp_l, sp_l, ks_l, vs_l, ss_l = [], [], [], [], [], []
    for i in range(depth):
        j = i // 2
        if i % 2 == 0:
            w_mix = w_attn_out_bf
            (qkv_p,) = _norm_matmul(yp, g_mix, [w_attn_in_bf], (i, j), tp)
            (qkv_s,) = _norm_matmul(ys, g_mix, [w_attn_in_bf], (i, j), ts)
            qkv_p = qkv_p.reshape(bp, lp, Q_DIM + 2 * KV_DIM)
            qkv_s = qkv_s.reshape(bs, ls, Q_DIM + 2 * KV_DIM)
            op = _attn_prompt(qkv_p, attn_sinks[j], bias_p)
            sink_col = jnp.repeat(attn_sinks[j], ls).reshape(N_HEADS * ls, 1)
            os_, kn, vn = _attn_sample(qkv_s, cache_kt, cache_vt, j, bias_old, bias_new, sink_col)
            kv_shape = (bp, WINDOW, N_KV_HEADS, HEAD_DIM)
            kp_l.append(qkv_p[:, lp - WINDOW:, Q_DIM:Q_DIM + KV_DIM].reshape(kv_shape))
            vp_l.append(qkv_p[:, lp - WINDOW:, Q_DIM + KV_DIM:].reshape(kv_shape))
            ks_l.append(kn)
            vs_l.append(vn)
        else:
            w_mix = w_gla_out_bf
            proj_p, gd_p = _norm_matmul(yp, g_mix, [w_gla_main_bf, w_gla_gd_bf], (i, j), tp)
            proj_s, gd_s = _norm_matmul(ys, g_mix, [w_gla_main_bf, w_gla_gd_bf], (i, j), ts)
            op, sp = _gla_prompt(proj_p.reshape(bp, lp, n_main), gd_p.reshape(bp, lp, GLA_GATE_RANK),
                                 w_gla_gate_up, b_gate, g_gla, j)
            os_, sn = _gla_sample(proj_s.reshape(bs, ls, n_main), gd_s.reshape(bs, ls, GLA_GATE_RANK),
                                  state_gla, w_gla_gate_up, b_gate, g_gla, j)
            sp_l.append(sp)
            ss_l.append(sn)
        last = i == depth - 1
        yp = _mix_out_ffn(yp, op.reshape(bp * lp, D_MODEL), w_mix, g_ffn, w_ffn_in_bf, w_ffn_out_bf,
                          g_fin, (j, i), last, tp)
        ys = _mix_out_ffn(ys, os_.reshape(bs * ls, D_MODEL), w_mix, g_ffn, w_ffn_in_bf, w_ffn_out_bf,
                          g_fin, (j, i), last, ts)

    def window_major(xt):
        return jnp.transpose(xt.reshape(n_attn, bs, N_KV_HEADS, HEAD_DIM, WINDOW), (0, 1, 4, 2, 3))

    return (yp.reshape(bp, lp, D_MODEL), ys.reshape(bs, ls, D_MODEL),
            jnp.stack(kp_l), jnp.stack(vp_l), jnp.stack(sp_l),
            window_major(jnp.stack(ks_l)), window_major(jnp.stack(vs_l)), jnp.stack(ss_l))
```

```python
import functools

import jax
import jax.numpy as jnp
from jax import lax
from jax.experimental import pallas as pl
from jax.experimental.pallas import tpu as pltpu

F32 = jnp.float32
BF16 = jnp.bfloat16

D_MODEL = 1024
EPS = 1e-6
WINDOW = 128
HEAD_DIM = 64
N_HEADS = 16
N_KV_HEADS = 4
GROUP = 4
Q_DIM = 1024
KV_DIM = 256
GLA_HEADS = 4
GLA_DK = 512
GLA_DV = 1024
GLA_DK_HEAD = 128
GLA_DV_HEAD = 256
GLA_GATE_RANK = 16
GLA_GATE_NORMALIZER = 16.0
GLA_CHUNK = 64
D_FF = 2816
MASKED = -1e30

VMEM_LIMIT_BYTES = 52 * 1024 * 1024
LANES = 128
TOKEN_TILE = 1024
FFN_TOKEN_TILE = 512
FFN_TILE = 256
GLA_GROUP = 4
SAMPLE_BATCH_TILE = 8
NEW_KEY_PAD = 16

NT_DIMS = (((1,), (1,)), ((), ()))
TN_DIMS = (((0,), (0,)), ((), ()))


def _params(*sem):
    return pltpu.CompilerParams(dimension_semantics=sem, vmem_limit_bytes=VMEM_LIMIT_BYTES)


def _rms(x, g):
    r = lax.rsqrt(jnp.mean(x * x, axis=-1, keepdims=True) + EPS)
    return x * r * g


def _silu(x):
    return x / (1.0 + jnp.exp(-x))


def _norm_matmul_kernel(n, x_ref, g_ref, *refs):
    h = _rms(x_ref[...], g_ref[...]).astype(BF16)
    for w_ref, o_ref in zip(refs[:n], refs[n:]):
        o_ref[...] = jnp.dot(h, w_ref[...], preferred_element_type=F32)


def _norm_matmul(x, g, ws, layers, tm):
    t = x.shape[0]
    lg, lw = layers
    in_specs = [pl.BlockSpec((tm, D_MODEL), lambda i: (i, 0)),
                pl.BlockSpec((None, 1, D_MODEL), lambda i: (lg, 0, 0))]
    in_specs += [pl.BlockSpec((None,) + w.shape[1:], lambda i: (lw, 0, 0)) for w in ws]
    out_specs = [pl.BlockSpec((tm, w.shape[2]), lambda i: (i, 0)) for w in ws]
    out_shape = [jax.ShapeDtypeStruct((t, w.shape[2]), F32) for w in ws]
    return pl.pallas_call(
        functools.partial(_norm_matmul_kernel, len(ws)),
        grid=(t // tm,), in_specs=in_specs, out_specs=out_specs, out_shape=out_shape,
        compiler_params=_params("parallel"), name="norm_matmul",
    )(x, g, *ws)


def _ffn_kernel(final_norm, x_ref, a_ref, wm_ref, g_ref, wi_ref, wo_ref, gf_ref, o_ref, act_ref):
    y = x_ref[...] + jnp.dot(a_ref[...], wm_ref[...], preferred_element_type=F32)
    h = _rms(y, g_ref[...]).astype(BF16)
    for j in range(D_FF // FFN_TILE):
        gate = jnp.dot(h, wi_ref[:, j * FFN_TILE:(j + 1) * FFN_TILE], preferred_element_type=F32)
        up = jnp.dot(h, wi_ref[:, D_FF + j * FFN_TILE:D_FF + (j + 1) * FFN_TILE],
                     preferred_element_type=F32)
        act_ref[:, j * FFN_TILE:(j + 1) * FFN_TILE] = (_silu(gate) * up).astype(BF16)
    y = y + jnp.dot(act_ref[...], wo_ref[...], preferred_element_type=F32)
    if final_norm:
        y = _rms(y, gf_ref[...])
    o_ref[...] = y


def _resident(block_shape, index_map):
    return pl.BlockSpec(block_shape, index_map, pipeline_mode=pl.Buffered(1))


def _mix_out_ffn(x, a, w_mix, g, w_in, w_out, g_final, layers, final_norm, tm):
    t = x.shape[0]
    lm, lf = layers
    row = lambda i: (i, 0)
    return pl.pallas_call(
        functools.partial(_ffn_kernel, final_norm),
        grid=(t // tm,),
        in_specs=[pl.BlockSpec((tm, D_MODEL), row),
                  pl.BlockSpec((tm, D_MODEL), row),
                  _resident((None, D_MODEL, D_MODEL), lambda i: (lm, 0, 0)),
                  _resident((None, 1, D_MODEL), lambda i: (lf, 0, 0)),
                  _resident((None, D_MODEL, 2 * D_FF), lambda i: (lf, 0, 0)),
                  _resident((None, D_FF, D_MODEL), lambda i: (lf, 0, 0)),
                  _resident((1, D_MODEL), lambda i: (0, 0))],
        out_specs=pl.BlockSpec((tm, D_MODEL), row),
        out_shape=jax.ShapeDtypeStruct((t, D_MODEL), F32),
        scratch_shapes=[pltpu.VMEM((tm, D_FF), BF16)],
        compiler_params=_params("parallel"), name="mix_out_ffn",
    )(x, a, w_mix, g, w_in, w_out, g_final)


def _softmax_pv(scores, sink, values):
    m = sink
    for s in scores:
        m = jnp.maximum(m, jnp.max(s, axis=-1, keepdims=True))
    denom = jnp.exp(sink - m)
    out = None
    for s, v in zip(scores, values):
        p = jnp.exp(s - m)
        denom = denom + jnp.sum(p, axis=-1, keepdims=True)
        pv = jnp.dot(p.astype(BF16), v, preferred_element_type=F32)
        out = pv if out is None else out + pv
    return out / denom


def _attn_prompt_kernel(sink_ref, q_ref, kp_ref, kc_ref, vp_ref, vc_ref, bias_ref, o_ref):
    i = pl.program_id(1)
    sel = jnp.minimum(i, 1)
    q = (q_ref[0] * (HEAD_DIM ** -0.5)).astype(BF16)
    k = jnp.concatenate([kp_ref[0], kc_ref[0]], axis=0).astype(BF16)
    v = jnp.concatenate([vp_ref[0], vc_ref[0]], axis=0).astype(BF16)
    for h in range(N_HEADS):
        kv = h // GROUP
        qh = q[:, h * HEAD_DIM:(h + 1) * HEAD_DIM]
        kh = k[:, kv * HEAD_DIM:(kv + 1) * HEAD_DIM]
        vh = v[:, kv * HEAD_DIM:(kv + 1) * HEAD_DIM]
        s = lax.dot_general(qh, kh, NT_DIMS, preferred_element_type=F32) + bias_ref[sel, h]
        o = _softmax_pv([s], sink_ref[h], [vh])
        o_ref[0, :, h * HEAD_DIM:(h + 1) * HEAD_DIM] = o.astype(BF16)


def _attn_prompt(qkv, sinks, bias):
    b, l, _ = qkv.shape
    nb = l // WINDOW
    kcol, vcol = Q_DIM // KV_DIM, Q_DIM // KV_DIM + 1
    prev = lambda i: jnp.maximum(i - 1, 0)
    return pl.pallas_call(
        _attn_prompt_kernel,
        grid=(b, nb),
        in_specs=[pl.BlockSpec(memory_space=pltpu.SMEM),
                  pl.BlockSpec((1, WINDOW, Q_DIM), lambda b, i: (b, i, 0)),
                  pl.BlockSpec((1, WINDOW, KV_DIM), lambda b, i: (b, prev(i), kcol)),
                  pl.BlockSpec((1, WINDOW, KV_DIM), lambda b, i: (b, i, kcol)),
                  pl.BlockSpec((1, WINDOW, KV_DIM), lambda b, i: (b, prev(i), vcol)),
                  pl.BlockSpec((1, WINDOW, KV_DIM), lambda b, i: (b, i, vcol)),
                  pl.BlockSpec(bias.shape, lambda b, i: (0, 0, 0, 0))],
        out_specs=pl.BlockSpec((1, WINDOW, Q_DIM), lambda b, i: (b, i, 0)),
        out_shape=jax.ShapeDtypeStruct((b, l, Q_DIM), BF16),
        compiler_params=_params("parallel", "parallel"), name="attn_prompt",
    )(sinks, qkv, qkv, qkv, qkv, qkv, bias)


def _head_lane_block(kv):
    return (kv * HEAD_DIM) // LANES, ((kv * HEAD_DIM) % LANES) // HEAD_DIM


def _attn_sample_kernel(q_ref, kn_ref, vn_ref, kt_ref, vt_ref, bo_ref, bn_ref, sink_ref,
                        o_ref, kto_ref, vto_ref):
    bt, l, _ = q_ref.shape
    low = lax.broadcasted_iota(jnp.int32, (l, LANES), 1) < HEAD_DIM
    keep = lax.broadcasted_iota(jnp.int32, (KV_DIM, WINDOW), 1) < WINDOW - l
    zero_block = jnp.zeros((l, LANES), F32)
    pad_rows = jnp.zeros((WINDOW - l, KV_DIM), F32)
    pad_keys = jnp.zeros((NEW_KEY_PAD - l, KV_DIM), F32)
    bias_old = bo_ref[...]
    bias_new = bn_ref[...]
    sink = sink_ref[...]

    def body(b, carry):
        q = q_ref[b] * (HEAD_DIM ** -0.5)
        pieces = []
        for h in range(N_HEADS):
            blk, half = _head_lane_block(h // GROUP)
            src = q[:, (h // 2) * LANES:(h // 2 + 1) * LANES]
            if h % 2 != half:
                src = pltpu.roll(src, HEAD_DIM, axis=1)
            src = jnp.where(low if half == 0 else jnp.logical_not(low), src, 0.0)
            pieces.append(jnp.concatenate(
                [src, zero_block] if blk == 0 else [zero_block, src], axis=1))
        qt = jnp.concatenate(pieces, axis=0).astype(BF16)

        k_old, v_old = kt_ref[b], vt_ref[b]
        k_new, v_new = kn_ref[b], vn_ref[b]
        k_pad = jnp.concatenate([k_new, pad_keys], axis=0).astype(BF16)
        v_pad = jnp.concatenate([v_new, pad_keys], axis=0).astype(BF16)
        s_old = jnp.dot(qt, k_old.astype(BF16), preferred_element_type=F32) + bias_old
        s_new = lax.dot_general(qt, k_pad, NT_DIMS, preferred_element_type=F32) + bias_new
        m = jnp.maximum(sink, jnp.maximum(jnp.max(s_old, axis=-1, keepdims=True),
                                          jnp.max(s_new, axis=-1, keepdims=True)))
        p_old = jnp.exp(s_old - m)
        p_new = jnp.exp(s_new - m)
        denom = (jnp.exp(sink - m) + jnp.sum(p_old, axis=-1, keepdims=True)
                 + jnp.sum(p_new, axis=-1, keepdims=True))
        o = lax.dot_general(p_old.astype(BF16), v_old.astype(BF16), NT_DIMS,
                            preferred_element_type=F32)
        o = o + jnp.dot(p_new.astype(BF16), v_pad, preferred_element_type=F32)
        o = o / denom
        cols = []
        for c in range(N_HEADS // 2):
            parts = []
            for h in (2 * c, 2 * c + 1):
                blk, half = _head_lane_block(h // GROUP)
                piece = o[h * l:(h + 1) * l, blk * LANES:(blk + 1) * LANES]
                if h % 2 != half:
                    piece = pltpu.roll(piece, HEAD_DIM, axis=1)
                parts.append(piece)
            cols.append(jnp.where(low, parts[0], parts[1]))
        o_ref[b] = jnp.concatenate(cols, axis=1).astype(BF16)

        for old, new, out_ref in ((k_old, k_new, kto_ref), (v_old, v_new, vto_ref)):
            shifted = pltpu.roll(old, WINDOW - l, axis=1)
            placed = jnp.concatenate([pad_rows, new], axis=0).T
            out_ref[b] = jnp.where(keep, shifted, placed)
        return carry

    lax.fori_loop(0, bt, body, 0, unroll=2)


def _attn_sample(qkv, cache_kt, cache_vt, layer, bias_old, bias_new, sink_col):
    b, l, _ = qkv.shape
    bt = min(SAMPLE_BATCH_TILE, b)
    kcol, vcol = Q_DIM // KV_DIM, Q_DIM // KV_DIM + 1
    const = lambda b: (0, 0)
    return pl.pallas_call(
        _attn_sample_kernel,
        grid=(b // bt,),
        in_specs=[pl.BlockSpec((bt, l, Q_DIM), lambda b: (b, 0, 0)),
                  pl.BlockSpec((bt, l, KV_DIM), lambda b: (b, 0, kcol)),
                  pl.BlockSpec((bt, l, KV_DIM), lambda b: (b, 0, vcol)),
                  pl.BlockSpec((None, bt, KV_DIM, WINDOW), lambda b: (layer, b, 0, 0)),
                  pl.BlockSpec((None, bt, KV_DIM, WINDOW), lambda b: (layer, b, 0, 0)),
                  pl.BlockSpec(bias_old.shape, const),
                  pl.BlockSpec(bias_new.shape, const),
                  pl.BlockSpec(sink_col.shape, const)],
        out_specs=[pl.BlockSpec((bt, l, Q_DIM), lambda b: (b, 0, 0)),
                   pl.BlockSpec((bt, KV_DIM, WINDOW), lambda b: (b, 0, 0)),
                   pl.BlockSpec((bt, KV_DIM, WINDOW), lambda b: (b, 0, 0))],
        out_shape=[jax.ShapeDtypeStruct((b, l, Q_DIM), BF16),
                   jax.ShapeDtypeStruct((b, KV_DIM, WINDOW), F32),
                   jax.ShapeDtypeStruct((b, KV_DIM, WINDOW), F32)],
        compiler_params=_params("parallel"), name="attn_sample",
    )(qkv, qkv, qkv, cache_kt, cache_vt, bias_old, bias_new, sink_col)


def _attn_bias_tables(l_sample):
    slopes = jnp.exp2(-8.0 * jnp.arange(1, N_HEADS + 1, dtype=F32) / N_HEADS)[:, None, None]
    t = jnp.arange(WINDOW, dtype=jnp.int32)[:, None]
    s = jnp.arange(2 * WINDOW, dtype=jnp.int32)[None, :]
    dist = t + WINDOW - s
    valid = (dist >= 0) & (dist <= WINDOW)
    later = jnp.where(valid[None], -slopes * dist.astype(F32)[None], MASKED)
    first = jnp.where((s >= WINDOW)[None], later, MASKED)
    prompt = jnp.stack([first, later])
    tq = jnp.arange(l_sample, dtype=jnp.int32)[:, None]
    sk = jnp.arange(WINDOW + NEW_KEY_PAD, dtype=jnp.int32)[None, :]
    dist = tq + WINDOW - sk
    valid = (dist >= 0) & (dist <= WINDOW) & (sk < WINDOW + l_sample)
    sample = jnp.where(valid[None], -slopes * dist.astype(F32)[None], MASKED)
    sample = sample.reshape(N_HEADS * l_sample, WINDOW + NEW_KEY_PAD)
    return prompt, sample[:, :WINDOW], sample[:, WINDOW:]


def _log_sigmoid(x):
    return jnp.minimum(x, 0.0) - jnp.log(1.0 + jnp.exp(-jnp.abs(x)))


def _cumsum_matmul(x, tril):
    out = None
    for _ in range(3):
        part = x.astype(BF16)
        x = x - part.astype(F32)
        term = jnp.dot(tril, part, preferred_element_type=F32)
        out = term if out is None else out + term
    return out


def _gla_gate_out(o, r, gn):
    return (_rms(o, gn) * _silu(r)).astype(BF16)


def _gla_prompt_kernel(q_ref, k_ref, v_ref, r_ref, gd_ref, wg_ref, bg_ref, gn_ref,
                       o_ref, s_ref, st_ref):
    c = GLA_CHUNK
    rows = GLA_GROUP * c
    n_groups = q_ref.shape[1] // rows
    row = lax.broadcasted_iota(jnp.int32, (rows, rows), 0)
    col = lax.broadcasted_iota(jnp.int32, (rows, rows), 1)
    causal = (row >= col) & (col >= row - jnp.bitwise_and(row, c - 1))
    tril = jnp.where(causal, 1.0, 0.0).astype(BF16)
    wg = wg_ref[...].astype(BF16)
    bg = bg_ref[...]
    gn = gn_ref[...]
    st_ref[...] = jnp.zeros_like(st_ref)

    def body(g, carry):
        sl = pl.ds(pl.multiple_of(g * rows, rows), rows)
        q = q_ref[0, sl, :] * (GLA_DK_HEAD ** -0.5)
        k = k_ref[0, sl, :]
        v_bf = v_ref[0, sl, :].astype(BF16)
        pre = jnp.dot(gd_ref[0, sl, :].astype(BF16), wg, preferred_element_type=F32) + bg
        b = _cumsum_matmul(_log_sigmoid(pre) / GLA_GATE_NORMALIZER, tril)
        per_chunk = lambda r0: jnp.concatenate(
            [jnp.broadcast_to(b[j * c + r0:j * c + r0 + 1, :], (c, GLA_DK_HEAD))
             for j in range(GLA_GROUP)], axis=0)
        b_mid = per_chunk(c // 2)
        b_end = per_chunk(c - 1)
        qa = (q * jnp.exp(b - b_mid)).astype(BF16)
        ka = (k * jnp.exp(b_mid - b)).astype(BF16)
        a = lax.dot_general(qa, ka, NT_DIMS, preferred_element_type=F32)
        a = jnp.where(causal, a, 0.0).astype(BF16)
        o_intra = jnp.dot(a, v_bf, preferred_element_type=F32)
        qb = (q * jnp.exp(b)).astype(BF16)
        k_end = (k * jnp.exp(b_end - b)).astype(BF16)
        st = st_ref[...]
        o_inter = []
        for j in range(GLA_GROUP):
            rs = slice(j * c, (j + 1) * c)
            o_inter.append(lax.dot_general(qb[rs], st.astype(BF16), NT_DIMS,
                                           preferred_element_type=F32))
            upd = lax.dot_general(v_bf[rs], k_end[rs], TN_DIMS, preferred_element_type=F32)
            st = st * jnp.exp(b[(j + 1) * c - 1:(j + 1) * c, :]) + upd
        st_ref[...] = st
        o = o_intra + jnp.concatenate(o_inter, axis=0)
        o_ref[0, sl, :] = _gla_gate_out(o, r_ref[0, sl, :], gn)
        return carry

    lax.fori_loop(0, n_groups, body, 0, unroll=4)
    s_ref[0, 0] = st_ref[...].T


def _gla_prompt(proj, gd, w_gate, b_gate, g_norm, layer):
    b, l, _ = proj.shape
    kcol = GLA_DK // GLA_DK_HEAD
    vcol = 2 * GLA_DK // GLA_DV_HEAD
    rcol = vcol + GLA_DV // GLA_DV_HEAD
    return pl.pallas_call(
        _gla_prompt_kernel,
        grid=(b, GLA_HEADS),
        in_specs=[pl.BlockSpec((1, l, GLA_DK_HEAD), lambda b, h: (b, 0, h)),
                  pl.BlockSpec((1, l, GLA_DK_HEAD), lambda b, h: (b, 0, kcol + h)),
                  pl.BlockSpec((1, l, GLA_DV_HEAD), lambda b, h: (b, 0, vcol + h)),
                  pl.BlockSpec((1, l, GLA_DV_HEAD), lambda b, h: (b, 0, rcol + h)),
                  pl.BlockSpec((1, l, GLA_GATE_RANK), lambda b, h: (b, 0, 0)),
                  pl.BlockSpec((None, GLA_GATE_RANK, GLA_DK_HEAD), lambda b, h: (layer, 0, h)),
                  pl.BlockSpec((None, 1, GLA_DK_HEAD), lambda b, h: (layer, 0, h)),
                  pl.BlockSpec((None, 1, GLA_DV_HEAD), lambda b, h: (layer, 0, 0))],
        out_specs=[pl.BlockSpec((1, l, GLA_DV_HEAD), lambda b, h: (b, 0, h)),
                   pl.BlockSpec((1, 1, GLA_DK_HEAD, GLA_DV_HEAD), lambda b, h: (b, h, 0, 0))],
        out_shape=[jax.ShapeDtypeStruct((b, l, GLA_DV), BF16),
                   jax.ShapeDtypeStruct((b, GLA_HEADS, GLA_DK_HEAD, GLA_DV_HEAD), F32)],
        scratch_shapes=[pltpu.VMEM((GLA_DV_HEAD, GLA_DK_HEAD), F32)],
        compiler_params=_params("parallel", "parallel"), name="gla_prompt",
    )(proj, proj, proj, proj, gd, w_gate, b_gate, g_norm)


def _cumsum_sublanes(x):
    n = x.shape[0]
    row = lax.broadcasted_iota(jnp.int32, x.shape, 0)
    shift = 1
    while shift < n:
        x = x + jnp.where(row >= shift, pltpu.roll(x, shift, axis=0), 0.0)
        shift *= 2
    return x


def _gla_sample_kernel(q_ref, k_ref, v_ref, r_ref, gd_ref, s0_ref, wg_ref, bg_ref, gn_ref,
                       o_ref, s_ref):
    bt, c, _ = q_ref.shape
    row = lax.broadcasted_iota(jnp.int32, (c, c), 0)
    col = lax.broadcasted_iota(jnp.int32, (c, c), 1)
    causal = row >= col
    wg = wg_ref[...].astype(BF16)
    bg = bg_ref[...]
    gn = gn_ref[...]

    def body(i, carry):
        q = q_ref[i] * (GLA_DK_HEAD ** -0.5)
        k = k_ref[i]
        v = v_ref[i]
        r = r_ref[i]
        pre = jnp.dot(gd_ref[i].astype(BF16), wg, preferred_element_type=F32) + bg
        b = _cumsum_sublanes(_log_sigmoid(pre) / GLA_GATE_NORMALIZER)
        b_mid = b[c // 2:c // 2 + 1, :]
        b_end = b[c - 1:c, :]
        qa = q * jnp.exp(b - b_mid)
        ka = k * jnp.exp(b_mid - b)
        qb = (q * jnp.exp(b)).astype(BF16)
        k_end_t = (k * jnp.exp(b_end - b)).T
        dec_col = jnp.broadcast_to(jnp.exp(b_end), (c, GLA_DK)).T[:, 0:1]
        for h in range(GLA_HEADS):
            ks = slice(h * GLA_DK_HEAD, (h + 1) * GLA_DK_HEAD)
            vs = slice(h * GLA_DV_HEAD, (h + 1) * GLA_DV_HEAD)
            s0 = s0_ref[i, h]
            a = lax.dot_general(qa[:, ks], ka[:, ks], NT_DIMS, preferred_element_type=F32)
            a = jnp.where(causal, a, 0.0)
            o = jnp.dot(a, v[:, vs], preferred_element_type=F32)
            o = o + jnp.dot(qb[:, ks], s0.astype(BF16), preferred_element_type=F32)
            upd = jnp.dot(k_end_t[ks, :], v[:, vs], preferred_element_type=F32)
            s_ref[i, h] = s0 * dec_col[ks, :] + upd
            o_ref[i, :, vs] = _gla_gate_out(o, r[:, vs], gn)
        return carry

    lax.fori_loop(0, bt, body, 0, unroll=2)


def _gla_sample(proj, gd, state, w_gate, b_gate, g_norm, layer):
    b, l, _ = proj.shape
    bt = min(SAMPLE_BATCH_TILE, b)
    state_block = (bt, GLA_HEADS, GLA_DK_HEAD, GLA_DV_HEAD)
    return pl.pallas_call(
        _gla_sample_kernel,
        grid=(b // bt,),
        in_specs=[pl.BlockSpec((bt, l, GLA_DK), lambda b: (b, 0, 0)),
                  pl.BlockSpec((bt, l, GLA_DK), lambda b: (b, 0, 1)),
                  pl.BlockSpec((bt, l, GLA_DV), lambda b: (b, 0, 1)),
                  pl.BlockSpec((bt, l, GLA_DV), lambda b: (b, 0, 2)),
                  pl.BlockSpec((bt, l, GLA_GATE_RANK), lambda b: (b, 0, 0)),
                  pl.BlockSpec((None,) + state_block, lambda b: (layer, b, 0, 0, 0)),
                  pl.BlockSpec((None, GLA_GATE_RANK, GLA_DK), lambda b: (layer, 0, 0)),
                  pl.BlockSpec((None, 1, GLA_DK), lambda b: (layer, 0, 0)),
                  pl.BlockSpec((None, 1, GLA_DV_HEAD), lambda b: (layer, 0, 0))],
        out_specs=[pl.BlockSpec((bt, l, GLA_DV), lambda b: (b, 0, 0)),
                   pl.BlockSpec(state_block, lambda b: (b, 0, 0, 0))],
        out_shape=[jax.ShapeDtypeStruct((b, l, GLA_DV), BF16),
                   jax.ShapeDtypeStruct(state.shape[1:], F32)],
        compiler_params=_params("parallel"), name="gla_sample",
    )(proj, proj, proj, proj, gd, state, w_gate, b_gate, g_norm)


def kernel(x_prompt, x_sample, cache_k, cache_v, state_gla, norm_mix, norm_ffn, norm_final,
           w_attn_in, w_attn_out, attn_sinks, w_gla_in, w_gla_gate_up, b_gla_gate, gla_out_norm,
           w_gla_out, w_ffn_in, w_ffn_out):
    bp, lp, _ = x_prompt.shape
    bs, ls, _ = x_sample.shape
    depth = norm_mix.shape[0]
    n_attn = cache_k.shape[0]
    yp = x_prompt.reshape(bp * lp, D_MODEL)
    ys = x_sample.reshape(bs * ls, D_MODEL)
    tp = min(TOKEN_TILE, bp * lp)
    ts = min(TOKEN_TILE, bs * ls)
    bias_p, bias_old, bias_new = _attn_bias_tables(ls)
    n_main = 2 * GLA_DK + 2 * GLA_DV

    g_mix = norm_mix.reshape(depth, 1, D_MODEL)
    g_ffn = norm_ffn.reshape(depth, 1, D_MODEL)
    g_fin = norm_final.reshape(1, D_MODEL)
    w_attn_in_bf = w_attn_in.astype(BF16)
    w_attn_out_bf = w_attn_out.astype(BF16)
    w_gla_main_bf = w_gla_in[:, :, :n_main].astype(BF16)
    w_gla_gd_bf = w_gla_in[:, :, n_main:].astype(BF16)
    w_gla_out_bf = w_gla_out.astype(BF16)
    w_ffn_in_bf = w_ffn_in.astype(BF16)
    w_ffn_out_bf = w_ffn_out.astype(BF16)
    b_gate = b_gla_gate.reshape(-1, 1, GLA_DK)
    g_gla = gla_out_norm.reshape(-1, 1, GLA_DV_HEAD)
    cache_kt = jnp.transpose(cache_k, (0, 1, 3, 4, 2)).reshape(n_attn, bs, KV_DIM, WINDOW)
    cache_vt = jnp.transpose(cache_v, (0, 1, 3, 4, 2)).reshape(n_attn, bs, KV_DIM, WINDOW)

    kp_l, vp_l, sp_l, ks_l, vs_l, ss_l = [], [], [], [], [], []
    for i in range(depth):
        j = i // 2
        if i % 2 == 0:
            w_mix = w_attn_out_bf
            (qkv_p,) = _norm_matmul(yp, g_mix, [w_attn_in_bf], (i, j), tp)
            (qkv_s,) = _norm_matmul(ys, g_mix, [w_attn_in_bf], (i, j), ts)
            qkv_p = qkv_p.reshape(bp, lp, Q_DIM + 2 * KV_DIM)
            qkv_s = qkv_s.reshape(bs, ls, Q_DIM + 2 * KV_DIM)
            op = _attn_prompt(qkv_p, attn_sinks[j], bias_p)
            sink_col = jnp.repeat(attn_sinks[j], ls).reshape(N_HEADS * ls, 1)
            os_, kn, vn = _attn_sample(qkv_s, cache_kt, cache_vt, j, bias_old, bias_new, sink_col)
            kv_shape = (bp, WINDOW, N_KV_HEADS, HEAD_DIM)
            kp_l.append(qkv_p[:, lp - WINDOW:, Q_DIM:Q_DIM + KV_DIM].reshape(kv_shape))
            vp_l.append(qkv_p[:, lp - WINDOW:, Q_DIM + KV_DIM:].reshape(kv_shape))
            ks_l.append(kn)
            vs_l.append(vn)
        else:
            w_mix = w_gla_out_bf
            proj_p, gd_p = _norm_matmul(yp, g_mix, [w_gla_main_bf, w_gla_gd_bf], (i, j), tp)
            proj_s, gd_s = _norm_matmul(ys, g_mix, [w_gla_main_bf, w_gla_gd_bf], (i, j), ts)
            op, sp = _gla_prompt(proj_p.reshape(bp, lp, n_main), gd_p.reshape(bp, lp, GLA_GATE_RANK),
                                 w_gla_gate_up, b_gate, g_gla, j)
            os_, sn = _gla_sample(proj_s.reshape(bs, ls, n_main), gd_s.reshape(bs, ls, GLA_GATE_RANK),
                                  state_gla, w_gla_gate_up, b_gate, g_gla, j)
            sp_l.append(sp)
            ss_l.append(sn)
        last = i == depth - 1
        yp = _mix_out_ffn(yp, op.reshape(bp * lp, D_MODEL), w_mix, g_ffn, w_ffn_in_bf, w_ffn_out_bf,
                          g_fin, (j, i), last, min(FFN_TOKEN_TILE, bp * lp))
        ys = _mix_out_ffn(ys, os_.reshape(bs * ls, D_MODEL), w_mix, g_ffn, w_ffn_in_bf, w_ffn_out_bf,
                          g_fin, (j, i), last, min(FFN_TOKEN_TILE, bs * ls))

    def window_major(xt):
        return jnp.transpose(xt.reshape(n_attn, bs, N_KV_HEADS, HEAD_DIM, WINDOW), (0, 1, 4, 2, 3))

    return (yp.reshape(bp, lp, D_MODEL), ys.reshape(bs, ls, D_MODEL),
            jnp.stack(kp_l), jnp.stack(vp_l), jnp.stack(sp_l),
            window_major(jnp.stack(ks_l)), window_major(jnp.stack(vs_l)), jnp.stack(ss_l))
```

```python
import functools

import jax
import jax.numpy as jnp
from jax import lax
from jax.experimental import pallas as pl
from jax.experimental.pallas import tpu as pltpu

F32 = jnp.float32
BF16 = jnp.bfloat16

D_MODEL = 1024
EPS = 1e-6
WINDOW = 128
HEAD_DIM = 64
N_HEADS = 16
N_KV_HEADS = 4
GROUP = 4
Q_DIM = 1024
KV_DIM = 256
GLA_HEADS = 4
GLA_DK = 512
GLA_DV = 1024
GLA_DK_HEAD = 128
GLA_DV_HEAD = 256
GLA_GATE_RANK = 16
GLA_GATE_NORMALIZER = 16.0
GLA_CHUNK = 64
D_FF = 2816
MASKED = -1e30

VMEM_LIMIT_BYTES = 52 * 1024 * 1024
LANES = 128
TOKEN_TILE = 1024
FFN_TOKEN_TILE = 512
FFN_TILE = 256
ATTN_Q_BLOCKS = 2
GLA_GROUP = 4
SAMPLE_BATCH_TILE = 8
NEW_KEY_PAD = 16

NT_DIMS = (((1,), (1,)), ((), ()))
TN_DIMS = (((0,), (0,)), ((), ()))


def _params(*sem):
    return pltpu.CompilerParams(dimension_semantics=sem, vmem_limit_bytes=VMEM_LIMIT_BYTES)


def _rms(x, g):
    r = lax.rsqrt(jnp.mean(x * x, axis=-1, keepdims=True) + EPS)
    return x * r * g


def _silu(x):
    return x / (1.0 + jnp.exp(-x))


def _norm_matmul_kernel(n, x_ref, g_ref, *refs):
    h = _rms(x_ref[...], g_ref[...]).astype(BF16)
    for w_ref, o_ref in zip(refs[:n], refs[n:]):
        o_ref[...] = jnp.dot(h, w_ref[...], preferred_element_type=F32)


def _norm_matmul(x, g, ws, layers, tm):
    t = x.shape[0]
    lg, lw = layers
    in_specs = [pl.BlockSpec((tm, D_MODEL), lambda i: (i, 0)),
                pl.BlockSpec((None, 1, D_MODEL), lambda i: (lg, 0, 0))]
    in_specs += [pl.BlockSpec((None,) + w.shape[1:], lambda i: (lw, 0, 0)) for w in ws]
    out_specs = [pl.BlockSpec((tm, w.shape[2]), lambda i: (i, 0)) for w in ws]
    out_shape = [jax.ShapeDtypeStruct((t, w.shape[2]), F32) for w in ws]
    return pl.pallas_call(
        functools.partial(_norm_matmul_kernel, len(ws)),
        grid=(t // tm,), in_specs=in_specs, out_specs=out_specs, out_shape=out_shape,
        compiler_params=_params("parallel"), name="norm_matmul",
    )(x, g, *ws)


def _ffn_kernel(final_norm, x_ref, a_ref, wm_ref, g_ref, wi_ref, wo_ref, gf_ref, o_ref, act_ref):
    y = x_ref[...] + jnp.dot(a_ref[...], wm_ref[...], preferred_element_type=F32)
    h = _rms(y, g_ref[...]).astype(BF16)
    for j in range(D_FF // FFN_TILE):
        gate = jnp.dot(h, wi_ref[:, j * FFN_TILE:(j + 1) * FFN_TILE], preferred_element_type=F32)
        up = jnp.dot(h, wi_ref[:, D_FF + j * FFN_TILE:D_FF + (j + 1) * FFN_TILE],
                     preferred_element_type=F32)
        act_ref[:, j * FFN_TILE:(j + 1) * FFN_TILE] = (_silu(gate) * up).astype(BF16)
    y = y + jnp.dot(act_ref[...], wo_ref[...], preferred_element_type=F32)
    if final_norm:
        y = _rms(y, gf_ref[...])
    o_ref[...] = y


def _resident(block_shape, index_map):
    return pl.BlockSpec(block_shape, index_map, pipeline_mode=pl.Buffered(1))


def _mix_out_ffn(x, a, w_mix, g, w_in, w_out, g_final, layers, final_norm, tm):
    t = x.shape[0]
    lm, lf = layers
    row = lambda i: (i, 0)
    return pl.pallas_call(
        functools.partial(_ffn_kernel, final_norm),
        grid=(t // tm,),
        in_specs=[pl.BlockSpec((tm, D_MODEL), row),
                  pl.BlockSpec((tm, D_MODEL), row),
                  _resident((None, D_MODEL, D_MODEL), lambda i: (lm, 0, 0)),
                  _resident((None, 1, D_MODEL), lambda i: (lf, 0, 0)),
                  _resident((None, D_MODEL, 2 * D_FF), lambda i: (lf, 0, 0)),
                  _resident((None, D_FF, D_MODEL), lambda i: (lf, 0, 0)),
                  _resident((1, D_MODEL), lambda i: (0, 0))],
        out_specs=pl.BlockSpec((tm, D_MODEL), row),
        out_shape=jax.ShapeDtypeStruct((t, D_MODEL), F32),
        scratch_shapes=[pltpu.VMEM((tm, D_FF), BF16)],
        compiler_params=_params("parallel"), name="mix_out_ffn",
    )(x, a, w_mix, g, w_in, w_out, g_final)


HALVES = LANES // HEAD_DIM
HEAD_ORDER = tuple((HALVES * (p // (HALVES * GROUP)) + p % HALVES) * GROUP + (p % (HALVES * GROUP)) // HALVES
                   for p in range(N_HEADS))


def _half_masks(rows):
    low = lax.broadcasted_iota(jnp.int32, (rows, LANES), 1) < HEAD_DIM
    return low, jnp.logical_not(low)


def _attn_prompt_kernel(sink_ref, q_ref, kp_ref, kc_ref, vp_ref, vc_ref, bias_ref, o_ref):
    step = pl.program_id(1)
    q_all = (q_ref[0] * (HEAD_DIM ** -0.5)).astype(BF16)
    k_all = jnp.concatenate([kp_ref[0], kc_ref[0]], axis=0).astype(BF16)
    v_all = jnp.concatenate([vp_ref[0], vc_ref[0]], axis=0).astype(BF16)
    masks = _half_masks(WINDOW)
    ones = jnp.ones((2 * WINDOW, LANES), BF16)
    zero = jnp.zeros((), BF16)
    for qb in range(ATTN_Q_BLOCKS):
        rows = slice(qb * WINDOW, (qb + 1) * WINDOW)
        keys = slice(qb * WINDOW, (qb + 2) * WINDOW)
        table = 1 if qb > 0 else jnp.minimum(step, 1)
        for blk in range(KV_DIM // LANES):
            lanes = slice(blk * LANES, (blk + 1) * LANES)
            k2 = k_all[keys, lanes]
            v2 = jnp.concatenate([v_all[keys, lanes], ones], axis=1)
            for c in range(blk * GROUP, (blk + 1) * GROUP):
                qc = q_all[rows, c * LANES:(c + 1) * LANES]
                halves = []
                for half in range(HALVES):
                    p = HALVES * c + half
                    qm = jnp.where(masks[half], qc, zero)
                    s = lax.dot_general(qm, k2, NT_DIMS, preferred_element_type=F32)
                    s = s + bias_ref[table, p]
                    sink = sink_ref[p]
                    m = jnp.maximum(jnp.max(s, axis=-1, keepdims=True), sink)
                    probs = jnp.exp(s - m).astype(BF16)
                    ov = jnp.dot(probs, v2, preferred_element_type=F32)
                    denom = ov[:, LANES:] + jnp.exp(sink - m)
                    halves.append(ov[:, :LANES] / denom)
                o_ref[0, rows, c * LANES:(c + 1) * LANES] = jnp.where(
                    masks[0], halves[0], halves[1]).astype(BF16)


def _attn_prompt(qkv, sinks, bias):
    b, l, _ = qkv.shape
    rows = ATTN_Q_BLOCKS * WINDOW
    kcol, vcol = Q_DIM // KV_DIM, Q_DIM // KV_DIM + 1
    prev = lambda i: jnp.maximum(ATTN_Q_BLOCKS * i - 1, 0)
    return pl.pallas_call(
        _attn_prompt_kernel,
        grid=(b, l // rows),
        in_specs=[pl.BlockSpec(memory_space=pltpu.SMEM),
                  pl.BlockSpec((1, rows, Q_DIM), lambda b, i: (b, i, 0)),
                  pl.BlockSpec((1, WINDOW, KV_DIM), lambda b, i: (b, prev(i), kcol)),
                  pl.BlockSpec((1, rows, KV_DIM), lambda b, i: (b, i, kcol)),
                  pl.BlockSpec((1, WINDOW, KV_DIM), lambda b, i: (b, prev(i), vcol)),
                  pl.BlockSpec((1, rows, KV_DIM), lambda b, i: (b, i, vcol)),
                  pl.BlockSpec(bias.shape, lambda b, i: (0, 0, 0, 0))],
        out_specs=pl.BlockSpec((1, rows, Q_DIM), lambda b, i: (b, i, 0)),
        out_shape=jax.ShapeDtypeStruct((b, l, Q_DIM), BF16),
        compiler_params=_params("parallel", "parallel"), name="attn_prompt",
    )(sinks, qkv, qkv, qkv, qkv, qkv, bias)


def _attn_sample_kernel(q_ref, kn_ref, vn_ref, kt_ref, vt_ref, bo_ref, bn_ref, sink_ref,
                        o_ref, kto_ref, vto_ref):
    bt, l, _ = q_ref.shape
    masks = _half_masks(l)
    keep = lax.broadcasted_iota(jnp.int32, (KV_DIM, WINDOW), 1) < WINDOW - l
    zero_block = jnp.zeros((l, LANES), F32)
    pad_rows = jnp.zeros((WINDOW - l, KV_DIM), F32)
    pad_keys = jnp.zeros((NEW_KEY_PAD - l, KV_DIM), F32)
    bias_old = bo_ref[...]
    bias_new = bn_ref[...]
    sink = sink_ref[...]

    def body(b, carry):
        q = q_ref[b] * (HEAD_DIM ** -0.5)
        pieces = []
        for p in range(N_HEADS):
            c = p // HALVES
            src = jnp.where(masks[p % HALVES], q[:, c * LANES:(c + 1) * LANES], 0.0)
            pieces.append(jnp.concatenate(
                [src, zero_block] if c // GROUP == 0 else [zero_block, src], axis=1))
        qt = jnp.concatenate(pieces, axis=0).astype(BF16)

        k_old, v_old = kt_ref[b], vt_ref[b]
        k_new, v_new = kn_ref[b], vn_ref[b]
        k_pad = jnp.concatenate([k_new, pad_keys], axis=0).astype(BF16)
        v_pad = jnp.concatenate([v_new, pad_keys], axis=0).astype(BF16)
        s_old = jnp.dot(qt, k_old.astype(BF16), preferred_element_type=F32) + bias_old
        s_new = lax.dot_general(qt, k_pad, NT_DIMS, preferred_element_type=F32) + bias_new
        m = jnp.maximum(sink, jnp.maximum(jnp.max(s_old, axis=-1, keepdims=True),
                                          jnp.max(s_new, axis=-1, keepdims=True)))
        p_old = jnp.exp(s_old - m)
        p_new = jnp.exp(s_new - m)
        denom = (jnp.exp(sink - m) + jnp.sum(p_old, axis=-1, keepdims=True)
                 + jnp.sum(p_new, axis=-1, keepdims=True))
        o = lax.dot_general(p_old.astype(BF16), v_old.astype(BF16), NT_DIMS,
                            preferred_element_type=F32)
        o = o + jnp.dot(p_new.astype(BF16), v_pad, preferred_element_type=F32)
        o = o / denom
        cols = []
        for c in range(N_HEADS // HALVES):
            lanes = slice((c // GROUP) * LANES, (c // GROUP + 1) * LANES)
            parts = [o[(HALVES * c + half) * l:(HALVES * c + half + 1) * l, lanes]
                     for half in range(HALVES)]
            cols.append(jnp.where(masks[0], parts[0], parts[1]))
        o_ref[b] = jnp.concatenate(cols, axis=1).astype(BF16)

        for old, new, out_ref in ((k_old, k_new, kto_ref), (v_old, v_new, vto_ref)):
            shifted = pltpu.roll(old, WINDOW - l, axis=1)
            placed = jnp.concatenate([pad_rows, new], axis=0).T
            out_ref[b] = jnp.where(keep, shifted, placed)
        return carry

    lax.fori_loop(0, bt, body, 0, unroll=2)


def _attn_sample(qkv, cache_kt, cache_vt, layer, bias_old, bias_new, sink_col):
    b, l, _ = qkv.shape
    bt = min(SAMPLE_BATCH_TILE, b)
    kcol, vcol = Q_DIM // KV_DIM, Q_DIM // KV_DIM + 1
    const = lambda b: (0, 0)
    return pl.pallas_call(
        _attn_sample_kernel,
        grid=(b // bt,),
        in_specs=[pl.BlockSpec((bt, l, Q_DIM), lambda b: (b, 0, 0)),
                  pl.BlockSpec((bt, l, KV_DIM), lambda b: (b, 0, kcol)),
                  pl.BlockSpec((bt, l, KV_DIM), lambda b: (b, 0, vcol)),
                  pl.BlockSpec((None, bt, KV_DIM, WINDOW), lambda b: (layer, b, 0, 0)),
                  pl.BlockSpec((None, bt, KV_DIM, WINDOW), lambda b: (layer, b, 0, 0)),
                  pl.BlockSpec(bias_old.shape, const),
                  pl.BlockSpec(bias_new.shape, const),
                  pl.BlockSpec(sink_col.shape, const)],
        out_specs=[pl.BlockSpec((bt, l, Q_DIM), lambda b: (b, 0, 0)),
                   pl.BlockSpec((bt, KV_DIM, WINDOW), lambda b: (b, 0, 0)),
                   pl.BlockSpec((bt, KV_DIM, WINDOW), lambda b: (b, 0, 0))],
        out_shape=[jax.ShapeDtypeStruct((b, l, Q_DIM), BF16),
                   jax.ShapeDtypeStruct((b, KV_DIM, WINDOW), F32),
                   jax.ShapeDtypeStruct((b, KV_DIM, WINDOW), F32)],
        compiler_params=_params("parallel"), name="attn_sample",
    )(qkv, qkv, qkv, cache_kt, cache_vt, bias_old, bias_new, sink_col)


def _attn_bias_tables(l_sample):
    slopes = jnp.exp2(-8.0 * jnp.arange(1, N_HEADS + 1, dtype=F32) / N_HEADS)
    slopes = slopes[jnp.array(HEAD_ORDER)][:, None, None]
    t = jnp.arange(WINDOW, dtype=jnp.int32)[:, None]
    s = jnp.arange(2 * WINDOW, dtype=jnp.int32)[None, :]
    dist = t + WINDOW - s
    valid = (dist >= 0) & (dist <= WINDOW)
    later = jnp.where(valid[None], -slopes * dist.astype(F32)[None], MASKED)
    first = jnp.where((s >= WINDOW)[None], later, MASKED)
    prompt = jnp.stack([first, later])
    tq = jnp.arange(l_sample, dtype=jnp.int32)[:, None]
    sk = jnp.arange(WINDOW + NEW_KEY_PAD, dtype=jnp.int32)[None, :]
    dist = tq + WINDOW - sk
    valid = (dist >= 0) & (dist <= WINDOW) & (sk < WINDOW + l_sample)
    sample = jnp.where(valid[None], -slopes * dist.astype(F32)[None], MASKED)
    sample = sample.reshape(N_HEADS * l_sample, WINDOW + NEW_KEY_PAD)
    return prompt, sample[:, :WINDOW], sample[:, WINDOW:]


def _log_sigmoid(x):
    return jnp.minimum(x, 0.0) - jnp.log(1.0 + jnp.exp(-jnp.abs(x)))


def _chunk_cumsum(x, c):
    pos = jnp.bitwise_and(lax.broadcasted_iota(jnp.int32, x.shape, 0), c - 1)
    shift = 1
    while shift < c:
        x = x + jnp.where(pos >= shift, pltpu.roll(x, shift, axis=0), 0.0)
        shift *= 2
    return x


def _gla_proj_kernel(chunk, x_ref, g_ref, wm_ref, wgd_ref, wg_ref, bg_ref, o_ref, b_ref):
    h = _rms(x_ref[...], g_ref[...]).astype(BF16)
    gd = jnp.dot(h, wgd_ref[...], preferred_element_type=F32)
    pre = jnp.dot(gd.astype(BF16), wg_ref[...].astype(BF16), preferred_element_type=F32) + bg_ref[...]
    b_ref[...] = _chunk_cumsum(_log_sigmoid(pre) / GLA_GATE_NORMALIZER, chunk)
    o_ref[...] = jnp.dot(h, wm_ref[...], preferred_element_type=F32)


def _gla_proj(x, g, w_main, w_gd, w_gate, b_gate, layers, chunk, tm):
    t = x.shape[0]
    lg, lw = layers
    n_main = w_main.shape[2]
    return pl.pallas_call(
        functools.partial(_gla_proj_kernel, chunk),
        grid=(t // tm,),
        in_specs=[pl.BlockSpec((tm, D_MODEL), lambda i: (i, 0)),
                  pl.BlockSpec((None, 1, D_MODEL), lambda i: (lg, 0, 0)),
                  pl.BlockSpec((None, D_MODEL, n_main), lambda i: (lw, 0, 0)),
                  pl.BlockSpec((None, D_MODEL, GLA_GATE_RANK), lambda i: (lw, 0, 0)),
                  pl.BlockSpec((None, GLA_GATE_RANK, GLA_DK), lambda i: (lw, 0, 0)),
                  pl.BlockSpec((None, 1, GLA_DK), lambda i: (lw, 0, 0))],
        out_specs=[pl.BlockSpec((tm, n_main), lambda i: (i, 0)),
                   pl.BlockSpec((tm, GLA_DK), lambda i: (i, 0))],
        out_shape=[jax.ShapeDtypeStruct((t, n_main), F32),
                   jax.ShapeDtypeStruct((t, GLA_DK), F32)],
        compiler_params=_params("parallel"), name="gla_proj",
    )(x, g, w_main, w_gd, w_gate, b_gate)


def _gla_gate_out(o, r, gn):
    return (_rms(o, gn) * _silu(r)).astype(BF16)


def _gla_prompt_kernel(q_ref, k_ref, v_ref, r_ref, b_ref, gn_ref, o_ref, s_ref, st_ref):
    c = GLA_CHUNK
    rows = GLA_GROUP * c
    n_groups = q_ref.shape[1] // rows
    row = lax.broadcasted_iota(jnp.int32, (rows, rows), 0)
    col = lax.broadcasted_iota(jnp.int32, (rows, rows), 1)
    causal = (row >= col) & (col >= row - jnp.bitwise_and(row, c - 1))
    gn = gn_ref[...]
    st_ref[...] = jnp.zeros_like(st_ref)

    def body(g, carry):
        sl = pl.ds(pl.multiple_of(g * rows, rows), rows)
        q = q_ref[0, sl, :] * (GLA_DK_HEAD ** -0.5)
        k = k_ref[0, sl, :]
        v_bf = v_ref[0, sl, :].astype(BF16)
        b = b_ref[0, sl, :]
        per_chunk = lambda r0: jnp.concatenate(
            [jnp.broadcast_to(b[j * c + r0:j * c + r0 + 1, :], (c, GLA_DK_HEAD))
             for j in range(GLA_GROUP)], axis=0)
        b_mid = per_chunk(c // 2)
        b_end = per_chunk(c - 1)
        qa = (q * jnp.exp(b - b_mid)).astype(BF16)
        ka = (k * jnp.exp(b_mid - b)).astype(BF16)
        a = lax.dot_general(qa, ka, NT_DIMS, preferred_element_type=F32)
        a = jnp.where(causal, a, 0.0).astype(BF16)
        o_intra = jnp.dot(a, v_bf, preferred_element_type=F32)
        qb = (q * jnp.exp(b)).astype(BF16)
        k_end = (k * jnp.exp(b_end - b)).astype(BF16)
        st = st_ref[...]
        o_inter = []
        for j in range(GLA_GROUP):
            rs = slice(j * c, (j + 1) * c)
            o_inter.append(lax.dot_general(qb[rs], st.astype(BF16), NT_DIMS,
                                           preferred_element_type=F32))
            upd = lax.dot_general(v_bf[rs], k_end[rs], TN_DIMS, preferred_element_type=F32)
            st = st * jnp.exp(b[(j + 1) * c - 1:(j + 1) * c, :]) + upd
        st_ref[...] = st
        o = o_intra + jnp.concatenate(o_inter, axis=0)
        o_ref[0, sl, :] = _gla_gate_out(o, r_ref[0, sl, :], gn)
        return carry

    lax.fori_loop(0, n_groups, body, 0, unroll=4)
    s_ref[0, 0] = st_ref[...].T


def _gla_prompt(proj, decay, g_norm, layer):
    b, l, _ = proj.shape
    kcol = GLA_DK // GLA_DK_HEAD
    vcol = 2 * GLA_DK // GLA_DV_HEAD
    rcol = vcol + GLA_DV // GLA_DV_HEAD
    return pl.pallas_call(
        _gla_prompt_kernel,
        grid=(b, GLA_HEADS),
        in_specs=[pl.BlockSpec((1, l, GLA_DK_HEAD), lambda b, h: (b, 0, h)),
                  pl.BlockSpec((1, l, GLA_DK_HEAD), lambda b, h: (b, 0, kcol + h)),
                  pl.BlockSpec((1, l, GLA_DV_HEAD), lambda b, h: (b, 0, vcol + h)),
                  pl.BlockSpec((1, l, GLA_DV_HEAD), lambda b, h: (b, 0, rcol + h)),
                  pl.BlockSpec((1, l, GLA_DK_HEAD), lambda b, h: (b, 0, h)),
                  pl.BlockSpec((None, 1, GLA_DV_HEAD), lambda b, h: (layer, 0, 0))],
        out_specs=[pl.BlockSpec((1, l, GLA_DV_HEAD), lambda b, h: (b, 0, h)),
                   pl.BlockSpec((1, 1, GLA_DK_HEAD, GLA_DV_HEAD), lambda b, h: (b, h, 0, 0))],
        out_shape=[jax.ShapeDtypeStruct((b, l, GLA_DV), BF16),
                   jax.ShapeDtypeStruct((b, GLA_HEADS, GLA_DK_HEAD, GLA_DV_HEAD), F32)],
        scratch_shapes=[pltpu.VMEM((GLA_DV_HEAD, GLA_DK_HEAD), F32)],
        compiler_params=_params("parallel", "parallel"), name="gla_prompt",
    )(proj, proj, proj, proj, decay, g_norm)


def _gla_sample_kernel(q_ref, k_ref, v_ref, r_ref, b_ref, s0_ref, gn_ref, o_ref, s_ref):
    bt, c, _ = q_ref.shape
    row = lax.broadcasted_iota(jnp.int32, (c, c), 0)
    col = lax.broadcasted_iota(jnp.int32, (c, c), 1)
    causal = row >= col
    gn = gn_ref[...]

    def body(i, carry):
        q = q_ref[i] * (GLA_DK_HEAD ** -0.5)
        k = k_ref[i]
        v = v_ref[i]
        r = r_ref[i]
        b = b_ref[i]
        b_mid = b[c // 2:c // 2 + 1, :]
        b_end = b[c - 1:c, :]
        qa = q * jnp.exp(b - b_mid)
        ka = k * jnp.exp(b_mid - b)
        qb = (q * jnp.exp(b)).astype(BF16)
        k_end_t = (k * jnp.exp(b_end - b)).T
        dec_col = jnp.broadcast_to(jnp.exp(b_end), (c, GLA_DK)).T[:, 0:1]
        for h in range(GLA_HEADS):
            ks = slice(h * GLA_DK_HEAD, (h + 1) * GLA_DK_HEAD)
            vs = slice(h * GLA_DV_HEAD, (h + 1) * GLA_DV_HEAD)
            s0 = s0_ref[i, h]
            a = lax.dot_general(qa[:, ks], ka[:, ks], NT_DIMS, preferred_element_type=F32)
            a = jnp.where(causal, a, 0.0)
            o = jnp.dot(a, v[:, vs], preferred_element_type=F32)
            o = o + jnp.dot(qb[:, ks], s0.astype(BF16), preferred_element_type=F32)
            upd = jnp.dot(k_end_t[ks, :], v[:, vs], preferred_element_type=F32)
            s_ref[i, h] = s0 * dec_col[ks, :] + upd
            o_ref[i, :, vs] = _gla_gate_out(o, r[:, vs], gn)
        return carry

    lax.fori_loop(0, bt, body, 0, unroll=2)


def _gla_sample(proj, decay, state, g_norm, layer):
    b, l, _ = proj.shape
    bt = min(SAMPLE_BATCH_TILE, b)
    state_block = (bt, GLA_HEADS, GLA_DK_HEAD, GLA_DV_HEAD)
    return pl.pallas_call(
        _gla_sample_kernel,
        grid=(b // bt,),
        in_specs=[pl.BlockSpec((bt, l, GLA_DK), lambda b: (b, 0, 0)),
                  pl.BlockSpec((bt, l, GLA_DK), lambda b: (b, 0, 1)),
                  pl.BlockSpec((bt, l, GLA_DV), lambda b: (b, 0, 1)),
                  pl.BlockSpec((bt, l, GLA_DV), lambda b: (b, 0, 2)),
                  pl.BlockSpec((bt, l, GLA_DK), lambda b: (b, 0, 0)),
                  pl.BlockSpec((None,) + state_block, lambda b: (layer, b, 0, 0, 0)),
                  pl.BlockSpec((None, 1, GLA_DV_HEAD), lambda b: (layer, 0, 0))],
        out_specs=[pl.BlockSpec((bt, l, GLA_DV), lambda b: (b, 0, 0)),
                   pl.BlockSpec(state_block, lambda b: (b, 0, 0, 0))],
        out_shape=[jax.ShapeDtypeStruct((b, l, GLA_DV), BF16),
                   jax.ShapeDtypeStruct(state.shape[1:], F32)],
        compiler_params=_params("parallel"), name="gla_sample",
    )(proj, proj, proj, proj, decay, state, g_norm)


def kernel(x_prompt, x_sample, cache_k, cache_v, state_gla, norm_mix, norm_ffn, norm_final,
           w_attn_in, w_attn_out, attn_sinks, w_gla_in, w_gla_gate_up, b_gla_gate, gla_out_norm,
           w_gla_out, w_ffn_in, w_ffn_out):
    bp, lp, _ = x_prompt.shape
    bs, ls, _ = x_sample.shape
    depth = norm_mix.shape[0]
    n_attn = cache_k.shape[0]
    yp = x_prompt.reshape(bp * lp, D_MODEL)
    ys = x_sample.reshape(bs * ls, D_MODEL)
    tp = min(TOKEN_TILE, bp * lp)
    ts = min(TOKEN_TILE, bs * ls)
    bias_p, bias_old, bias_new = _attn_bias_tables(ls)
    n_main = 2 * GLA_DK + 2 * GLA_DV

    g_mix = norm_mix.reshape(depth, 1, D_MODEL)
    g_ffn = norm_ffn.reshape(depth, 1, D_MODEL)
    g_fin = norm_final.reshape(1, D_MODEL)
    order = jnp.array(HEAD_ORDER)
    w_q = w_attn_in[:, :, :Q_DIM].reshape(n_attn, D_MODEL, N_HEADS, HEAD_DIM)[:, :, order, :]
    w_attn_in_bf = jnp.concatenate(
        [w_q.reshape(n_attn, D_MODEL, Q_DIM), w_attn_in[:, :, Q_DIM:]], axis=2).astype(BF16)
    w_attn_out_bf = w_attn_out.reshape(n_attn, N_HEADS, HEAD_DIM, D_MODEL)[:, order].reshape(
        n_attn, Q_DIM, D_MODEL).astype(BF16)
    sinks = attn_sinks[:, order]
    w_gla_main_bf = w_gla_in[:, :, :n_main].astype(BF16)
    w_gla_gd_bf = w_gla_in[:, :, n_main:].astype(BF16)
    w_gla_out_bf = w_gla_out.astype(BF16)
    w_ffn_in_bf = w_ffn_in.astype(BF16)
    w_ffn_out_bf = w_ffn_out.astype(BF16)
    b_gate = b_gla_gate.reshape(-1, 1, GLA_DK)
    g_gla = gla_out_norm.reshape(-1, 1, GLA_DV_HEAD)
    cache_kt = jnp.transpose(cache_k, (0, 1, 3, 4, 2)).reshape(n_attn, bs, KV_DIM, WINDOW)
    cache_vt = jnp.transpose(cache_v, (0, 1, 3, 4, 2)).reshape(n_attn, bs, KV_DIM, WINDOW)

    kp_l, vp_l, sp_l, ks_l, vs_l, ss_l = [], [], [], [], [], []
    for i in range(depth):
        j = i // 2
        if i % 2 == 0:
            w_mix = w_attn_out_bf
            (qkv_p,) = _norm_matmul(yp, g_mix, [w_attn_in_bf], (i, j), tp)
            (qkv_s,) = _norm_matmul(ys, g_mix, [w_attn_in_bf], (i, j), ts)
            qkv_p = qkv_p.reshape(bp, lp, Q_DIM + 2 * KV_DIM)
            qkv_s = qkv_s.reshape(bs, ls, Q_DIM + 2 * KV_DIM)
            op = _attn_prompt(qkv_p, sinks[j], bias_p)
            sink_col = jnp.repeat(sinks[j], ls).reshape(N_HEADS * ls, 1)
            os_, kn, vn = _attn_sample(qkv_s, cache_kt, cache_vt, j, bias_old, bias_new, sink_col)
            kv_shape = (bp, WINDOW, N_KV_HEADS, HEAD_DIM)
            kp_l.append(qkv_p[:, lp - WINDOW:, Q_DIM:Q_DIM + KV_DIM].reshape(kv_shape))
            vp_l.append(qkv_p[:, lp - WINDOW:, Q_DIM + KV_DIM:].reshape(kv_shape))
            ks_l.append(kn)
            vs_l.append(vn)
        else:
            w_mix = w_gla_out_bf
            proj_p, dec_p = _gla_proj(yp, g_mix, w_gla_main_bf, w_gla_gd_bf, w_gla_gate_up, b_gate,
                                      (i, j), GLA_CHUNK, tp)
            proj_s, dec_s = _gla_proj(ys, g_mix, w_gla_main_bf, w_gla_gd_bf, w_gla_gate_up, b_gate,
                                      (i, j), ls, ts)
            op, sp = _gla_prompt(proj_p.reshape(bp, lp, n_main), dec_p.reshape(bp, lp, GLA_DK),
                                 g_gla, j)
            os_, sn = _gla_sample(proj_s.reshape(bs, ls, n_main), dec_s.reshape(bs, ls, GLA_DK),
                                  state_gla, g_gla, j)
            sp_l.append(sp)
            ss_l.append(sn)
        last = i == depth - 1
        yp = _mix_out_ffn(yp, op.reshape(bp * lp, D_MODEL), w_mix, g_ffn, w_ffn_in_bf, w_ffn_out_bf,
                          g_fin, (j, i), last, min(FFN_TOKEN_TILE, bp * lp))
        ys = _mix_out_ffn(ys, os_.reshape(bs * ls, D_MODEL), w_mix, g_ffn, w_ffn_in_bf, w_ffn_out_bf,
                          g_fin, (j, i), last, min(FFN_TOKEN_TILE, bs * ls))

    def window_major(xt):
        return jnp.transpose(xt.reshape(n_attn, bs, N_KV_HEADS, HEAD_DIM, WINDOW), (0, 1, 4, 2, 3))

    return (yp.reshape(bp, lp, D_MODEL), ys.reshape(bs, ls, D_MODEL),
            jnp.stack(kp_l), jnp.stack(vp_l), jnp.stack(sp_l),
            window_major(jnp.stack(ks_l)), window_major(jnp.stack(vs_l)), jnp.stack(ss_l))
```

```python
import functools

import jax
import jax.numpy as jnp
from jax import lax
from jax.experimental import pallas as pl
from jax.experimental.pallas import tpu as pltpu

F32 = jnp.float32
BF16 = jnp.bfloat16

D_MODEL = 1024
EPS = 1e-6
WINDOW = 128
HEAD_DIM = 64
N_HEADS = 16
N_KV_HEADS = 4
GROUP = 4
Q_DIM = 1024
KV_DIM = 256
GLA_HEADS = 4
GLA_DK = 512
GLA_DV = 1024
GLA_DK_HEAD = 128
GLA_DV_HEAD = 256
GLA_GATE_RANK = 16
GLA_GATE_NORMALIZER = 16.0
GLA_CHUNK = 64
D_FF = 2816
MASKED = -1e30
LOG2E = 1.4426950408889634

VMEM_LIMIT_BYTES = 52 * 1024 * 1024
LANES = 128
TOKEN_TILE = 1024
FFN_TOKEN_TILE = 512
FFN_TILE = 256
ATTN_Q_BLOCKS = 2
GLA_GROUP = 4
SAMPLE_BATCH_TILE = 8
NEW_KEY_PAD = 16

NT_DIMS = (((1,), (1,)), ((), ()))
TN_DIMS = (((0,), (0,)), ((), ()))


def _params(*sem):
    return pltpu.CompilerParams(dimension_semantics=sem, vmem_limit_bytes=VMEM_LIMIT_BYTES)


def _rms(x, g):
    r = lax.rsqrt(jnp.mean(x * x, axis=-1, keepdims=True) + EPS)
    return x * r * g


def _silu(x):
    return x / (1.0 + jnp.exp(-x))


def _norm_matmul_kernel(n, x_ref, g_ref, *refs):
    h = _rms(x_ref[...], g_ref[...]).astype(BF16)
    for w_ref, o_ref in zip(refs[:n], refs[n:]):
        o_ref[...] = jnp.dot(h, w_ref[...], preferred_element_type=F32)


def _norm_matmul(x, g, ws, layers, tm):
    t = x.shape[0]
    lg, lw = layers
    in_specs = [pl.BlockSpec((tm, D_MODEL), lambda i: (i, 0)),
                pl.BlockSpec((None, 1, D_MODEL), lambda i: (lg, 0, 0))]
    in_specs += [pl.BlockSpec((None,) + w.shape[1:], lambda i: (lw, 0, 0)) for w in ws]
    out_specs = [pl.BlockSpec((tm, w.shape[2]), lambda i: (i, 0)) for w in ws]
    out_shape = [jax.ShapeDtypeStruct((t, w.shape[2]), F32) for w in ws]
    return pl.pallas_call(
        functools.partial(_norm_matmul_kernel, len(ws)),
        grid=(t // tm,), in_specs=in_specs, out_specs=out_specs, out_shape=out_shape,
        compiler_params=_params("parallel"), name="norm_matmul",
    )(x, g, *ws)


def _ffn_kernel(final_norm, x_ref, a_ref, wm_ref, g_ref, wi_ref, wo_ref, gf_ref, o_ref, act_ref):
    y = x_ref[...] + jnp.dot(a_ref[...], wm_ref[...], preferred_element_type=F32)
    h = _rms(y, g_ref[...]).astype(BF16)
    for j in range(D_FF // FFN_TILE):
        gate = jnp.dot(h, wi_ref[:, j * FFN_TILE:(j + 1) * FFN_TILE], preferred_element_type=F32)
        up = jnp.dot(h, wi_ref[:, D_FF + j * FFN_TILE:D_FF + (j + 1) * FFN_TILE],
                     preferred_element_type=F32)
        act_ref[:, j * FFN_TILE:(j + 1) * FFN_TILE] = (_silu(gate) * up).astype(BF16)
    y = y + jnp.dot(act_ref[...], wo_ref[...], preferred_element_type=F32)
    if final_norm:
        y = _rms(y, gf_ref[...])
    o_ref[...] = y


def _resident(block_shape, index_map):
    return pl.BlockSpec(block_shape, index_map, pipeline_mode=pl.Buffered(1))


def _mix_out_ffn(x, a, w_mix, g, w_in, w_out, g_final, layers, final_norm, tm):
    t = x.shape[0]
    lm, lf = layers
    row = lambda i: (i, 0)
    return pl.pallas_call(
        functools.partial(_ffn_kernel, final_norm),
        grid=(t // tm,),
        in_specs=[pl.BlockSpec((tm, D_MODEL), row),
                  pl.BlockSpec((tm, D_MODEL), row),
                  _resident((None, D_MODEL, D_MODEL), lambda i: (lm, 0, 0)),
                  _resident((None, 1, D_MODEL), lambda i: (lf, 0, 0)),
                  _resident((None, D_MODEL, 2 * D_FF), lambda i: (lf, 0, 0)),
                  _resident((None, D_FF, D_MODEL), lambda i: (lf, 0, 0)),
                  _resident((1, D_MODEL), lambda i: (0, 0))],
        out_specs=pl.BlockSpec((tm, D_MODEL), row),
        out_shape=jax.ShapeDtypeStruct((t, D_MODEL), F32),
        scratch_shapes=[pltpu.VMEM((tm, D_FF), BF16)],
        compiler_params=_params("parallel"), name="mix_out_ffn",
    )(x, a, w_mix, g, w_in, w_out, g_final)


HALVES = LANES // HEAD_DIM
HEAD_ORDER = tuple((HALVES * (p // (HALVES * GROUP)) + p % HALVES) * GROUP + (p % (HALVES * GROUP)) // HALVES
                   for p in range(N_HEADS))


def _half_masks(rows):
    low = lax.broadcasted_iota(jnp.int32, (rows, LANES), 1) < HEAD_DIM
    return low, jnp.logical_not(low)


def _attn_prompt_kernel(sink_ref, q_ref, kp_ref, kc_ref, vp_ref, vc_ref, bias_ref, o_ref):
    step = pl.program_id(1)
    q_all = (q_ref[0] * (LOG2E * HEAD_DIM ** -0.5)).astype(BF16)
    k_all = jnp.concatenate([kp_ref[0], kc_ref[0]], axis=0).astype(BF16)
    v_all = jnp.concatenate([vp_ref[0], vc_ref[0]], axis=0).astype(BF16)
    masks = _half_masks(WINDOW)
    ones = jnp.ones((2 * WINDOW, LANES), BF16)
    zero = jnp.zeros((), BF16)
    for qb in range(ATTN_Q_BLOCKS):
        rows = slice(qb * WINDOW, (qb + 1) * WINDOW)
        keys = slice(qb * WINDOW, (qb + 2) * WINDOW)
        table = 1 if qb > 0 else jnp.minimum(step, 1)
        for blk in range(KV_DIM // LANES):
            lanes = slice(blk * LANES, (blk + 1) * LANES)
            k2 = k_all[keys, lanes]
            v2 = jnp.concatenate([v_all[keys, lanes], ones], axis=1)
            for c in range(blk * GROUP, (blk + 1) * GROUP):
                qc = q_all[rows, c * LANES:(c + 1) * LANES]
                halves = []
                for half in range(HALVES):
                    p = HALVES * c + half
                    qm = jnp.where(masks[half], qc, zero)
                    s = lax.dot_general(qm, k2, NT_DIMS, preferred_element_type=F32)
                    s = s + bias_ref[table, p]
                    sink = sink_ref[p]
                    m = jnp.maximum(jnp.max(s, axis=-1, keepdims=True), sink)
                    probs = jnp.exp2(s - m).astype(BF16)
                    ov = jnp.dot(probs, v2, preferred_element_type=F32)
                    denom = ov[:, LANES:] + jnp.exp2(sink - m)
                    halves.append(ov[:, :LANES] / denom)
                o_ref[0, rows, c * LANES:(c + 1) * LANES] = jnp.where(
                    masks[0], halves[0], halves[1]).astype(BF16)


def _attn_prompt(qkv, sinks, bias):
    b, l, _ = qkv.shape
    rows = ATTN_Q_BLOCKS * WINDOW
    kcol, vcol = Q_DIM // KV_DIM, Q_DIM // KV_DIM + 1
    prev = lambda i: jnp.maximum(ATTN_Q_BLOCKS * i - 1, 0)
    return pl.pallas_call(
        _attn_prompt_kernel,
        grid=(b, l // rows),
        in_specs=[pl.BlockSpec(memory_space=pltpu.SMEM),
                  pl.BlockSpec((1, rows, Q_DIM), lambda b, i: (b, i, 0)),
                  pl.BlockSpec((1, WINDOW, KV_DIM), lambda b, i: (b, prev(i), kcol)),
                  pl.BlockSpec((1, rows, KV_DIM), lambda b, i: (b, i, kcol)),
                  pl.BlockSpec((1, WINDOW, KV_DIM), lambda b, i: (b, prev(i), vcol)),
                  pl.BlockSpec((1, rows, KV_DIM), lambda b, i: (b, i, vcol)),
                  pl.BlockSpec(bias.shape, lambda b, i: (0, 0, 0, 0))],
        out_specs=pl.BlockSpec((1, rows, Q_DIM), lambda b, i: (b, i, 0)),
        out_shape=jax.ShapeDtypeStruct((b, l, Q_DIM), BF16),
        compiler_params=_params("parallel", "parallel"), name="attn_prompt",
    )(sinks, qkv, qkv, qkv, qkv, qkv, bias)


def _own_slab(ref, owner_of_all, layer):
    if not owner_of_all:
        return ref
    for other in range(ref.shape[0]):
        if other != layer:
            ref[other] = jnp.zeros(ref.shape[1:], ref.dtype)
    return ref.at[layer]


def _stacked_out_specs(prev, n_layers, layer, slab_shape, block, index):
    shape = jax.ShapeDtypeStruct((n_layers,) + slab_shape, F32)
    if prev is None:
        return pl.BlockSpec((n_layers,) + block, lambda b: (0,) + index(b)), shape, [], []
    spec = pl.BlockSpec((None,) + block, lambda b: (layer,) + index(b))
    return spec, shape, [pl.BlockSpec(memory_space=pl.ANY)], [prev]


def _attn_sample_kernel(layer, owner_of_all, q_ref, kn_ref, vn_ref, kt_ref, vt_ref, bo_ref, bn_ref,
                        sink_ref, *rest):
    o_ref, kto_ref, vto_ref = rest[-3:]
    kto_ref = _own_slab(kto_ref, owner_of_all, layer)
    vto_ref = _own_slab(vto_ref, owner_of_all, layer)
    bt, l, _ = q_ref.shape
    masks = _half_masks(l)
    keep = lax.broadcasted_iota(jnp.int32, (KV_DIM, WINDOW), 1) < WINDOW - l
    zero_block = jnp.zeros((l, LANES), F32)
    pad_rows = jnp.zeros((WINDOW - l, KV_DIM), F32)
    pad_keys = jnp.zeros((NEW_KEY_PAD - l, KV_DIM), F32)
    bias_old = bo_ref[...]
    bias_new = bn_ref[...]
    sink = sink_ref[...]

    def body(b, carry):
        q = q_ref[b] * (HEAD_DIM ** -0.5)
        pieces = []
        for p in range(N_HEADS):
            c = p // HALVES
            src = jnp.where(masks[p % HALVES], q[:, c * LANES:(c + 1) * LANES], 0.0)
            pieces.append(jnp.concatenate(
                [src, zero_block] if c // GROUP == 0 else [zero_block, src], axis=1))
        qt = jnp.concatenate(pieces, axis=0).astype(BF16)

        k_old, v_old = kt_ref[b], vt_ref[b]
        k_new, v_new = kn_ref[b], vn_ref[b]
        k_pad = jnp.concatenate([k_new, pad_keys], axis=0).astype(BF16)
        v_pad = jnp.concatenate([v_new, pad_keys], axis=0).astype(BF16)
        s_old = jnp.dot(qt, k_old.astype(BF16), preferred_element_type=F32) + bias_old
        s_new = lax.dot_general(qt, k_pad, NT_DIMS, preferred_element_type=F32) + bias_new
        m = jnp.maximum(sink, jnp.maximum(jnp.max(s_old, axis=-1, keepdims=True),
                                          jnp.max(s_new, axis=-1, keepdims=True)))
        p_old = jnp.exp(s_old - m)
        p_new = jnp.exp(s_new - m)
        denom = (jnp.exp(sink - m) + jnp.sum(p_old, axis=-1, keepdims=True)
                 + jnp.sum(p_new, axis=-1, keepdims=True))
        o = lax.dot_general(p_old.astype(BF16), v_old.astype(BF16), NT_DIMS,
                            preferred_element_type=F32)
        o = o + jnp.dot(p_new.astype(BF16), v_pad, preferred_element_type=F32)
        o = o / denom
        cols = []
        for c in range(N_HEADS // HALVES):
            lanes = slice((c // GROUP) * LANES, (c // GROUP + 1) * LANES)
            parts = [o[(HALVES * c + half) * l:(HALVES * c + half + 1) * l, lanes]
                     for half in range(HALVES)]
            cols.append(jnp.where(masks[0], parts[0], parts[1]))
        o_ref[b] = jnp.concatenate(cols, axis=1).astype(BF16)

        for old, new, out_ref in ((k_old, k_new, kto_ref), (v_old, v_new, vto_ref)):
            shifted = pltpu.roll(old, WINDOW - l, axis=1)
            placed = jnp.concatenate([pad_rows, new], axis=0).T
            out_ref[b] = jnp.where(keep, shifted, placed)
        return carry

    lax.fori_loop(0, bt, body, 0, unroll=4)


def _attn_sample(qkv, cache_kt, cache_vt, layer, bias_old, bias_new, sink_col, prev_kv):
    b, l, _ = qkv.shape
    n_layers = cache_kt.shape[0]
    bt = min(SAMPLE_BATCH_TILE, b)
    kcol, vcol = Q_DIM // KV_DIM, Q_DIM // KV_DIM + 1
    const = lambda b: (0, 0)
    in_specs = [pl.BlockSpec((bt, l, Q_DIM), lambda b: (b, 0, 0)),
                pl.BlockSpec((bt, l, KV_DIM), lambda b: (b, 0, kcol)),
                pl.BlockSpec((bt, l, KV_DIM), lambda b: (b, 0, vcol)),
                pl.BlockSpec((None, bt, KV_DIM, WINDOW), lambda b: (layer, b, 0, 0)),
                pl.BlockSpec((None, bt, KV_DIM, WINDOW), lambda b: (layer, b, 0, 0)),
                pl.BlockSpec(bias_old.shape, const),
                pl.BlockSpec(bias_new.shape, const),
                pl.BlockSpec(sink_col.shape, const)]
    inputs = [qkv, qkv, qkv, cache_kt, cache_vt, bias_old, bias_new, sink_col]
    out_specs = [pl.BlockSpec((bt, l, Q_DIM), lambda b: (b, 0, 0))]
    out_shape = [jax.ShapeDtypeStruct((b, l, Q_DIM), BF16)]
    aliases = {}
    for prev in (prev_kv if prev_kv is not None else (None, None)):
        spec, shape, extra_specs, extra_inputs = _stacked_out_specs(
            prev, n_layers, layer, (b, KV_DIM, WINDOW), (bt, KV_DIM, WINDOW), lambda b: (b, 0, 0))
        if extra_inputs:
            aliases[len(inputs)] = len(out_specs)
        in_specs += extra_specs
        inputs += extra_inputs
        out_specs.append(spec)
        out_shape.append(shape)
    return pl.pallas_call(
        functools.partial(_attn_sample_kernel, layer, prev_kv is None),
        grid=(b // bt,), in_specs=in_specs, out_specs=out_specs, out_shape=out_shape,
        input_output_aliases=aliases,
        compiler_params=_params("parallel"), name="attn_sample",
    )(*inputs)


def _attn_bias_tables(l_sample):
    slopes = jnp.exp2(-8.0 * jnp.arange(1, N_HEADS + 1, dtype=F32) / N_HEADS)
    slopes = slopes[jnp.array(HEAD_ORDER)][:, None, None]
    t = jnp.arange(WINDOW, dtype=jnp.int32)[:, None]
    s = jnp.arange(2 * WINDOW, dtype=jnp.int32)[None, :]
    dist = t + WINDOW - s
    valid = (dist >= 0) & (dist <= WINDOW)
    later = jnp.where(valid[None], -slopes * dist.astype(F32)[None], MASKED)
    first = jnp.where((s >= WINDOW)[None], later, MASKED)
    prompt = jnp.stack([first, later]) * LOG2E
    tq = jnp.arange(l_sample, dtype=jnp.int32)[:, None]
    sk = jnp.arange(WINDOW + NEW_KEY_PAD, dtype=jnp.int32)[None, :]
    dist = tq + WINDOW - sk
    valid = (dist >= 0) & (dist <= WINDOW) & (sk < WINDOW + l_sample)
    sample = jnp.where(valid[None], -slopes * dist.astype(F32)[None], MASKED)
    sample = sample.reshape(N_HEADS * l_sample, WINDOW + NEW_KEY_PAD)
    return prompt, sample[:, :WINDOW], sample[:, WINDOW:]


def _log_sigmoid(x):
    return jnp.minimum(x, 0.0) - jnp.log(1.0 + jnp.exp(-jnp.abs(x)))


def _chunk_cumsum(x, c):
    pos = jnp.bitwise_and(lax.broadcasted_iota(jnp.int32, x.shape, 0), c - 1)
    shift = 1
    while shift < c:
        x = x + jnp.where(pos >= shift, pltpu.roll(x, shift, axis=0), 0.0)
        shift *= 2
    return x


def _gla_proj_kernel(chunk, x_ref, g_ref, wm_ref, wgd_ref, wg_ref, bg_ref, o_ref, b_ref):
    h = _rms(x_ref[...], g_ref[...]).astype(BF16)
    gd = jnp.dot(h, wgd_ref[...], preferred_element_type=F32)
    pre = jnp.dot(gd.astype(BF16), wg_ref[...].astype(BF16), preferred_element_type=F32) + bg_ref[...]
    b_ref[...] = _chunk_cumsum(_log_sigmoid(pre) / GLA_GATE_NORMALIZER, chunk)
    o_ref[...] = jnp.dot(h, wm_ref[...], preferred_element_type=F32)


def _gla_proj(x, g, w_main, w_gd, w_gate, b_gate, layers, chunk, tm):
    t = x.shape[0]
    lg, lw = layers
    n_main = w_main.shape[2]
    return pl.pallas_call(
        functools.partial(_gla_proj_kernel, chunk),
        grid=(t // tm,),
        in_specs=[pl.BlockSpec((tm, D_MODEL), lambda i: (i, 0)),
                  pl.BlockSpec((None, 1, D_MODEL), lambda i: (lg, 0, 0)),
                  pl.BlockSpec((None, D_MODEL, n_main), lambda i: (lw, 0, 0)),
                  pl.BlockSpec((None, D_MODEL, GLA_GATE_RANK), lambda i: (lw, 0, 0)),
                  pl.BlockSpec((None, GLA_GATE_RANK, GLA_DK), lambda i: (lw, 0, 0)),
                  pl.BlockSpec((None, 1, GLA_DK), lambda i: (lw, 0, 0))],
        out_specs=[pl.BlockSpec((tm, n_main), lambda i: (i, 0)),
                   pl.BlockSpec((tm, GLA_DK), lambda i: (i, 0))],
        out_shape=[jax.ShapeDtypeStruct((t, n_main), F32),
                   jax.ShapeDtypeStruct((t, GLA_DK), F32)],
        compiler_params=_params("parallel"), name="gla_proj",
    )(x, g, w_main, w_gd, w_gate, b_gate)


def _gla_gate_out(o, r, gn):
    return (_rms(o, gn) * _silu(r)).astype(BF16)


def _gla_prompt_kernel(q_ref, k_ref, v_ref, r_ref, b_ref, gn_ref, o_ref, s_ref, st_ref):
    c = GLA_CHUNK
    rows = GLA_GROUP * c
    n_groups = q_ref.shape[1] // rows
    row = lax.broadcasted_iota(jnp.int32, (rows, rows), 0)
    col = lax.broadcasted_iota(jnp.int32, (rows, rows), 1)
    causal = (row >= col) & (col >= row - jnp.bitwise_and(row, c - 1))
    gn = gn_ref[...]
    st_ref[...] = jnp.zeros_like(st_ref)

    def body(g, carry):
        sl = pl.ds(pl.multiple_of(g * rows, rows), rows)
        q = q_ref[0, sl, :] * (GLA_DK_HEAD ** -0.5)
        k = k_ref[0, sl, :]
        v_bf = v_ref[0, sl, :].astype(BF16)
        b = b_ref[0, sl, :]
        per_chunk = lambda r0: jnp.concatenate(
            [jnp.broadcast_to(b[j * c + r0:j * c + r0 + 1, :], (c, GLA_DK_HEAD))
             for j in range(GLA_GROUP)], axis=0)
        b_mid = per_chunk(c // 2)
        b_end = per_chunk(c - 1)
        qa = (q * jnp.exp(b - b_mid)).astype(BF16)
        ka = (k * jnp.exp(b_mid - b)).astype(BF16)
        a = lax.dot_general(qa, ka, NT_DIMS, preferred_element_type=F32)
        a = jnp.where(causal, a, 0.0).astype(BF16)
        o_intra = jnp.dot(a, v_bf, preferred_element_type=F32)
        qb = (q * jnp.exp(b)).astype(BF16)
        k_end = (k * jnp.exp(b_end - b)).astype(BF16)
        st = st_ref[...]
        o_inter = []
        for j in range(GLA_GROUP):
            rs = slice(j * c, (j + 1) * c)
            o_inter.append(lax.dot_general(qb[rs], st.astype(BF16), NT_DIMS,
                                           preferred_element_type=F32))
            upd = lax.dot_general(v_bf[rs], k_end[rs], TN_DIMS, preferred_element_type=F32)
            st = st * jnp.exp(b[(j + 1) * c - 1:(j + 1) * c, :]) + upd
        st_ref[...] = st
        o = o_intra + jnp.concatenate(o_inter, axis=0)
        o_ref[0, sl, :] = _gla_gate_out(o, r_ref[0, sl, :], gn)
        return carry

    lax.fori_loop(0, n_groups, body, 0, unroll=4)
    s_ref[0, 0] = st_ref[...].T


def _gla_prompt(proj, decay, g_norm, layer):
    b, l, _ = proj.shape
    kcol = GLA_DK // GLA_DK_HEAD
    vcol = 2 * GLA_DK // GLA_DV_HEAD
    rcol = vcol + GLA_DV // GLA_DV_HEAD
    return pl.pallas_call(
        _gla_prompt_kernel,
        grid=(b, GLA_HEADS),
        in_specs=[pl.BlockSpec((1, l, GLA_DK_HEAD), lambda b, h: (b, 0, h)),
                  pl.BlockSpec((1, l, GLA_DK_HEAD), lambda b, h: (b, 0, kcol + h)),
                  pl.BlockSpec((1, l, GLA_DV_HEAD), lambda b, h: (b, 0, vcol + h)),
                  pl.BlockSpec((1, l, GLA_DV_HEAD), lambda b, h: (b, 0, rcol + h)),
                  pl.BlockSpec((1, l, GLA_DK_HEAD), lambda b, h: (b, 0, h)),
                  pl.BlockSpec((None, 1, GLA_DV_HEAD), lambda b, h: (layer, 0, 0))],
        out_specs=[pl.BlockSpec((1, l, GLA_DV_HEAD), lambda b, h: (b, 0, h)),
                   pl.BlockSpec((1, 1, GLA_DK_HEAD, GLA_DV_HEAD), lambda b, h: (b, h, 0, 0))],
        out_shape=[jax.ShapeDtypeStruct((b, l, GLA_DV), BF16),
                   jax.ShapeDtypeStruct((b, GLA_HEADS, GLA_DK_HEAD, GLA_DV_HEAD), F32)],
        scratch_shapes=[pltpu.VMEM((GLA_DV_HEAD, GLA_DK_HEAD), F32)],
        compiler_params=_params("parallel", "parallel"), name="gla_prompt",
    )(proj, proj, proj, proj, decay, g_norm)


def _gla_sample_kernel(layer, owner_of_all, q_ref, k_ref, v_ref, r_ref, b_ref, s0_ref, gn_ref, *rest):
    o_ref, s_ref = rest[-2:]
    s_ref = _own_slab(s_ref, owner_of_all, layer)
    bt, c, _ = q_ref.shape
    row = lax.broadcasted_iota(jnp.int32, (c, c), 0)
    col = lax.broadcasted_iota(jnp.int32, (c, c), 1)
    causal = row >= col
    gn = gn_ref[...]

    def body(i, carry):
        q = q_ref[i] * (GLA_DK_HEAD ** -0.5)
        k = k_ref[i]
        v = v_ref[i]
        r = r_ref[i]
        b = b_ref[i]
        b_mid = b[c // 2:c // 2 + 1, :]
        b_end = b[c - 1:c, :]
        qa = q * jnp.exp(b - b_mid)
        ka = k * jnp.exp(b_mid - b)
        qb = (q * jnp.exp(b)).astype(BF16)
        k_end_t = (k * jnp.exp(b_end - b)).T
        dec_col = jnp.broadcast_to(jnp.exp(b_end), (c, GLA_DK)).T[:, 0:1]
        for h in range(GLA_HEADS):
            ks = slice(h * GLA_DK_HEAD, (h + 1) * GLA_DK_HEAD)
            vs = slice(h * GLA_DV_HEAD, (h + 1) * GLA_DV_HEAD)
            s0 = s0_ref[i, h]
            a = lax.dot_general(qa[:, ks], ka[:, ks], NT_DIMS, preferred_element_type=F32)
            a = jnp.where(causal, a, 0.0)
            o = jnp.dot(a, v[:, vs], preferred_element_type=F32)
            o = o + jnp.dot(qb[:, ks], s0.astype(BF16), preferred_element_type=F32)
            upd = jnp.dot(k_end_t[ks, :], v[:, vs], preferred_element_type=F32)
            s_ref[i, h] = s0 * dec_col[ks, :] + upd
            o_ref[i, :, vs] = _gla_gate_out(o, r[:, vs], gn)
        return carry

    lax.fori_loop(0, bt, body, 0, unroll=2)


def _gla_sample(proj, decay, state, g_norm, layer, prev_state):
    b, l, _ = proj.shape
    bt = min(SAMPLE_BATCH_TILE, b)
    state_block = (bt, GLA_HEADS, GLA_DK_HEAD, GLA_DV_HEAD)
    in_specs = [pl.BlockSpec((bt, l, GLA_DK), lambda b: (b, 0, 0)),
                pl.BlockSpec((bt, l, GLA_DK), lambda b: (b, 0, 1)),
                pl.BlockSpec((bt, l, GLA_DV), lambda b: (b, 0, 1)),
                pl.BlockSpec((bt, l, GLA_DV), lambda b: (b, 0, 2)),
                pl.BlockSpec((bt, l, GLA_DK), lambda b: (b, 0, 0)),
                pl.BlockSpec((None,) + state_block, lambda b: (layer, b, 0, 0, 0)),
                pl.BlockSpec((None, 1, GLA_DV_HEAD), lambda b: (layer, 0, 0))]
    inputs = [proj, proj, proj, proj, decay, state, g_norm]
    spec, shape, extra_specs, extra_inputs = _stacked_out_specs(
        prev_state, state.shape[0], layer, state.shape[1:], state_block, lambda b: (b, 0, 0, 0))
    return pl.pallas_call(
        functools.partial(_gla_sample_kernel, layer, prev_state is None),
        grid=(b // bt,),
        in_specs=in_specs + extra_specs,
        out_specs=[pl.BlockSpec((bt, l, GLA_DV), lambda b: (b, 0, 0)), spec],
        out_shape=[jax.ShapeDtypeStruct((b, l, GLA_DV), BF16), shape],
        input_output_aliases={len(inputs): 1} if extra_inputs else {},
        compiler_params=_params("parallel"), name="gla_sample",
    )(*inputs, *extra_inputs)


def kernel(x_prompt, x_sample, cache_k, cache_v, state_gla, norm_mix, norm_ffn, norm_final,
           w_attn_in, w_attn_out, attn_sinks, w_gla_in, w_gla_gate_up, b_gla_gate, gla_out_norm,
           w_gla_out, w_ffn_in, w_ffn_out):
    bp, lp, _ = x_prompt.shape
    bs, ls, _ = x_sample.shape
    depth = norm_mix.shape[0]
    n_attn = cache_k.shape[0]
    yp = x_prompt.reshape(bp * lp, D_MODEL)
    ys = x_sample.reshape(bs * ls, D_MODEL)
    tp = min(TOKEN_TILE, bp * lp)
    ts = min(TOKEN_TILE, bs * ls)
    bias_p, bias_old, bias_new = _attn_bias_tables(ls)
    n_main = 2 * GLA_DK + 2 * GLA_DV

    g_mix = norm_mix.reshape(depth, 1, D_MODEL)
    g_ffn = norm_ffn.reshape(depth, 1, D_MODEL)
    g_fin = norm_final.reshape(1, D_MODEL)
    order = jnp.array(HEAD_ORDER)
    w_q = w_attn_in[:, :, :Q_DIM].reshape(n_attn, D_MODEL, N_HEADS, HEAD_DIM)[:, :, order, :]
    w_attn_in_bf = jnp.concatenate(
        [w_q.reshape(n_attn, D_MODEL, Q_DIM), w_attn_in[:, :, Q_DIM:]], axis=2).astype(BF16)
    w_attn_out_bf = w_attn_out.reshape(n_attn, N_HEADS, HEAD_DIM, D_MODEL)[:, order].reshape(
        n_attn, Q_DIM, D_MODEL).astype(BF16)
    sinks = attn_sinks[:, order]
    w_gla_main_bf = w_gla_in[:, :, :n_main].astype(BF16)
    w_gla_gd_bf = w_gla_in[:, :, n_main:].astype(BF16)
    w_gla_out_bf = w_gla_out.astype(BF16)
    w_ffn_in_bf = w_ffn_in.astype(BF16)
    w_ffn_out_bf = w_ffn_out.astype(BF16)
    b_gate = b_gla_gate.reshape(-1, 1, GLA_DK)
    g_gla = gla_out_norm.reshape(-1, 1, GLA_DV_HEAD)
    cache_kt = jnp.transpose(cache_k, (0, 1, 3, 4, 2)).reshape(n_attn, bs, KV_DIM, WINDOW)
    cache_vt = jnp.transpose(cache_v, (0, 1, 3, 4, 2)).reshape(n_attn, bs, KV_DIM, WINDOW)

    kp_l, vp_l, sp_l = [], [], []
    new_kv, new_state = None, None
    for i in range(depth):
        j = i // 2
        if i % 2 == 0:
            w_mix = w_attn_out_bf
            (qkv_p,) = _norm_matmul(yp, g_mix, [w_attn_in_bf], (i, j), tp)
            (qkv_s,) = _norm_matmul(ys, g_mix, [w_attn_in_bf], (i, j), ts)
            qkv_p = qkv_p.reshape(bp, lp, Q_DIM + 2 * KV_DIM)
            qkv_s = qkv_s.reshape(bs, ls, Q_DIM + 2 * KV_DIM)
            op = _attn_prompt(qkv_p, sinks[j] * LOG2E, bias_p)
            sink_col = jnp.repeat(sinks[j], ls).reshape(N_HEADS * ls, 1)
            os_, *new_kv = _attn_sample(qkv_s, cache_kt, cache_vt, j, bias_old, bias_new, sink_col,
                                        new_kv)
            kv_shape = (bp, WINDOW, N_KV_HEADS, HEAD_DIM)
            kp_l.append(qkv_p[:, lp - WINDOW:, Q_DIM:Q_DIM + KV_DIM].reshape(kv_shape))
            vp_l.append(qkv_p[:, lp - WINDOW:, Q_DIM + KV_DIM:].reshape(kv_shape))
        else:
            w_mix = w_gla_out_bf
            proj_p, dec_p = _gla_proj(yp, g_mix, w_gla_main_bf, w_gla_gd_bf, w_gla_gate_up, b_gate,
                                      (i, j), GLA_CHUNK, tp)
            proj_s, dec_s = _gla_proj(ys, g_mix, w_gla_main_bf, w_gla_gd_bf, w_gla_gate_up, b_gate,
                                      (i, j), ls, ts)
            op, sp = _gla_prompt(proj_p.reshape(bp, lp, n_main), dec_p.reshape(bp, lp, GLA_DK),
                                 g_gla, j)
            os_, new_state = _gla_sample(proj_s.reshape(bs, ls, n_main),
                                         dec_s.reshape(bs, ls, GLA_DK), state_gla, g_gla, j, new_state)
            sp_l.append(sp)
        last = i == depth - 1
        yp = _mix_out_ffn(yp, op.reshape(bp * lp, D_MODEL), w_mix, g_ffn, w_ffn_in_bf, w_ffn_out_bf,
                          g_fin, (j, i), last, min(FFN_TOKEN_TILE, bp * lp))
        ys = _mix_out_ffn(ys, os_.reshape(bs * ls, D_MODEL), w_mix, g_ffn, w_ffn_in_bf, w_ffn_out_bf,
                          g_fin, (j, i), last, min(FFN_TOKEN_TILE, bs * ls))

    def window_major(xt):
        return jnp.transpose(xt.reshape(n_attn, bs, N_KV_HEADS, HEAD_DIM, WINDOW), (0, 1, 4, 2, 3))

    return (yp.reshape(bp, lp, D_MODEL), ys.reshape(bs, ls, D_MODEL),
            jnp.stack(kp_l), jnp.stack(vp_l), jnp.stack(sp_l),
            window_major(new_kv[0]), window_major(new_kv[1]), new_state)
```

```python
import functools

import jax
import jax.numpy as jnp
from jax import lax
from jax.experimental import pallas as pl
from jax.experimental.pallas import tpu as pltpu

F32 = jnp.float32
BF16 = jnp.bfloat16

D_MODEL = 1024
EPS = 1e-6
WINDOW = 128
HEAD_DIM = 64
N_HEADS = 16
N_KV_HEADS = 4
GROUP = 4
Q_DIM = 1024
KV_DIM = 256
GLA_HEADS = 4
GLA_DK = 512
GLA_DV = 1024
GLA_DK_HEAD = 128
GLA_DV_HEAD = 256
GLA_GATE_RANK = 16
GLA_GATE_NORMALIZER = 16.0
GLA_CHUNK = 64
D_FF = 2816
MASKED = -1e30
LOG2E = 1.4426950408889634

VMEM_LIMIT_BYTES = 52 * 1024 * 1024
LANES = 128
BF16_SUBLANES = 16
TOKEN_TILE = 1024
PROJ_SUBTILE = 256
FFN_TOKEN_TILE = 512
FFN_TILE = 256
ATTN_Q_BLOCKS = 2
GLA_GROUP = 4
SAMPLE_BATCH_TILE = 8
NEW_KEY_PAD = 16

NT_DIMS = (((1,), (1,)), ((), ()))
TN_DIMS = (((0,), (0,)), ((), ()))


def _params(*sem):
    return pltpu.CompilerParams(dimension_semantics=sem, vmem_limit_bytes=VMEM_LIMIT_BYTES)


def _rms(x, g):
    r = lax.rsqrt(jnp.mean(x * x, axis=-1, keepdims=True) + EPS)
    return x * r * g


def _silu(x):
    return x / (1.0 + jnp.exp(-x))


def _norm_matmul_kernel(n, x_ref, g_ref, *refs):
    for r in range(0, x_ref.shape[0], PROJ_SUBTILE):
        rows = slice(r, r + PROJ_SUBTILE)
        h = _rms(x_ref[rows, :], g_ref[...]).astype(BF16)
        for w_ref, o_ref in zip(refs[:n], refs[n:]):
            o_ref[rows, :] = jnp.dot(h, w_ref[...], preferred_element_type=F32)


def _norm_matmul(x, g, ws, layers, tm):
    t = x.shape[0]
    lg, lw = layers
    in_specs = [pl.BlockSpec((tm, D_MODEL), lambda i: (i, 0)),
                pl.BlockSpec((None, 1, D_MODEL), lambda i: (lg, 0, 0))]
    in_specs += [pl.BlockSpec((None,) + w.shape[1:], lambda i: (lw, 0, 0)) for w in ws]
    out_specs = [pl.BlockSpec((tm, w.shape[2]), lambda i: (i, 0)) for w in ws]
    out_shape = [jax.ShapeDtypeStruct((t, w.shape[2]), F32) for w in ws]
    return pl.pallas_call(
        functools.partial(_norm_matmul_kernel, len(ws)),
        grid=(t // tm,), in_specs=in_specs, out_specs=out_specs, out_shape=out_shape,
        compiler_params=_params("parallel"), name="norm_matmul",
    )(x, g, *ws)


def _ffn_kernel(final_norm, convert_next, x_ref, a_ref, wm_ref, g_ref, wi_ref, wo_ref, gf_ref, *rest):
    if convert_next:
        nwi_ref, nwo_ref, o_ref, owi_ref, owo_ref, act_ref = rest
        owi_ref[...] = nwi_ref[...].astype(BF16)
        owo_ref[...] = nwo_ref[...].astype(BF16)
    else:
        o_ref, act_ref = rest
    y = x_ref[...] + jnp.dot(a_ref[...], wm_ref[...], preferred_element_type=F32)
    h = _rms(y, g_ref[...]).astype(BF16)
    for j in range(D_FF // FFN_TILE):
        gate = jnp.dot(h, wi_ref[:, j * FFN_TILE:(j + 1) * FFN_TILE], preferred_element_type=F32)
        up = jnp.dot(h, wi_ref[:, D_FF + j * FFN_TILE:D_FF + (j + 1) * FFN_TILE],
                     preferred_element_type=F32)
        act_ref[:, j * FFN_TILE:(j + 1) * FFN_TILE] = (_silu(gate) * up).astype(BF16)
    y = y + jnp.dot(act_ref[...], wo_ref[...], preferred_element_type=F32)
    if final_norm:
        y = _rms(y, gf_ref[...])
    o_ref[...] = y


def _resident(block_shape, index_map):
    return pl.BlockSpec(block_shape, index_map, pipeline_mode=pl.Buffered(1))


def _slab_rows(total, steps):
    rows = BF16_SUBLANES
    while total % rows or total // rows > steps:
        rows += BF16_SUBLANES
    return rows


def _mix_out_ffn(x, a, w_mix, g, w_in, w_out, g_final, layers, final_norm, tm, next_f32=None):
    t = x.shape[0]
    lm, lf = layers
    steps = t // tm
    row = lambda i: (i, 0)
    in_specs = [pl.BlockSpec((tm, D_MODEL), row),
                pl.BlockSpec((tm, D_MODEL), row),
                _resident((None, D_MODEL, D_MODEL), lambda i: (lm, 0, 0)),
                _resident((None, 1, D_MODEL), lambda i: (lf, 0, 0)),
                _resident((D_MODEL, 2 * D_FF), lambda i: (0, 0)),
                _resident((D_FF, D_MODEL), lambda i: (0, 0)),
                _resident((1, D_MODEL), lambda i: (0, 0))]
    inputs = [x, a, w_mix, g, w_in, w_out, g_final]
    out_specs = [pl.BlockSpec((tm, D_MODEL), row)]
    out_shape = [jax.ShapeDtypeStruct((t, D_MODEL), F32)]
    if next_f32 is not None:
        for w_all in next_f32:
            rows = _slab_rows(w_all.shape[1], steps)
            last = w_all.shape[1] // rows - 1
            slab = lambda i, last=last: (jnp.minimum(i, last), 0)
            in_specs.append(pl.BlockSpec((None, rows, w_all.shape[2]),
                                         lambda i, slab=slab: (lf + 1,) + slab(i)))
            inputs.append(w_all)
            out_specs.append(pl.BlockSpec((rows, w_all.shape[2]), slab))
            out_shape.append(jax.ShapeDtypeStruct(w_all.shape[1:], BF16))
    return pl.pallas_call(
        functools.partial(_ffn_kernel, final_norm, next_f32 is not None),
        grid=(steps,), in_specs=in_specs, out_specs=out_specs, out_shape=out_shape,
        scratch_shapes=[pltpu.VMEM((tm, D_FF), BF16)],
        compiler_params=_params("arbitrary"), name="mix_out_ffn",
    )(*inputs)


HALVES = LANES // HEAD_DIM
HEAD_ORDER = tuple((HALVES * (p // (HALVES * GROUP)) + p % HALVES) * GROUP + (p % (HALVES * GROUP)) // HALVES
                   for p in range(N_HEADS))


def _regroup_heads(x, axis, width):
    lead, tail = x.shape[:axis], x.shape[axis + 1:]
    x = x.reshape(lead + (N_KV_HEADS // HALVES, HALVES, GROUP, width) + tail)
    x = jnp.swapaxes(x, axis + 1, axis + 2)
    return x.reshape(lead + (N_HEADS * width,) + tail)


def _half_masks(rows):
    low = lax.broadcasted_iota(jnp.int32, (rows, LANES), 1) < HEAD_DIM
    return low, jnp.logical_not(low)


def _attn_prompt_kernel(sink_ref, q_ref, kp_ref, kc_ref, vp_ref, vc_ref, bias_ref, o_ref):
    step = pl.program_id(1)
    q_all = (q_ref[0] * (LOG2E * HEAD_DIM ** -0.5)).astype(BF16)
    k_all = jnp.concatenate([kp_ref[0], kc_ref[0]], axis=0).astype(BF16)
    v_all = jnp.concatenate([vp_ref[0], vc_ref[0]], axis=0).astype(BF16)
    masks = _half_masks(WINDOW)
    ones = jnp.ones((2 * WINDOW, LANES), BF16)
    zero = jnp.zeros((), BF16)
    for qb in range(ATTN_Q_BLOCKS):
        rows = slice(qb * WINDOW, (qb + 1) * WINDOW)
        keys = slice(qb * WINDOW, (qb + 2) * WINDOW)
        table = 1 if qb > 0 else jnp.minimum(step, 1)
        for blk in range(KV_DIM // LANES):
            lanes = slice(blk * LANES, (blk + 1) * LANES)
            k2 = k_all[keys, lanes]
            v2 = jnp.concatenate([v_all[keys, lanes], ones], axis=1)
            for c in range(blk * GROUP, (blk + 1) * GROUP):
                qc = q_all[rows, c * LANES:(c + 1) * LANES]
                halves = []
                for half in range(HALVES):
                    p = HALVES * c + half
                    qm = jnp.where(masks[half], qc, zero)
                    s = lax.dot_general(qm, k2, NT_DIMS, preferred_element_type=F32)
                    s = s + bias_ref[table, p]
                    sink = sink_ref[p]
                    m = jnp.maximum(jnp.max(s, axis=-1, keepdims=True), sink)
                    probs = jnp.exp2(s - m).astype(BF16)
                    ov = jnp.dot(probs, v2, preferred_element_type=F32)
                    denom = ov[:, LANES:] + jnp.exp2(sink - m)
                    halves.append(ov[:, :LANES] / denom)
                o_ref[0, rows, c * LANES:(c + 1) * LANES] = jnp.where(
                    masks[0], halves[0], halves[1]).astype(BF16)


def _attn_prompt(qkv, sinks, bias):
    b, l, _ = qkv.shape
    rows = ATTN_Q_BLOCKS * WINDOW
    kcol, vcol = Q_DIM // KV_DIM, Q_DIM // KV_DIM + 1
    prev = lambda i: jnp.maximum(ATTN_Q_BLOCKS * i - 1, 0)
    return pl.pallas_call(
        _attn_prompt_kernel,
        grid=(b, l // rows),
        in_specs=[pl.BlockSpec(memory_space=pltpu.SMEM),
                  pl.BlockSpec((1, rows, Q_DIM), lambda b, i: (b, i, 0)),
                  pl.BlockSpec((1, WINDOW, KV_DIM), lambda b, i: (b, prev(i), kcol)),
                  pl.BlockSpec((1, rows, KV_DIM), lambda b, i: (b, i, kcol)),
                  pl.BlockSpec((1, WINDOW, KV_DIM), lambda b, i: (b, prev(i), vcol)),
                  pl.BlockSpec((1, rows, KV_DIM), lambda b, i: (b, i, vcol)),
                  pl.BlockSpec(bias.shape, lambda b, i: (0, 0, 0, 0))],
        out_specs=pl.BlockSpec((1, rows, Q_DIM), lambda b, i: (b, i, 0)),
        out_shape=jax.ShapeDtypeStruct((b, l, Q_DIM), BF16),
        compiler_params=_params("parallel", "parallel"), name="attn_prompt",
    )(sinks, qkv, qkv, qkv, qkv, qkv, bias)


def _own_slab(ref, owner_of_all, layer):
    if not owner_of_all:
        return ref
    for other in range(ref.shape[0]):
        if other != layer:
            ref[other] = jnp.zeros(ref.shape[1:], ref.dtype)
    return ref.at[layer]


def _stacked_out_specs(prev, n_layers, layer, slab_shape, block, index):
    shape = jax.ShapeDtypeStruct((n_layers,) + slab_shape, F32)
    if prev is None:
        return pl.BlockSpec((n_layers,) + block, lambda b: (0,) + index(b)), shape, [], []
    spec = pl.BlockSpec((None,) + block, lambda b: (layer,) + index(b))
    return spec, shape, [pl.BlockSpec(memory_space=pl.ANY)], [prev]


def _attn_sample_kernel(layer, owner_of_all, q_ref, kn_ref, vn_ref, kt_ref, vt_ref, bo_ref, bn_ref,
                        sink_ref, *rest):
    o_ref, kto_ref, vto_ref = rest[-3:]
    kto_ref = _own_slab(kto_ref, owner_of_all, layer)
    vto_ref = _own_slab(vto_ref, owner_of_all, layer)
    bt, l, _ = q_ref.shape
    masks = _half_masks(l)
    keep = lax.broadcasted_iota(jnp.int32, (KV_DIM, WINDOW), 1) < WINDOW - l
    zero_block = jnp.zeros((l, LANES), F32)
    pad_rows = jnp.zeros((WINDOW - l, KV_DIM), F32)
    pad_keys = jnp.zeros((NEW_KEY_PAD - l, KV_DIM), F32)
    ones_old = jnp.ones((LANES, WINDOW), BF16)
    ones_new = jnp.ones((NEW_KEY_PAD, LANES), BF16)
    bias_old = bo_ref[...]
    bias_new = bn_ref[...]
    sink = sink_ref[...]

    def body(b, carry):
        q = q_ref[b] * (HEAD_DIM ** -0.5)
        pieces = []
        for p in range(N_HEADS):
            c = p // HALVES
            src = jnp.where(masks[p % HALVES], q[:, c * LANES:(c + 1) * LANES], 0.0)
            pieces.append(jnp.concatenate(
                [src, zero_block] if c // GROUP == 0 else [zero_block, src], axis=1))
        qt = jnp.concatenate(pieces, axis=0).astype(BF16)

        k_old, v_old = kt_ref[b], vt_ref[b]
        k_new, v_new = kn_ref[b], vn_ref[b]
        k_pad = jnp.concatenate([k_new, pad_keys], axis=0).astype(BF16)
        v_pad = jnp.concatenate([v_new, pad_keys], axis=0).astype(BF16)
        s_old = jnp.dot(qt, k_old.astype(BF16), preferred_element_type=F32) + bias_old
        s_new = lax.dot_general(qt, k_pad, NT_DIMS, preferred_element_type=F32) + bias_new
        m = jnp.maximum(sink, jnp.maximum(jnp.max(s_old, axis=-1, keepdims=True),
                                          jnp.max(s_new, axis=-1, keepdims=True)))
        p_old = jnp.exp(s_old - m).astype(BF16)
        p_new = jnp.exp(s_new - m).astype(BF16)
        v_old_1 = jnp.concatenate([v_old.astype(BF16), ones_old], axis=0)
        v_new_1 = jnp.concatenate([v_pad, ones_new], axis=1)
        o = lax.dot_general(p_old, v_old_1, NT_DIMS, preferred_element_type=F32)
        o = o + jnp.dot(p_new, v_new_1, preferred_element_type=F32)
        inv = 1.0 / (o[:, KV_DIM:] + jnp.exp(sink - m))
        cols = []
        for c in range(N_HEADS // HALVES):
            lanes = slice((c // GROUP) * LANES, (c // GROUP + 1) * LANES)
            parts = [o[(HALVES * c + half) * l:(HALVES * c + half + 1) * l, lanes]
                     * inv[(HALVES * c + half) * l:(HALVES * c + half + 1) * l, :]
                     for half in range(HALVES)]
            cols.append(jnp.where(masks[0], parts[0], parts[1]))
        o_ref[b] = jnp.concatenate(cols, axis=1).astype(BF16)

        for old, new, out_ref in ((k_old, k_new, kto_ref), (v_old, v_new, vto_ref)):
            shifted = pltpu.roll(old, WINDOW - l, axis=1)
            placed = jnp.concatenate([pad_rows, new], axis=0).T
            out_ref[b] = jnp.where(keep, shifted, placed)
        return carry

    lax.fori_loop(0, bt, body, 0, unroll=4)


def _attn_sample(qkv, cache_kt, cache_vt, layer, bias_old, bias_new, sink_col, prev_kv):
    b, l, _ = qkv.shape
    n_layers = cache_kt.shape[0]
    bt = min(SAMPLE_BATCH_TILE, b)
    kcol, vcol = Q_DIM // KV_DIM, Q_DIM // KV_DIM + 1
    const = lambda b: (0, 0)
    in_specs = [pl.BlockSpec((bt, l, Q_DIM), lambda b: (b, 0, 0)),
                pl.BlockSpec((bt, l, KV_DIM), lambda b: (b, 0, kcol)),
                pl.BlockSpec((bt, l, KV_DIM), lambda b: (b, 0, vcol)),
                pl.BlockSpec((None, bt, KV_DIM, WINDOW), lambda b: (layer, b, 0, 0)),
                pl.BlockSpec((None, bt, KV_DIM, WINDOW), lambda b: (layer, b, 0, 0)),
                pl.BlockSpec(bias_old.shape, const),
                pl.BlockSpec(bias_new.shape, const),
                pl.BlockSpec(sink_col.shape, const)]
    inputs = [qkv, qkv, qkv, cache_kt, cache_vt, bias_old, bias_new, sink_col]
    out_specs = [pl.BlockSpec((bt, l, Q_DIM), lambda b: (b, 0, 0))]
    out_shape = [jax.ShapeDtypeStruct((b, l, Q_DIM), BF16)]
    aliases = {}
    for prev in (prev_kv if prev_kv is not None else (None, None)):
        spec, shape, extra_specs, extra_inputs = _stacked_out_specs(
            prev, n_layers, layer, (b, KV_DIM, WINDOW), (bt, KV_DIM, WINDOW), lambda b: (b, 0, 0))
        if extra_inputs:
            aliases[len(inputs)] = len(out_specs)
        in_specs += extra_specs
        inputs += extra_inputs
        out_specs.append(spec)
        out_shape.append(shape)
    return pl.pallas_call(
        functools.partial(_attn_sample_kernel, layer, prev_kv is None),
        grid=(b // bt,), in_specs=in_specs, out_specs=out_specs, out_shape=out_shape,
        input_output_aliases=aliases,
        compiler_params=_params("parallel"), name="attn_sample",
    )(*inputs)


def _attn_bias_tables(l_sample):
    slopes = jnp.exp2(-8.0 * jnp.arange(1, N_HEADS + 1, dtype=F32) / N_HEADS)
    slopes = slopes[jnp.array(HEAD_ORDER)][:, None, None]
    t = jnp.arange(WINDOW, dtype=jnp.int32)[:, None]
    s = jnp.arange(2 * WINDOW, dtype=jnp.int32)[None, :]
    dist = t + WINDOW - s
    valid = (dist >= 0) & (dist <= WINDOW)
    later = jnp.where(valid[None], -slopes * dist.astype(F32)[None], MASKED)
    first = jnp.where((s >= WINDOW)[None], later, MASKED)
    prompt = jnp.stack([first, later]) * LOG2E
    tq = jnp.arange(l_sample, dtype=jnp.int32)[:, None]
    sk = jnp.arange(WINDOW + NEW_KEY_PAD, dtype=jnp.int32)[None, :]
    dist = tq + WINDOW - sk
    valid = (dist >= 0) & (dist <= WINDOW) & (sk < WINDOW + l_sample)
    sample = jnp.where(valid[None], -slopes * dist.astype(F32)[None], MASKED)
    sample = sample.reshape(N_HEADS * l_sample, WINDOW + NEW_KEY_PAD)
    return prompt, sample[:, :WINDOW], sample[:, WINDOW:]


def _log_sigmoid(x):
    return jnp.minimum(x, 0.0) - jnp.log(1.0 + jnp.exp(-jnp.abs(x)))


def _chunk_cumsum(x, c):
    pos = jnp.bitwise_and(lax.broadcasted_iota(jnp.int32, x.shape, 0), c - 1)
    shift = 1
    while shift < c:
        x = x + jnp.where(pos >= shift, pltpu.roll(x, shift, axis=0), 0.0)
        shift *= 2
    return x


def _gla_proj_kernel(chunk, x_ref, g_ref, wm_ref, wgd_ref, wg_ref, bg_ref, o_ref, b_ref):
    wg = wg_ref[...].astype(BF16)
    for r in range(0, x_ref.shape[0], PROJ_SUBTILE):
        rows = slice(r, r + PROJ_SUBTILE)
        h = _rms(x_ref[rows, :], g_ref[...]).astype(BF16)
        gd = jnp.dot(h, wgd_ref[...], preferred_element_type=F32)
        pre = jnp.dot(gd.astype(BF16), wg, preferred_element_type=F32) + bg_ref[...]
        b_ref[rows, :] = _chunk_cumsum(_log_sigmoid(pre) / GLA_GATE_NORMALIZER, chunk)
        o_ref[rows, :] = jnp.dot(h, wm_ref[...], preferred_element_type=F32)


def _gla_proj(x, g, w_main, w_gd, w_gate, b_gate, layers, chunk, tm):
    t = x.shape[0]
    lg, lw = layers
    n_main = 2 * GLA_DK + 2 * GLA_DV
    return pl.pallas_call(
        functools.partial(_gla_proj_kernel, chunk),
        grid=(t // tm,),
        in_specs=[pl.BlockSpec((tm, D_MODEL), lambda i: (i, 0)),
                  pl.BlockSpec((None, 1, D_MODEL), lambda i: (lg, 0, 0)),
                  pl.BlockSpec((None, D_MODEL, n_main), lambda i: (lw, 0, 0)),
                  pl.BlockSpec((None, D_MODEL, GLA_GATE_RANK), lambda i: (lw, 0, 0)),
                  pl.BlockSpec((None, GLA_GATE_RANK, GLA_DK), lambda i: (lw, 0, 0)),
                  pl.BlockSpec((None, 1, GLA_DK), lambda i: (lw, 0, 0))],
        out_specs=[pl.BlockSpec((tm, n_main), lambda i: (i, 0)),
                   pl.BlockSpec((tm, GLA_DK), lambda i: (i, 0))],
        out_shape=[jax.ShapeDtypeStruct((t, n_main), F32),
                   jax.ShapeDtypeStruct((t, GLA_DK), F32)],
        compiler_params=_params("parallel"), name="gla_proj",
    )(x, g, w_main, w_gd, w_gate, b_gate)


def _gla_gate_out(o, r, gn):
    return (_rms(o, gn) * _silu(r)).astype(BF16)


def _gla_prompt_kernel(q_ref, k_ref, v_ref, r_ref, b_ref, gn_ref, o_ref, s_ref, st_ref):
    c = GLA_CHUNK
    rows = GLA_GROUP * c
    n_groups = q_ref.shape[1] // rows
    row = lax.broadcasted_iota(jnp.int32, (rows, rows), 0)
    col = lax.broadcasted_iota(jnp.int32, (rows, rows), 1)
    causal = (row >= col) & (col >= row - jnp.bitwise_and(row, c - 1))
    gn = gn_ref[...]
    st_ref[...] = jnp.zeros_like(st_ref)

    def body(g, carry):
        sl = pl.ds(pl.multiple_of(g * rows, rows), rows)
        q = q_ref[0, sl, :] * (GLA_DK_HEAD ** -0.5)
        k = k_ref[0, sl, :]
        v_bf = v_ref[0, sl, :].astype(BF16)
        b = b_ref[0, sl, :]
        per_chunk = lambda r0: jnp.concatenate(
            [jnp.broadcast_to(b[j * c + r0:j * c + r0 + 1, :], (c, GLA_DK_HEAD))
             for j in range(GLA_GROUP)], axis=0)
        b_mid = per_chunk(c // 2)
        b_end = per_chunk(c - 1)
        qa = (q * jnp.exp(b - b_mid)).astype(BF16)
        ka = (k * jnp.exp(b_mid - b)).astype(BF16)
        a = lax.dot_general(qa, ka, NT_DIMS, preferred_element_type=F32)
        a = jnp.where(causal, a, 0.0).astype(BF16)
        o_intra = jnp.dot(a, v_bf, preferred_element_type=F32)
        qb = (q * jnp.exp(b)).astype(BF16)
        k_end = (k * jnp.exp(b_end - b)).astype(BF16)
        st = st_ref[...]
        o_inter = []
        for j in range(GLA_GROUP):
            rs = slice(j * c, (j + 1) * c)
            o_inter.append(lax.dot_general(qb[rs], st.astype(BF16), NT_DIMS,
                                           preferred_element_type=F32))
            upd = lax.dot_general(v_bf[rs], k_end[rs], TN_DIMS, preferred_element_type=F32)
            st = st * jnp.exp(b[(j + 1) * c - 1:(j + 1) * c, :]) + upd
        st_ref[...] = st
        o = o_intra + jnp.concatenate(o_inter, axis=0)
        o_ref[0, sl, :] = _gla_gate_out(o, r_ref[0, sl, :], gn)
        return carry

    lax.fori_loop(0, n_groups, body, 0, unroll=4)
    s_ref[0, 0] = st_ref[...].T


def _gla_prompt(proj, decay, g_norm, layer):
    b, l, _ = proj.shape
    kcol = GLA_DK // GLA_DK_HEAD
    vcol = 2 * GLA_DK // GLA_DV_HEAD
    rcol = vcol + GLA_DV // GLA_DV_HEAD
    return pl.pallas_call(
        _gla_prompt_kernel,
        grid=(b, GLA_HEADS),
        in_specs=[pl.BlockSpec((1, l, GLA_DK_HEAD), lambda b, h: (b, 0, h)),
                  pl.BlockSpec((1, l, GLA_DK_HEAD), lambda b, h: (b, 0, kcol + h)),
                  pl.BlockSpec((1, l, GLA_DV_HEAD), lambda b, h: (b, 0, vcol + h)),
                  pl.BlockSpec((1, l, GLA_DV_HEAD), lambda b, h: (b, 0, rcol + h)),
                  pl.BlockSpec((1, l, GLA_DK_HEAD), lambda b, h: (b, 0, h)),
                  pl.BlockSpec((None, 1, GLA_DV_HEAD), lambda b, h: (layer, 0, 0))],
        out_specs=[pl.BlockSpec((1, l, GLA_DV_HEAD), lambda b, h: (b, 0, h)),
                   pl.BlockSpec((1, 1, GLA_DK_HEAD, GLA_DV_HEAD), lambda b, h: (b, h, 0, 0))],
        out_shape=[jax.ShapeDtypeStruct((b, l, GLA_DV), BF16),
                   jax.ShapeDtypeStruct((b, GLA_HEADS, GLA_DK_HEAD, GLA_DV_HEAD), F32)],
        scratch_shapes=[pltpu.VMEM((GLA_DV_HEAD, GLA_DK_HEAD), F32)],
        compiler_params=_params("parallel", "parallel"), name="gla_prompt",
    )(proj, proj, proj, proj, decay, g_norm)


def _gla_sample_kernel(layer, owner_of_all, q_ref, k_ref, v_ref, r_ref, b_ref, s0_ref, gn_ref, *rest):
    o_ref, s_ref = rest[-2:]
    s_ref = _own_slab(s_ref, owner_of_all, layer)
    bt, c, _ = q_ref.shape
    row = lax.broadcasted_iota(jnp.int32, (c, c), 0)
    col = lax.broadcasted_iota(jnp.int32, (c, c), 1)
    causal = row >= col
    gn = gn_ref[...]

    def body(i, carry):
        q = q_ref[i] * (GLA_DK_HEAD ** -0.5)
        k = k_ref[i]
        v = v_ref[i]
        r = r_ref[i]
        b = b_ref[i]
        b_mid = b[c // 2:c // 2 + 1, :]
        b_end = b[c - 1:c, :]
        qa = q * jnp.exp(b - b_mid)
        ka = k * jnp.exp(b_mid - b)
        qb = (q * jnp.exp(b)).astype(BF16)
        k_end_t = (k * jnp.exp(b_end - b)).T
        dec_col = jnp.broadcast_to(jnp.exp(b_end), (c, GLA_DK)).T[:, 0:1]
        for h in range(GLA_HEADS):
            ks = slice(h * GLA_DK_HEAD, (h + 1) * GLA_DK_HEAD)
            vs = slice(h * GLA_DV_HEAD, (h + 1) * GLA_DV_HEAD)
            s0 = s0_ref[i, h]
            a = lax.dot_general(qa[:, ks], ka[:, ks], NT_DIMS, preferred_element_type=F32)
            a = jnp.where(causal, a, 0.0)
            o = jnp.dot(a, v[:, vs], preferred_element_type=F32)
            o = o + jnp.dot(qb[:, ks], s0.astype(BF16), preferred_element_type=F32)
            upd = jnp.dot(k_end_t[ks, :], v[:, vs], preferred_element_type=F32)
            s_ref[i, h] = s0 * dec_col[ks, :] + upd
            o_ref[i, :, vs] = _gla_gate_out(o, r[:, vs], gn)
        return carry

    lax.fori_loop(0, bt, body, 0, unroll=2)


def _gla_sample(proj, decay, state, g_norm, layer, prev_state):
    b, l, _ = proj.shape
    bt = min(SAMPLE_BATCH_TILE, b)
    state_block = (bt, GLA_HEADS, GLA_DK_HEAD, GLA_DV_HEAD)
    in_specs = [pl.BlockSpec((bt, l, GLA_DK), lambda b: (b, 0, 0)),
                pl.BlockSpec((bt, l, GLA_DK), lambda b: (b, 0, 1)),
                pl.BlockSpec((bt, l, GLA_DV), lambda b: (b, 0, 1)),
                pl.BlockSpec((bt, l, GLA_DV), lambda b: (b, 0, 2)),
                pl.BlockSpec((bt, l, GLA_DK), lambda b: (b, 0, 0)),
                pl.BlockSpec((None,) + state_block, lambda b: (layer, b, 0, 0, 0)),
                pl.BlockSpec((None, 1, GLA_DV_HEAD), lambda b: (layer, 0, 0))]
    inputs = [proj, proj, proj, proj, decay, state, g_norm]
    spec, shape, extra_specs, extra_inputs = _stacked_out_specs(
        prev_state, state.shape[0], layer, state.shape[1:], state_block, lambda b: (b, 0, 0, 0))
    return pl.pallas_call(
        functools.partial(_gla_sample_kernel, layer, prev_state is None),
        grid=(b // bt,),
        in_specs=in_specs + extra_specs,
        out_specs=[pl.BlockSpec((bt, l, GLA_DV), lambda b: (b, 0, 0)), spec],
        out_shape=[jax.ShapeDtypeStruct((b, l, GLA_DV), BF16), shape],
        input_output_aliases={len(inputs): 1} if extra_inputs else {},
        compiler_params=_params("parallel"), name="gla_sample",
    )(*inputs, *extra_inputs)


def kernel(x_prompt, x_sample, cache_k, cache_v, state_gla, norm_mix, norm_ffn, norm_final,
           w_attn_in, w_attn_out, attn_sinks, w_gla_in, w_gla_gate_up, b_gla_gate, gla_out_norm,
           w_gla_out, w_ffn_in, w_ffn_out):
    bp, lp, _ = x_prompt.shape
    bs, ls, _ = x_sample.shape
    depth = norm_mix.shape[0]
    n_attn = cache_k.shape[0]
    yp = x_prompt.reshape(bp * lp, D_MODEL)
    ys = x_sample.reshape(bs * ls, D_MODEL)
    tp = min(TOKEN_TILE, bp * lp)
    ts = min(TOKEN_TILE, bs * ls)
    bias_p, bias_old, bias_new = _attn_bias_tables(ls)
    n_main = 2 * GLA_DK + 2 * GLA_DV

    g_mix = norm_mix.reshape(depth, 1, D_MODEL)
    g_ffn = norm_ffn.reshape(depth, 1, D_MODEL)
    g_fin = norm_final.reshape(1, D_MODEL)
    w_attn_bf = w_attn_in.astype(BF16)
    w_attn_in_bf = jnp.concatenate(
        [_regroup_heads(w_attn_bf[:, :, :Q_DIM], 2, HEAD_DIM), w_attn_bf[:, :, Q_DIM:]], axis=2)
    w_attn_out_bf = _regroup_heads(w_attn_out.astype(BF16), 1, HEAD_DIM)
    sinks = _regroup_heads(attn_sinks, 1, 1)
    w_gla_main_bf = w_gla_in.astype(BF16)
    w_gla_gd_bf = w_gla_in[:, :, n_main:].astype(BF16)
    w_gla_out_bf = w_gla_out.astype(BF16)
    w_fi, w_fo = w_ffn_in[0].astype(BF16), w_ffn_out[0].astype(BF16)
    b_gate = b_gla_gate.reshape(-1, 1, GLA_DK)
    g_gla = gla_out_norm.reshape(-1, 1, GLA_DV_HEAD)
    cache_kt = jnp.transpose(cache_k, (0, 1, 3, 4, 2)).reshape(n_attn, bs, KV_DIM, WINDOW)
    cache_vt = jnp.transpose(cache_v, (0, 1, 3, 4, 2)).reshape(n_attn, bs, KV_DIM, WINDOW)

    kp_l, vp_l, sp_l = [], [], []
    new_kv, new_state = None, None
    for i in range(depth):
        j = i // 2
        if i % 2 == 0:
            w_mix = w_attn_out_bf
            (qkv_p,) = _norm_matmul(yp, g_mix, [w_attn_in_bf], (i, j), tp)
            (qkv_s,) = _norm_matmul(ys, g_mix, [w_attn_in_bf], (i, j), ts)
            qkv_p = qkv_p.reshape(bp, lp, Q_DIM + 2 * KV_DIM)
            qkv_s = qkv_s.reshape(bs, ls, Q_DIM + 2 * KV_DIM)
            op = _attn_prompt(qkv_p, sinks[j] * LOG2E, bias_p)
            sink_col = jnp.repeat(sinks[j], ls).reshape(N_HEADS * ls, 1)
            os_, *new_kv = _attn_sample(qkv_s, cache_kt, cache_vt, j, bias_old, bias_new, sink_col,
                                        new_kv)
            kv_shape = (bp, WINDOW, N_KV_HEADS, HEAD_DIM)
            kp_l.append(qkv_p[:, lp - WINDOW:, Q_DIM:Q_DIM + KV_DIM].reshape(kv_shape))
            vp_l.append(qkv_p[:, lp - WINDOW:, Q_DIM + KV_DIM:].reshape(kv_shape))
        else:
            w_mix = w_gla_out_bf
            proj_p, dec_p = _gla_proj(yp, g_mix, w_gla_main_bf, w_gla_gd_bf, w_gla_gate_up, b_gate,
                                      (i, j), GLA_CHUNK, tp)
            proj_s, dec_s = _gla_proj(ys, g_mix, w_gla_main_bf, w_gla_gd_bf, w_gla_gate_up, b_gate,
                                      (i, j), ls, ts)
            op, sp = _gla_prompt(proj_p.reshape(bp, lp, n_main), dec_p.reshape(bp, lp, GLA_DK),
                                 g_gla, j)
            os_, new_state = _gla_sample(proj_s.reshape(bs, ls, n_main),
                                         dec_s.reshape(bs, ls, GLA_DK), state_gla, g_gla, j, new_state)
            sp_l.append(sp)
        last = i == depth - 1
        yp, *w_next = _mix_out_ffn(yp, op.reshape(bp * lp, D_MODEL), w_mix, g_ffn, w_fi, w_fo,
                                   g_fin, (j, i), last, min(FFN_TOKEN_TILE, bp * lp),
                                   None if last else (w_ffn_in, w_ffn_out))
        (ys,) = _mix_out_ffn(ys, os_.reshape(bs * ls, D_MODEL), w_mix, g_ffn, w_fi, w_fo,
                             g_fin, (j, i), last, min(FFN_TOKEN_TILE, bs * ls))
        if w_next:
            w_fi, w_fo = w_next

    def window_major(xt):
        return jnp.transpose(xt.reshape(n_attn, bs, N_KV_HEADS, HEAD_DIM, WINDOW), (0, 1, 4, 2, 3))

    return (yp.reshape(bp, lp, D_MODEL), ys.reshape(bs, ls, D_MODEL),
            jnp.stack(kp_l), jnp.stack(vp_l), jnp.stack(sp_l),
            window_major(new_kv[0]), window_major(new_kv[1]), new_state)
```

```python
import functools

import jax
import jax.numpy as jnp
from jax import lax
from jax.experimental import pallas as pl
from jax.experimental.pallas import tpu as pltpu

F32 = jnp.float32
BF16 = jnp.bfloat16

D_MODEL = 1024
EPS = 1e-6
WINDOW = 128
HEAD_DIM = 64
N_HEADS = 16
N_KV_HEADS = 4
GROUP = 4
Q_DIM = 1024
KV_DIM = 256
GLA_HEADS = 4
GLA_DK = 512
GLA_DV = 1024
GLA_DK_HEAD = 128
GLA_DV_HEAD = 256
GLA_GATE_RANK = 16
GLA_GATE_NORMALIZER = 16.0
GLA_CHUNK = 64
D_FF = 2816
MASKED = -1e30
LOG2E = 1.4426950408889634

VMEM_LIMIT_BYTES = 52 * 1024 * 1024
LANES = 128
BF16_SUBLANES = 16
TOKEN_TILE = 1024
FFN_TOKEN_TILE = 512
FFN_TILE = 256
ATTN_Q_BLOCKS = 8
GLA_GROUP = 4
SAMPLE_BATCH_TILE = 8
NEW_KEY_PAD = 16

NT_DIMS = (((1,), (1,)), ((), ()))
TN_DIMS = (((0,), (0,)), ((), ()))


def _params(*sem):
    return pltpu.CompilerParams(dimension_semantics=sem, vmem_limit_bytes=VMEM_LIMIT_BYTES)


def _rms(x, g):
    r = lax.rsqrt(jnp.mean(x * x, axis=-1, keepdims=True) + EPS)
    return x * r * g


def _silu(x):
    return x / (1.0 + jnp.exp(-x))


def _norm_matmul_kernel(n, x_ref, g_ref, *refs):
    h = _rms(x_ref[...], g_ref[...]).astype(BF16)
    for w_ref, o_ref in zip(refs[:n], refs[n:]):
        o_ref[...] = jnp.dot(h, w_ref[...], preferred_element_type=F32)


def _norm_matmul(x, g, ws, layers, tm):
    t = x.shape[0]
    lg, lw = layers
    in_specs = [pl.BlockSpec((tm, D_MODEL), lambda i: (i, 0)),
                pl.BlockSpec((None, 1, D_MODEL), lambda i: (lg, 0, 0))]
    in_specs += [pl.BlockSpec((None,) + w.shape[1:], lambda i: (lw, 0, 0)) for w in ws]
    out_specs = [pl.BlockSpec((tm, w.shape[2]), lambda i: (i, 0)) for w in ws]
    out_shape = [jax.ShapeDtypeStruct((t, w.shape[2]), F32) for w in ws]
    return pl.pallas_call(
        functools.partial(_norm_matmul_kernel, len(ws)),
        grid=(t // tm,), in_specs=in_specs, out_specs=out_specs, out_shape=out_shape,
        compiler_params=_params("parallel"), name="norm_matmul",
    )(x, g, *ws)


def _ffn_kernel(final_norm, convert_next, x_ref, a_ref, wm_ref, g_ref, wi_ref, wo_ref, gf_ref, *rest):
    if convert_next:
        nwi_ref, nwo_ref, o_ref, owi_ref, owo_ref, act_ref = rest
        owi_ref[...] = nwi_ref[...].astype(BF16)
        owo_ref[...] = nwo_ref[...].astype(BF16)
    else:
        o_ref, act_ref = rest
    y = x_ref[...] + jnp.dot(a_ref[...], wm_ref[...], preferred_element_type=F32)
    h = _rms(y, g_ref[...]).astype(BF16)
    for j in range(D_FF // FFN_TILE):
        gate = jnp.dot(h, wi_ref[:, j * FFN_TILE:(j + 1) * FFN_TILE], preferred_element_type=F32)
        up = jnp.dot(h, wi_ref[:, D_FF + j * FFN_TILE:D_FF + (j + 1) * FFN_TILE],
                     preferred_element_type=F32)
        act_ref[:, j * FFN_TILE:(j + 1) * FFN_TILE] = (_silu(gate) * up).astype(BF16)
    y = y + jnp.dot(act_ref[...], wo_ref[...], preferred_element_type=F32)
    if final_norm:
        y = _rms(y, gf_ref[...])
    o_ref[...] = y


def _resident(block_shape, index_map):
    return pl.BlockSpec(block_shape, index_map, pipeline_mode=pl.Buffered(1))


def _slab_rows(total, steps):
    rows = BF16_SUBLANES
    while total % rows or total // rows > steps:
        rows += BF16_SUBLANES
    return rows


def _mix_out_ffn(x, a, w_mix, g, w_in, w_out, g_final, layers, final_norm, tm, next_f32=None):
    t = x.shape[0]
    lm, lf = layers
    steps = t // tm
    row = lambda i: (i, 0)
    in_specs = [pl.BlockSpec((tm, D_MODEL), row),
                pl.BlockSpec((tm, D_MODEL), row),
                _resident((None, D_MODEL, D_MODEL), lambda i: (lm, 0, 0)),
                _resident((None, 1, D_MODEL), lambda i: (lf, 0, 0)),
                _resident((D_MODEL, 2 * D_FF), lambda i: (0, 0)),
                _resident((D_FF, D_MODEL), lambda i: (0, 0)),
                _resident((1, D_MODEL), lambda i: (0, 0))]
    inputs = [x, a, w_mix, g, w_in, w_out, g_final]
    out_specs = [pl.BlockSpec((tm, D_MODEL), row)]
    out_shape = [jax.ShapeDtypeStruct((t, D_MODEL), F32)]
    if next_f32 is not None:
        for w_all in next_f32:
            rows = _slab_rows(w_all.shape[1], steps)
            last = w_all.shape[1] // rows - 1
            slab = lambda i, last=last: (jnp.minimum(i, last), 0)
            in_specs.append(pl.BlockSpec((None, rows, w_all.shape[2]),
                                         lambda i, slab=slab: (lf + 1,) + slab(i)))
            inputs.append(w_all)
            out_specs.append(pl.BlockSpec((rows, w_all.shape[2]), slab))
            out_shape.append(jax.ShapeDtypeStruct(w_all.shape[1:], BF16))
    return pl.pallas_call(
        functools.partial(_ffn_kernel, final_norm, next_f32 is not None),
        grid=(steps,), in_specs=in_specs, out_specs=out_specs, out_shape=out_shape,
        scratch_shapes=[pltpu.VMEM((tm, D_FF), BF16)],
        compiler_params=_params("arbitrary"), name="mix_out_ffn",
    )(*inputs)


HALVES = LANES // HEAD_DIM
HEAD_ORDER = tuple((HALVES * (p // (HALVES * GROUP)) + p % HALVES) * GROUP + (p % (HALVES * GROUP)) // HALVES
                   for p in range(N_HEADS))


def _regroup_heads(x, axis, width):
    lead, tail = x.shape[:axis], x.shape[axis + 1:]
    x = x.reshape(lead + (N_KV_HEADS // HALVES, HALVES, GROUP, width) + tail)
    x = jnp.swapaxes(x, axis + 1, axis + 2)
    return x.reshape(lead + (N_HEADS * width,) + tail)


def _half_masks(rows):
    low = lax.broadcasted_iota(jnp.int32, (rows, LANES), 1) < HEAD_DIM
    return low, jnp.logical_not(low)


def _attn_prompt_kernel(sink_ref, q_ref, kp_ref, kc_ref, vp_ref, vc_ref, bias_ref, o_ref):
    step = pl.program_id(1)
    q_all = (q_ref[0] * (LOG2E * HEAD_DIM ** -0.5)).astype(BF16)
    k_all = jnp.concatenate([kp_ref[0], kc_ref[0]], axis=0).astype(BF16)
    v_f32 = jnp.concatenate([vp_ref[0], vc_ref[0]], axis=0)
    masks = _half_masks(WINDOW)
    key_masks = _half_masks(2 * WINDOW)
    zero = jnp.zeros((), BF16)
    half_ones = [jnp.where(mask, 1.0, 0.0).astype(BF16) for mask in key_masks]
    for qb in range(ATTN_Q_BLOCKS):
        rows = slice(qb * WINDOW, (qb + 1) * WINDOW)
        keys = slice(qb * WINDOW, (qb + 2) * WINDOW)
        table = 1 if qb > 0 else jnp.minimum(step, 1)
        for blk in range(KV_DIM // LANES):
            lanes = slice(blk * LANES, (blk + 1) * LANES)
            k2 = k_all[keys, lanes]
            v2 = v_f32[keys, lanes]
            v_pair = jnp.concatenate(
                [jnp.concatenate([jnp.where(key_masks[half], v2, 0.0).astype(BF16), half_ones[half]],
                                 axis=1)
                 for half in range(HALVES)], axis=0)
            for c in range(blk * GROUP, (blk + 1) * GROUP):
                qc = q_all[rows, c * LANES:(c + 1) * LANES]
                probs, sink_terms = [], []
                for half in range(HALVES):
                    p = HALVES * c + half
                    qm = jnp.where(masks[half], qc, zero)
                    s = lax.dot_general(qm, k2, NT_DIMS, preferred_element_type=F32)
                    s = s + bias_ref[table, p]
                    sink = sink_ref[p]
                    m = jnp.maximum(jnp.max(s, axis=-1, keepdims=True), sink)
                    probs.append(jnp.exp2(s - m).astype(BF16))
                    sink_terms.append(jnp.exp2(sink - m))
                ov = jnp.dot(jnp.concatenate(probs, axis=1), v_pair,
                             preferred_element_type=F32)
                denom = ov[:, LANES:] + jnp.where(
                    masks[0], *[jnp.broadcast_to(t, (WINDOW, LANES)) for t in sink_terms])
                o_ref[0, rows, c * LANES:(c + 1) * LANES] = (ov[:, :LANES] / denom).astype(BF16)


def _attn_prompt(qkv, sinks, bias):
    b, l, _ = qkv.shape
    rows = ATTN_Q_BLOCKS * WINDOW
    kcol, vcol = Q_DIM // KV_DIM, Q_DIM // KV_DIM + 1
    prev = lambda i: jnp.maximum(ATTN_Q_BLOCKS * i - 1, 0)
    return pl.pallas_call(
        _attn_prompt_kernel,
        grid=(b, l // rows),
        in_specs=[pl.BlockSpec(memory_space=pltpu.SMEM),
                  pl.BlockSpec((1, rows, Q_DIM), lambda b, i: (b, i, 0)),
                  pl.BlockSpec((1, WINDOW, KV_DIM), lambda b, i: (b, prev(i), kcol)),
                  pl.BlockSpec((1, rows, KV_DIM), lambda b, i: (b, i, kcol)),
                  pl.BlockSpec((1, WINDOW, KV_DIM), lambda b, i: (b, prev(i), vcol)),
                  pl.BlockSpec((1, rows, KV_DIM), lambda b, i: (b, i, vcol)),
                  pl.BlockSpec(bias.shape, lambda b, i: (0, 0, 0, 0))],
        out_specs=pl.BlockSpec((1, rows, Q_DIM), lambda b, i: (b, i, 0)),
        out_shape=jax.ShapeDtypeStruct((b, l, Q_DIM), BF16),
        compiler_params=_params("parallel", "parallel"), name="attn_prompt",
    )(sinks, qkv, qkv, qkv, qkv, qkv, bias)


def _own_slab(ref, owner_of_all, layer):
    if not owner_of_all:
        return ref
    for other in range(ref.shape[0]):
        if other != layer:
            ref[other] = jnp.zeros(ref.shape[1:], ref.dtype)
    return ref.at[layer]


def _stacked_out_specs(prev, n_layers, layer, slab_shape, block, index):
    shape = jax.ShapeDtypeStruct((n_layers,) + slab_shape, F32)
    if prev is None:
        return pl.BlockSpec((n_layers,) + block, lambda b: (0,) + index(b)), shape, [], []
    spec = pl.BlockSpec((None,) + block, lambda b: (layer,) + index(b))
    return spec, shape, [pl.BlockSpec(memory_space=pl.ANY)], [prev]


def _attn_sample_kernel(layer, owner_of_all, q_ref, kn_ref, vn_ref, kt_ref, vt_ref, bo_ref, bn_ref,
                        sink_ref, *rest):
    o_ref, kto_ref, vto_ref = rest[-3:]
    kto_ref = _own_slab(kto_ref, owner_of_all, layer)
    vto_ref = _own_slab(vto_ref, owner_of_all, layer)
    bt, l, _ = q_ref.shape
    masks = _half_masks(l)
    keep = lax.broadcasted_iota(jnp.int32, (KV_DIM, WINDOW), 1) < WINDOW - l
    zero_block = jnp.zeros((l, LANES), F32)
    pad_rows = jnp.zeros((WINDOW - l, KV_DIM), F32)
    pad_keys = jnp.zeros((NEW_KEY_PAD - l, KV_DIM), F32)
    ones_old = jnp.ones((LANES, WINDOW), BF16)
    ones_new = jnp.ones((NEW_KEY_PAD, LANES), BF16)
    bias_old = bo_ref[...]
    bias_new = bn_ref[...]
    sink = sink_ref[...]

    def body(b, carry):
        q = q_ref[b] * (HEAD_DIM ** -0.5)
        pieces = []
        for p in range(N_HEADS):
            c = p // HALVES
            src = jnp.where(masks[p % HALVES], q[:, c * LANES:(c + 1) * LANES], 0.0)
            pieces.append(jnp.concatenate(
                [src, zero_block] if c // GROUP == 0 else [zero_block, src], axis=1))
        qt = jnp.concatenate(pieces, axis=0).astype(BF16)

        k_old, v_old = kt_ref[b], vt_ref[b]
        k_new, v_new = kn_ref[b], vn_ref[b]
        k_pad = jnp.concatenate([k_new, pad_keys], axis=0).astype(BF16)
        v_pad = jnp.concatenate([v_new, pad_keys], axis=0).astype(BF16)
        s_old = jnp.dot(qt, k_old.astype(BF16), preferred_element_type=F32) + bias_old
        s_new = lax.dot_general(qt, k_pad, NT_DIMS, preferred_element_type=F32) + bias_new
        m = jnp.maximum(sink, jnp.maximum(jnp.max(s_old, axis=-1, keepdims=True),
                                          jnp.max(s_new, axis=-1, keepdims=True)))
        p_old = jnp.exp(s_old - m).astype(BF16)
        p_new = jnp.exp(s_new - m).astype(BF16)
        v_old_1 = jnp.concatenate([v_old.astype(BF16), ones_old], axis=0)
        v_new_1 = jnp.concatenate([v_pad, ones_new], axis=1)
        o = lax.dot_general(p_old, v_old_1, NT_DIMS, preferred_element_type=F32)
        o = o + jnp.dot(p_new, v_new_1, preferred_element_type=F32)
        inv = 1.0 / (o[:, KV_DIM:] + jnp.exp(sink - m))
        cols = []
        for c in range(N_HEADS // HALVES):
            lanes = slice((c // GROUP) * LANES, (c // GROUP + 1) * LANES)
            parts = [o[(HALVES * c + half) * l:(HALVES * c + half + 1) * l, lanes]
                     * inv[(HALVES * c + half) * l:(HALVES * c + half + 1) * l, :]
                     for half in range(HALVES)]
            cols.append(jnp.where(masks[0], parts[0], parts[1]))
        o_ref[b] = jnp.concatenate(cols, axis=1).astype(BF16)

        for old, new, out_ref in ((k_old, k_new, kto_ref), (v_old, v_new, vto_ref)):
            shifted = pltpu.roll(old, WINDOW - l, axis=1)
            placed = jnp.concatenate([pad_rows, new], axis=0).T
            out_ref[b] = jnp.where(keep, shifted, placed)
        return carry

    lax.fori_loop(0, bt, body, 0, unroll=4)


def _attn_sample(qkv, cache_kt, cache_vt, layer, bias_old, bias_new, sink_col, prev_kv):
    b, l, _ = qkv.shape
    n_layers = cache_kt.shape[0]
    bt = min(SAMPLE_BATCH_TILE, b)
    kcol, vcol = Q_DIM // KV_DIM, Q_DIM // KV_DIM + 1
    const = lambda b: (0, 0)
    in_specs = [pl.BlockSpec((bt, l, Q_DIM), lambda b: (b, 0, 0)),
                pl.BlockSpec((bt, l, KV_DIM), lambda b: (b, 0, kcol)),
                pl.BlockSpec((bt, l, KV_DIM), lambda b: (b, 0, vcol)),
                pl.BlockSpec((None, bt, KV_DIM, WINDOW), lambda b: (layer, b, 0, 0)),
                pl.BlockSpec((None, bt, KV_DIM, WINDOW), lambda b: (layer, b, 0, 0)),
                pl.BlockSpec(bias_old.shape, const),
                pl.BlockSpec(bias_new.shape, const),
                pl.BlockSpec(sink_col.shape, const)]
    inputs = [qkv, qkv, qkv, cache_kt, cache_vt, bias_old, bias_new, sink_col]
    out_specs = [pl.BlockSpec((bt, l, Q_DIM), lambda b: (b, 0, 0))]
    out_shape = [jax.ShapeDtypeStruct((b, l, Q_DIM), BF16)]
    aliases = {}
    for prev in (prev_kv if prev_kv is not None else (None, None)):
        spec, shape, extra_specs, extra_inputs = _stacked_out_specs(
            prev, n_layers, layer, (b, KV_DIM, WINDOW), (bt, KV_DIM, WINDOW), lambda b: (b, 0, 0))
        if extra_inputs:
            aliases[len(inputs)] = len(out_specs)
        in_specs += extra_specs
        inputs += extra_inputs
        out_specs.append(spec)
        out_shape.append(shape)
    return pl.pallas_call(
        functools.partial(_attn_sample_kernel, layer, prev_kv is None),
        grid=(b // bt,), in_specs=in_specs, out_specs=out_specs, out_shape=out_shape,
        input_output_aliases=aliases,
        compiler_params=_params("parallel"), name="attn_sample",
    )(*inputs)


def _attn_bias_tables(l_sample):
    slopes = jnp.exp2(-8.0 * jnp.arange(1, N_HEADS + 1, dtype=F32) / N_HEADS)
    slopes = slopes[jnp.array(HEAD_ORDER)][:, None, None]
    t = jnp.arange(WINDOW, dtype=jnp.int32)[:, None]
    s = jnp.arange(2 * WINDOW, dtype=jnp.int32)[None, :]
    dist = t + WINDOW - s
    valid = (dist >= 0) & (dist <= WINDOW)
    later = jnp.where(valid[None], -slopes * dist.astype(F32)[None], MASKED)
    first = jnp.where((s >= WINDOW)[None], later, MASKED)
    prompt = jnp.stack([first, later]) * LOG2E
    tq = jnp.arange(l_sample, dtype=jnp.int32)[:, None]
    sk = jnp.arange(WINDOW + NEW_KEY_PAD, dtype=jnp.int32)[None, :]
    dist = tq + WINDOW - sk
    valid = (dist >= 0) & (dist <= WINDOW) & (sk < WINDOW + l_sample)
    sample = jnp.where(valid[None], -slopes * dist.astype(F32)[None], MASKED)
    sample = sample.reshape(N_HEADS * l_sample, WINDOW + NEW_KEY_PAD)
    return prompt, sample[:, :WINDOW], sample[:, WINDOW:]


def _log_sigmoid(x):
    return jnp.minimum(x, 0.0) - jnp.log(1.0 + jnp.exp(-jnp.abs(x)))


def _chunk_cumsum(x, c):
    pos = jnp.bitwise_and(lax.broadcasted_iota(jnp.int32, x.shape, 0), c - 1)
    shift = 1
    while shift < c:
        x = x + jnp.where(pos >= shift, pltpu.roll(x, shift, axis=0), 0.0)
        shift *= 2
    return x


def _gla_proj_kernel(chunk, x_ref, g_ref, wm_ref, wgd_ref, wg_ref, bg_ref, o_ref, b_ref):
    h = _rms(x_ref[...], g_ref[...]).astype(BF16)
    gd = jnp.dot(h, wgd_ref[...], preferred_element_type=F32)
    pre = jnp.dot(gd.astype(BF16), wg_ref[...].astype(BF16), preferred_element_type=F32) + bg_ref[...]
    b_ref[...] = _chunk_cumsum(_log_sigmoid(pre) / GLA_GATE_NORMALIZER, chunk)
    n_qkv = 2 * GLA_DK + GLA_DV
    o_ref[:, :n_qkv] = jnp.dot(h, wm_ref[:, :n_qkv], preferred_element_type=F32)
    o_ref[:, n_qkv:] = _silu(jnp.dot(h, wm_ref[:, n_qkv:], preferred_element_type=F32))


def _gla_proj(x, g, w_main, w_gd, w_gate, b_gate, layers, chunk, tm):
    t = x.shape[0]
    lg, lw = layers
    n_main = 2 * GLA_DK + 2 * GLA_DV
    return pl.pallas_call(
        functools.partial(_gla_proj_kernel, chunk),
        grid=(t // tm,),
        in_specs=[pl.BlockSpec((tm, D_MODEL), lambda i: (i, 0)),
                  pl.BlockSpec((None, 1, D_MODEL), lambda i: (lg, 0, 0)),
                  pl.BlockSpec((None, D_MODEL, n_main), lambda i: (lw, 0, 0)),
                  pl.BlockSpec((None, D_MODEL, GLA_GATE_RANK), lambda i: (lw, 0, 0)),
                  pl.BlockSpec((None, GLA_GATE_RANK, GLA_DK), lambda i: (lw, 0, 0)),
                  pl.BlockSpec((None, 1, GLA_DK), lambda i: (lw, 0, 0))],
        out_specs=[pl.BlockSpec((tm, n_main), lambda i: (i, 0)),
                   pl.BlockSpec((tm, GLA_DK), lambda i: (i, 0))],
        out_shape=[jax.ShapeDtypeStruct((t, n_main), F32),
                   jax.ShapeDtypeStruct((t, GLA_DK), F32)],
        compiler_params=_params("parallel"), name="gla_proj",
    )(x, g, w_main, w_gd, w_gate, b_gate)


def _gla_gate_out(o, gate, gn):
    return (_rms(o, gn) * gate).astype(BF16)


def _gla_prompt_kernel(q_ref, k_ref, v_ref, r_ref, b_ref, gn_ref, o_ref, s_ref, st_ref):
    c = GLA_CHUNK
    rows = GLA_GROUP * c
    n_groups = q_ref.shape[1] // rows
    row = lax.broadcasted_iota(jnp.int32, (rows, rows), 0)
    col = lax.broadcasted_iota(jnp.int32, (rows, rows), 1)
    causal = (row >= col) & (col >= row - jnp.bitwise_and(row, c - 1))
    gn = gn_ref[...]
    st_ref[...] = jnp.zeros_like(st_ref)

    def body(g, carry):
        sl = pl.ds(pl.multiple_of(g * rows, rows), rows)
        q = q_ref[0, sl, :] * (GLA_DK_HEAD ** -0.5)
        k = k_ref[0, sl, :]
        v_bf = v_ref[0, sl, :].astype(BF16)
        b = b_ref[0, sl, :]
        per_chunk = lambda r0: jnp.concatenate(
            [jnp.broadcast_to(b[j * c + r0:j * c + r0 + 1, :], (c, GLA_DK_HEAD))
             for j in range(GLA_GROUP)], axis=0)
        b_mid = per_chunk(c // 2)
        b_end = per_chunk(c - 1)
        qa = (q * jnp.exp(b - b_mid)).astype(BF16)
        ka = (k * jnp.exp(b_mid - b)).astype(BF16)
        a = lax.dot_general(qa, ka, NT_DIMS, preferred_element_type=F32)
        a = jnp.where(causal, a, 0.0).astype(BF16)
        o_intra = jnp.dot(a, v_bf, preferred_element_type=F32)
        qb = (q * jnp.exp(b)).astype(BF16)
        k_end = (k * jnp.exp(b_end - b)).astype(BF16)
        st = st_ref[...]
        o_inter = []
        for j in range(GLA_GROUP):
            rs = slice(j * c, (j + 1) * c)
            o_inter.append(lax.dot_general(qb[rs], st.astype(BF16), NT_DIMS,
                                           preferred_element_type=F32))
            upd = lax.dot_general(v_bf[rs], k_end[rs], TN_DIMS, preferred_element_type=F32)
            st = st * jnp.exp(b[(j + 1) * c - 1:(j + 1) * c, :]) + upd
        st_ref[...] = st
        o = o_intra + jnp.concatenate(o_inter, axis=0)
        o_ref[0, sl, :] = _gla_gate_out(o, r_ref[0, sl, :], gn)
        return carry

    lax.fori_loop(0, n_groups, body, 0, unroll=8)
    s_ref[0, 0] = st_ref[...].T


def _gla_prompt(proj, decay, g_norm, layer):
    b, l, _ = proj.shape
    kcol = GLA_DK // GLA_DK_HEAD
    vcol = 2 * GLA_DK // GLA_DV_HEAD
    rcol = vcol + GLA_DV // GLA_DV_HEAD
    return pl.pallas_call(
        _gla_prompt_kernel,
        grid=(b, GLA_HEADS),
        in_specs=[pl.BlockSpec((1, l, GLA_DK_HEAD), lambda b, h: (b, 0, h)),
                  pl.BlockSpec((1, l, GLA_DK_HEAD), lambda b, h: (b, 0, kcol + h)),
                  pl.BlockSpec((1, l, GLA_DV_HEAD), lambda b, h: (b, 0, vcol + h)),
                  pl.BlockSpec((1, l, GLA_DV_HEAD), lambda b, h: (b, 0, rcol + h)),
                  pl.BlockSpec((1, l, GLA_DK_HEAD), lambda b, h: (b, 0, h)),
                  pl.BlockSpec((None, 1, GLA_DV_HEAD), lambda b, h: (layer, 0, 0))],
        out_specs=[pl.BlockSpec((1, l, GLA_DV_HEAD), lambda b, h: (b, 0, h)),
                   pl.BlockSpec((1, 1, GLA_DK_HEAD, GLA_DV_HEAD), lambda b, h: (b, h, 0, 0))],
        out_shape=[jax.ShapeDtypeStruct((b, l, GLA_DV), BF16),
                   jax.ShapeDtypeStruct((b, GLA_HEADS, GLA_DK_HEAD, GLA_DV_HEAD), F32)],
        scratch_shapes=[pltpu.VMEM((GLA_DV_HEAD, GLA_DK_HEAD), F32)],
        compiler_params=_params("parallel", "parallel"), name="gla_prompt",
    )(proj, proj, proj, proj, decay, g_norm)


def _gla_sample_kernel(layer, owner_of_all, q_ref, k_ref, v_ref, r_ref, b_ref, s0_ref, gn_ref, *rest):
    o_ref, s_ref = rest[-2:]
    s_ref = _own_slab(s_ref, owner_of_all, layer)
    bt, c, _ = q_ref.shape
    row = lax.broadcasted_iota(jnp.int32, (c, c), 0)
    col = lax.broadcasted_iota(jnp.int32, (c, c), 1)
    causal = row >= col
    gn = gn_ref[...]

    def body(i, carry):
        q = q_ref[i] * (GLA_DK_HEAD ** -0.5)
        k = k_ref[i]
        v = v_ref[i]
        r = r_ref[i]
        b = b_ref[i]
        b_mid = b[c // 2:c // 2 + 1, :]
        b_end = b[c - 1:c, :]
        qa = q * jnp.exp(b - b_mid)
        ka = k * jnp.exp(b_mid - b)
        qb = (q * jnp.exp(b)).astype(BF16)
        k_end_t = (k * jnp.exp(b_end - b)).T
        dec_col = jnp.broadcast_to(jnp.exp(b_end), (c, GLA_DK)).T[:, 0:1]
        for h in range(GLA_HEADS):
            ks = slice(h * GLA_DK_HEAD, (h + 1) * GLA_DK_HEAD)
            vs = slice(h * GLA_DV_HEAD, (h + 1) * GLA_DV_HEAD)
            s0 = s0_ref[i, h]
            a = lax.dot_general(qa[:, ks], ka[:, ks], NT_DIMS, preferred_element_type=F32)
            a = jnp.where(causal, a, 0.0)
            o = jnp.dot(a, v[:, vs], preferred_element_type=F32)
            o = o + jnp.dot(qb[:, ks], s0.astype(BF16), preferred_element_type=F32)
            upd = jnp.dot(k_end_t[ks, :], v[:, vs], preferred_element_type=F32)
            s_ref[i, h] = s0 * dec_col[ks, :] + upd
            o_ref[i, :, vs] = _gla_gate_out(o, r[:, vs], gn)
        return carry

    lax.fori_loop(0, bt, body, 0, unroll=2)


def _gla_sample(proj, decay, state, g_norm, layer, prev_state):
    b, l, _ = proj.shape
    bt = min(SAMPLE_BATCH_TILE, b)
    state_block = (bt, GLA_HEADS, GLA_DK_HEAD, GLA_DV_HEAD)
    in_specs = [pl.BlockSpec((bt, l, GLA_DK), lambda b: (b, 0, 0)),
                pl.BlockSpec((bt, l, GLA_DK), lambda b: (b, 0, 1)),
                pl.BlockSpec((bt, l, GLA_DV), lambda b: (b, 0, 1)),
                pl.BlockSpec((bt, l, GLA_DV), lambda b: (b, 0, 2)),
                pl.BlockSpec((bt, l, GLA_DK), lambda b: (b, 0, 0)),
                pl.BlockSpec((None,) + state_block, lambda b: (layer, b, 0, 0, 0)),
                pl.BlockSpec((None, 1, GLA_DV_HEAD), lambda b: (layer, 0, 0))]
    inputs = [proj, proj, proj, proj, decay, state, g_norm]
    spec, shape, extra_specs, extra_inputs = _stacked_out_specs(
        prev_state, state.shape[0], layer, state.shape[1:], state_block, lambda b: (b, 0, 0, 0))
    return pl.pallas_call(
        functools.partial(_gla_sample_kernel, layer, prev_state is None),
        grid=(b // bt,),
        in_specs=in_specs + extra_specs,
        out_specs=[pl.BlockSpec((bt, l, GLA_DV), lambda b: (b, 0, 0)), spec],
        out_shape=[jax.ShapeDtypeStruct((b, l, GLA_DV), BF16), shape],
        input_output_aliases={len(inputs): 1} if extra_inputs else {},
        compiler_params=_params("parallel"), name="gla_sample",
    )(*inputs, *extra_inputs)


def kernel(x_prompt, x_sample, cache_k, cache_v, state_gla, norm_mix, norm_ffn, norm_final,
           w_attn_in, w_attn_out, attn_sinks, w_gla_in, w_gla_gate_up, b_gla_gate, gla_out_norm,
           w_gla_out, w_ffn_in, w_ffn_out):
    bp, lp, _ = x_prompt.shape
    bs, ls, _ = x_sample.shape
    depth = norm_mix.shape[0]
    n_attn = cache_k.shape[0]
    yp = x_prompt.reshape(bp * lp, D_MODEL)
    ys = x_sample.reshape(bs * ls, D_MODEL)
    tp = min(TOKEN_TILE, bp * lp)
    ts = min(TOKEN_TILE, bs * ls)
    bias_p, bias_old, bias_new = _attn_bias_tables(ls)
    n_main = 2 * GLA_DK + 2 * GLA_DV

    g_mix = norm_mix.reshape(depth, 1, D_MODEL)
    g_ffn = norm_ffn.reshape(depth, 1, D_MODEL)
    g_fin = norm_final.reshape(1, D_MODEL)
    w_attn_bf = w_attn_in.astype(BF16)
    w_attn_in_bf = jnp.concatenate(
        [_regroup_heads(w_attn_bf[:, :, :Q_DIM], 2, HEAD_DIM), w_attn_bf[:, :, Q_DIM:]], axis=2)
    w_attn_out_bf = _regroup_heads(w_attn_out.astype(BF16), 1, HEAD_DIM)
    sinks = _regroup_heads(attn_sinks, 1, 1)
    w_gla_main_bf = w_gla_in.astype(BF16)
    w_gla_gd_bf = w_gla_in[:, :, n_main:].astype(BF16)
    w_gla_out_bf = w_gla_out.astype(BF16)
    w_fi, w_fo = w_ffn_in[0].astype(BF16), w_ffn_out[0].astype(BF16)
    b_gate = b_gla_gate.reshape(-1, 1, GLA_DK)
    g_gla = gla_out_norm.reshape(-1, 1, GLA_DV_HEAD)
    cache_kt = jnp.transpose(cache_k, (0, 1, 3, 4, 2)).reshape(n_attn, bs, KV_DIM, WINDOW)
    cache_vt = jnp.transpose(cache_v, (0, 1, 3, 4, 2)).reshape(n_attn, bs, KV_DIM, WINDOW)

    kp_l, vp_l, sp_l = [], [], []
    new_kv, new_state = None, None
    for i in range(depth):
        j = i // 2
        if i % 2 == 0:
            w_mix = w_attn_out_bf
            (qkv_p,) = _norm_matmul(yp, g_mix, [w_attn_in_bf], (i, j), tp)
            (qkv_s,) = _norm_matmul(ys, g_mix, [w_attn_in_bf], (i, j), ts)
            qkv_p = qkv_p.reshape(bp, lp, Q_DIM + 2 * KV_DIM)
            qkv_s = qkv_s.reshape(bs, ls, Q_DIM + 2 * KV_DIM)
            op = _attn_prompt(qkv_p, sinks[j] * LOG2E, bias_p)
            sink_col = jnp.repeat(sinks[j], ls).reshape(N_HEADS * ls, 1)
            os_, *new_kv = _attn_sample(qkv_s, cache_kt, cache_vt, j, bias_old, bias_new, sink_col,
                                        new_kv)
            kv_shape = (bp, WINDOW, N_KV_HEADS, HEAD_DIM)
            kp_l.append(qkv_p[:, lp - WINDOW:, Q_DIM:Q_DIM + KV_DIM].reshape(kv_shape))
            vp_l.append(qkv_p[:, lp - WINDOW:, Q_DIM + KV_DIM:].reshape(kv_shape))
        else:
            w_mix = w_gla_out_bf
            proj_p, dec_p = _gla_proj(yp, g_mix, w_gla_main_bf, w_gla_gd_bf, w_gla_gate_up, b_gate,
                                      (i, j), GLA_CHUNK, tp)
            proj_s, dec_s = _gla_proj(ys, g_mix, w_gla_main_bf, w_gla_gd_bf, w_gla_gate_up, b_gate,
                                      (i, j), ls, ts)
            op, sp = _gla_prompt(proj_p.reshape(bp, lp, n_main), dec_p.reshape(bp, lp, GLA_DK),
                                 g_gla, j)
            os_, new_state = _gla_sample(proj_s.reshape(bs, ls, n_main),
                                         dec_s.reshape(bs, ls, GLA_DK), state_gla, g_gla, j, new_state)
            sp_l.append(sp)
        last = i == depth - 1
        yp, *w_next = _mix_out_ffn(yp, op.reshape(bp * lp, D_MODEL), w_mix, g_ffn, w_fi, w_fo,
                                   g_fin, (j, i), last, min(FFN_TOKEN_TILE, bp * lp),
                                   None if last else (w_ffn_in, w_ffn_out))
        (ys,) = _mix_out_ffn(ys, os_.reshape(bs * ls, D_MODEL), w_mix, g_ffn, w_fi, w_fo,
                             g_fin, (j, i), last, min(FFN_TOKEN_TILE, bs * ls))
        if w_next:
            w_fi, w_fo = w_next

    def window_major(xt):
        return jnp.transpose(xt.reshape(n_attn, bs, N_KV_HEADS, HEAD_DIM, WINDOW), (0, 1, 4, 2, 3))

    return (yp.reshape(bp, lp, D_MODEL), ys.reshape(bs, ls, D_MODEL),
            jnp.stack(kp_l), jnp.stack(vp_l), jnp.stack(sp_l),
            window_major(new_kv[0]), window_major(new_kv[1]), new_state)
```

```python
import functools

import jax
import jax.numpy as jnp
from jax import lax
from jax.experimental import pallas as pl
from jax.experimental.pallas import tpu as pltpu

F32 = jnp.float32
BF16 = jnp.bfloat16

D_MODEL = 1024
EPS = 1e-6
WINDOW = 128
HEAD_DIM = 64
N_HEADS = 16
N_KV_HEADS = 4
GROUP = 4
Q_DIM = 1024
KV_DIM = 256
GLA_HEADS = 4
GLA_DK = 512
GLA_DV = 1024
GLA_DK_HEAD = 128
GLA_DV_HEAD = 256
GLA_GATE_RANK = 16
GLA_GATE_NORMALIZER = 16.0
GLA_CHUNK = 64
D_FF = 2816
MASKED = -1e30
LOG2E = 1.4426950408889634

VMEM_LIMIT_BYTES = 52 * 1024 * 1024
LANES = 128
BF16_SUBLANES = 16
TOKEN_TILE = 1024
FFN_TOKEN_TILE = 512
FFN_TILE = 256
ATTN_Q_BLOCKS = 8
GLA_GROUP = 4
SAMPLE_BATCH_TILE = 8
NEW_KEY_PAD = 16

NT_DIMS = (((1,), (1,)), ((), ()))
TN_DIMS = (((0,), (0,)), ((), ()))


def _params(*sem):
    return pltpu.CompilerParams(dimension_semantics=sem, vmem_limit_bytes=VMEM_LIMIT_BYTES)


def _rms(x, g):
    r = lax.rsqrt(jnp.mean(x * x, axis=-1, keepdims=True) + EPS)
    return x * r * g


def _silu(x):
    return x / (1.0 + jnp.exp(-x))


def _norm_matmul_kernel(n, x_ref, g_ref, *refs):
    h = _rms(x_ref[...], g_ref[...]).astype(BF16)
    for w_ref, o_ref in zip(refs[:n], refs[n:]):
        o_ref[...] = jnp.dot(h, w_ref[...], preferred_element_type=F32)


def _norm_matmul(x, g, ws, layers, tm):
    t = x.shape[0]
    lg, lw = layers
    in_specs = [pl.BlockSpec((tm, D_MODEL), lambda i: (i, 0)),
                pl.BlockSpec((None, 1, D_MODEL), lambda i: (lg, 0, 0))]
    in_specs += [pl.BlockSpec((None,) + w.shape[1:], lambda i: (lw, 0, 0)) for w in ws]
    out_specs = [pl.BlockSpec((tm, w.shape[2]), lambda i: (i, 0)) for w in ws]
    out_shape = [jax.ShapeDtypeStruct((t, w.shape[2]), F32) for w in ws]
    return pl.pallas_call(
        functools.partial(_norm_matmul_kernel, len(ws)),
        grid=(t // tm,), in_specs=in_specs, out_specs=out_specs, out_shape=out_shape,
        compiler_params=_params("parallel"), name="norm_matmul",
    )(x, g, *ws)


def _ffn_kernel(final_norm, convert_next, x_ref, a_ref, wm_ref, g_ref, wi_ref, wo_ref, gf_ref, *rest):
    if convert_next:
        nwi_ref, nwo_ref, o_ref, owi_ref, owo_ref, act_ref = rest
        owi_ref[...] = nwi_ref[...].astype(BF16)
        owo_ref[...] = nwo_ref[...].astype(BF16)
    else:
        o_ref, act_ref = rest
    y = x_ref[...] + jnp.dot(a_ref[...], wm_ref[...], preferred_element_type=F32)
    h = _rms(y, g_ref[...]).astype(BF16)
    for j in range(D_FF // FFN_TILE):
        gate = jnp.dot(h, wi_ref[:, j * FFN_TILE:(j + 1) * FFN_TILE], preferred_element_type=F32)
        up = jnp.dot(h, wi_ref[:, D_FF + j * FFN_TILE:D_FF + (j + 1) * FFN_TILE],
                     preferred_element_type=F32)
        act_ref[:, j * FFN_TILE:(j + 1) * FFN_TILE] = (_silu(gate) * up).astype(BF16)
    y = y + jnp.dot(act_ref[...], wo_ref[...], preferred_element_type=F32)
    if final_norm:
        y = _rms(y, gf_ref[...])
    o_ref[...] = y


def _resident(block_shape, index_map):
    return pl.BlockSpec(block_shape, index_map, pipeline_mode=pl.Buffered(1))


def _slab_rows(total, steps):
    rows = BF16_SUBLANES
    while total % rows or total // rows > steps:
        rows += BF16_SUBLANES
    return rows


def _mix_out_ffn(x, a, w_mix, g, w_in, w_out, g_final, layers, final_norm, tm, next_f32=None):
    t = x.shape[0]
    lm, lf = layers
    steps = t // tm
    row = lambda i: (i, 0)
    in_specs = [pl.BlockSpec((tm, D_MODEL), row),
                pl.BlockSpec((tm, D_MODEL), row),
                _resident((None, D_MODEL, D_MODEL), lambda i: (lm, 0, 0)),
                _resident((None, 1, D_MODEL), lambda i: (lf, 0, 0)),
                _resident((D_MODEL, 2 * D_FF), lambda i: (0, 0)),
                _resident((D_FF, D_MODEL), lambda i: (0, 0)),
                _resident((1, D_MODEL), lambda i: (0, 0))]
    inputs = [x, a, w_mix, g, w_in, w_out, g_final]
    out_specs = [pl.BlockSpec((tm, D_MODEL), row)]
    out_shape = [jax.ShapeDtypeStruct((t, D_MODEL), F32)]
    if next_f32 is not None:
        for w_all in next_f32:
            rows = _slab_rows(w_all.shape[1], steps)
            last = w_all.shape[1] // rows - 1
            slab = lambda i, last=last: (jnp.minimum(i, last), 0)
            in_specs.append(pl.BlockSpec((None, rows, w_all.shape[2]),
                                         lambda i, slab=slab: (lf + 1,) + slab(i)))
            inputs.append(w_all)
            out_specs.append(pl.BlockSpec((rows, w_all.shape[2]), slab))
            out_shape.append(jax.ShapeDtypeStruct(w_all.shape[1:], BF16))
    return pl.pallas_call(
        functools.partial(_ffn_kernel, final_norm, next_f32 is not None),
        grid=(steps,), in_specs=in_specs, out_specs=out_specs, out_shape=out_shape,
        scratch_shapes=[pltpu.VMEM((tm, D_FF), BF16)],
        compiler_params=_params("arbitrary"), name="mix_out_ffn",
    )(*inputs)


HALVES = LANES // HEAD_DIM
HEAD_ORDER = tuple((HALVES * (p // (HALVES * GROUP)) + p % HALVES) * GROUP + (p % (HALVES * GROUP)) // HALVES
                   for p in range(N_HEADS))


def _regroup_heads(x, axis, width):
    lead, tail = x.shape[:axis], x.shape[axis + 1:]
    x = x.reshape(lead + (N_KV_HEADS // HALVES, HALVES, GROUP, width) + tail)
    x = jnp.swapaxes(x, axis + 1, axis + 2)
    return x.reshape(lead + (N_HEADS * width,) + tail)


def _half_masks(rows):
    low = lax.broadcasted_iota(jnp.int32, (rows, LANES), 1) < HEAD_DIM
    return low, jnp.logical_not(low)


def _attn_prompt_kernel(sink_ref, q_ref, kp_ref, kc_ref, vp_ref, vc_ref, bias_ref, o_ref):
    step = pl.program_id(1)
    q_all = (q_ref[0] * (LOG2E * HEAD_DIM ** -0.5)).astype(BF16)
    k_all = jnp.concatenate([kp_ref[0], kc_ref[0]], axis=0).astype(BF16)
    v_f32 = jnp.concatenate([vp_ref[0], vc_ref[0]], axis=0)
    masks = _half_masks(WINDOW)
    key_masks = _half_masks(2 * WINDOW)
    zero = jnp.zeros((), BF16)
    half_ones = [jnp.where(mask, 1.0, 0.0).astype(BF16) for mask in key_masks]
    for qb in range(ATTN_Q_BLOCKS):
        rows = slice(qb * WINDOW, (qb + 1) * WINDOW)
        keys = slice(qb * WINDOW, (qb + 2) * WINDOW)
        table = 1 if qb > 0 else jnp.minimum(step, 1)
        for blk in range(KV_DIM // LANES):
            lanes = slice(blk * LANES, (blk + 1) * LANES)
            k2 = k_all[keys, lanes]
            v2 = v_f32[keys, lanes]
            v_pair = jnp.concatenate(
                [jnp.concatenate([jnp.where(key_masks[half], v2, 0.0).astype(BF16), half_ones[half]],
                                 axis=1)
                 for half in range(HALVES)], axis=0)
            for c in range(blk * GROUP, (blk + 1) * GROUP):
                qc = q_all[rows, c * LANES:(c + 1) * LANES]
                probs, sink_terms = [], []
                for half in range(HALVES):
                    p = HALVES * c + half
                    qm = jnp.where(masks[half], qc, zero)
                    s = lax.dot_general(qm, k2, NT_DIMS, preferred_element_type=F32)
                    s = s + bias_ref[table, p]
                    sink = sink_ref[p]
                    m = jnp.maximum(jnp.max(s, axis=-1, keepdims=True), sink)
                    probs.append(jnp.exp2(s - m).astype(BF16))
                    sink_terms.append(jnp.exp2(sink - m))
                ov = jnp.dot(jnp.concatenate(probs, axis=1), v_pair,
                             preferred_element_type=F32)
                denom = ov[:, LANES:] + jnp.where(
                    masks[0], *[jnp.broadcast_to(t, (WINDOW, LANES)) for t in sink_terms])
                o_ref[0, rows, c * LANES:(c + 1) * LANES] = (ov[:, :LANES] / denom).astype(BF16)


def _attn_prompt(qkv, sinks, bias):
    b, l, _ = qkv.shape
    rows = ATTN_Q_BLOCKS * WINDOW
    kcol, vcol = Q_DIM // KV_DIM, Q_DIM // KV_DIM + 1
    prev = lambda i: jnp.maximum(ATTN_Q_BLOCKS * i - 1, 0)
    return pl.pallas_call(
        _attn_prompt_kernel,
        grid=(b, l // rows),
        in_specs=[pl.BlockSpec(memory_space=pltpu.SMEM),
                  pl.BlockSpec((1, rows, Q_DIM), lambda b, i: (b, i, 0)),
                  pl.BlockSpec((1, WINDOW, KV_DIM), lambda b, i: (b, prev(i), kcol)),
                  pl.BlockSpec((1, rows, KV_DIM), lambda b, i: (b, i, kcol)),
                  pl.BlockSpec((1, WINDOW, KV_DIM), lambda b, i: (b, prev(i), vcol)),
                  pl.BlockSpec((1, rows, KV_DIM), lambda b, i: (b, i, vcol)),
                  pl.BlockSpec(bias.shape, lambda b, i: (0, 0, 0, 0))],
        out_specs=pl.BlockSpec((1, rows, Q_DIM), lambda b, i: (b, i, 0)),
        out_shape=jax.ShapeDtypeStruct((b, l, Q_DIM), BF16),
        compiler_params=_params("parallel", "parallel"), name="attn_prompt",
    )(sinks, qkv, qkv, qkv, qkv, qkv, bias)


def _own_slab(ref, owner_of_all, layer):
    if not owner_of_all:
        return ref
    for other in range(ref.shape[0]):
        if other != layer:
            ref[other] = jnp.zeros(ref.shape[1:], ref.dtype)
    return ref.at[layer]


def _stacked_out_specs(prev, n_layers, layer, slab_shape, block, index):
    shape = jax.ShapeDtypeStruct((n_layers,) + slab_shape, F32)
    if prev is None:
        return pl.BlockSpec((n_layers,) + block, lambda b: (0,) + index(b)), shape, [], []
    spec = pl.BlockSpec((None,) + block, lambda b: (layer,) + index(b))
    return spec, shape, [pl.BlockSpec(memory_space=pl.ANY)], [prev]


def _attn_sample_kernel(layer, owner_of_all, q_ref, kn_ref, vn_ref, kt_ref, vt_ref, bo_ref, bn_ref,
                        sink_ref, *rest):
    o_ref, kto_ref, vto_ref = rest[-3:]
    kto_ref = _own_slab(kto_ref, owner_of_all, layer)
    vto_ref = _own_slab(vto_ref, owner_of_all, layer)
    bt, l, _ = q_ref.shape
    masks = _half_masks(l)
    keep = lax.broadcasted_iota(jnp.int32, (KV_DIM, WINDOW), 1) < WINDOW - l
    zero_block = jnp.zeros((l, LANES), F32)
    pad_rows = jnp.zeros((WINDOW - l, KV_DIM), F32)
    pad_keys = jnp.zeros((NEW_KEY_PAD - l, KV_DIM), F32)
    ones_old = jnp.ones((LANES, WINDOW), BF16)
    ones_new = jnp.ones((NEW_KEY_PAD, LANES), BF16)
    bias_old = bo_ref[...]
    bias_new = bn_ref[...]
    sink = sink_ref[...]

    def body(b, carry):
        q = q_ref[b] * (HEAD_DIM ** -0.5)
        pieces = []
        for p in range(N_HEADS):
            c = p // HALVES
            src = jnp.where(masks[p % HALVES], q[:, c * LANES:(c + 1) * LANES], 0.0)
            pieces.append(jnp.concatenate(
                [src, zero_block] if c // GROUP == 0 else [zero_block, src], axis=1))
        qt = jnp.concatenate(pieces, axis=0).astype(BF16)

        k_old, v_old = kt_ref[b], vt_ref[b]
        k_new, v_new = kn_ref[b], vn_ref[b]
        k_pad = jnp.concatenate([k_new, pad_keys], axis=0).astype(BF16)
        v_pad = jnp.concatenate([v_new, pad_keys], axis=0).astype(BF16)
        s_old = jnp.dot(qt, k_old.astype(BF16), preferred_element_type=F32) + bias_old
        s_new = lax.dot_general(qt, k_pad, NT_DIMS, preferred_element_type=F32) + bias_new
        m = jnp.maximum(sink, jnp.maximum(jnp.max(s_old, axis=-1, keepdims=True),
                                          jnp.max(s_new, axis=-1, keepdims=True)))
        p_old = jnp.exp(s_old - m).astype(BF16)
        p_new = jnp.exp(s_new - m).astype(BF16)
        v_old_1 = jnp.concatenate([v_old.astype(BF16), ones_old], axis=0)
        v_new_1 = jnp.concatenate([v_pad, ones_new], axis=1)
        o = lax.dot_general(p_old, v_old_1, NT_DIMS, preferred_element_type=F32)
        o = o + jnp.dot(p_new, v_new_1, preferred_element_type=F32)
        inv = 1.0 / (o[:, KV_DIM:] + jnp.exp(sink - m))
        cols = []
        for c in range(N_HEADS // HALVES):
            lanes = slice((c // GROUP) * LANES, (c // GROUP + 1) * LANES)
            parts = [o[(HALVES * c + half) * l:(HALVES * c + half + 1) * l, lanes]
                     * inv[(HALVES * c + half) * l:(HALVES * c + half + 1) * l, :]
                     for half in range(HALVES)]
            cols.append(jnp.where(masks[0], parts[0], parts[1]))
        o_ref[b] = jnp.concatenate(cols, axis=1).astype(BF16)

        for old, new, out_ref in ((k_old, k_new, kto_ref), (v_old, v_new, vto_ref)):
            shifted = pltpu.roll(old, WINDOW - l, axis=1)
            placed = jnp.concatenate([pad_rows, new], axis=0).T
            out_ref[b] = jnp.where(keep, shifted, placed)
        return carry

    lax.fori_loop(0, bt, body, 0, unroll=4)


def _attn_sample(qkv, cache_kt, cache_vt, layer, bias_old, bias_new, sink_col, prev_kv):
    b, l, _ = qkv.shape
    n_layers = cache_kt.shape[0]
    bt = min(SAMPLE_BATCH_TILE, b)
    kcol, vcol = Q_DIM // KV_DIM, Q_DIM // KV_DIM + 1
    const = lambda b: (0, 0)
    in_specs = [pl.BlockSpec((bt, l, Q_DIM), lambda b: (b, 0, 0)),
                pl.BlockSpec((bt, l, KV_DIM), lambda b: (b, 0, kcol)),
                pl.BlockSpec((bt, l, KV_DIM), lambda b: (b, 0, vcol)),
                pl.BlockSpec((None, bt, KV_DIM, WINDOW), lambda b: (layer, b, 0, 0)),
                pl.BlockSpec((None, bt, KV_DIM, WINDOW), lambda b: (layer, b, 0, 0)),
                pl.BlockSpec(bias_old.shape, const),
                pl.BlockSpec(bias_new.shape, const),
                pl.BlockSpec(sink_col.shape, const)]
    inputs = [qkv, qkv, qkv, cache_kt, cache_vt, bias_old, bias_new, sink_col]
    out_specs = [pl.BlockSpec((bt, l, Q_DIM), lambda b: (b, 0, 0))]
    out_shape = [jax.ShapeDtypeStruct((b, l, Q_DIM), BF16)]
    aliases = {}
    for prev in (prev_kv if prev_kv is not None else (None, None)):
        spec, shape, extra_specs, extra_inputs = _stacked_out_specs(
            prev, n_layers, layer, (b, KV_DIM, WINDOW), (bt, KV_DIM, WINDOW), lambda b: (b, 0, 0))
        if extra_inputs:
            aliases[len(inputs)] = len(out_specs)
        in_specs += extra_specs
        inputs += extra_inputs
        out_specs.append(spec)
        out_shape.append(shape)
    return pl.pallas_call(
        functools.partial(_attn_sample_kernel, layer, prev_kv is None),
        grid=(b // bt,), in_specs=in_specs, out_specs=out_specs, out_shape=out_shape,
        input_output_aliases=aliases,
        compiler_params=_params("parallel"), name="attn_sample",
    )(*inputs)


def _attn_bias_tables(l_sample):
    slopes = jnp.exp2(-8.0 * jnp.arange(1, N_HEADS + 1, dtype=F32) / N_HEADS)
    slopes = slopes[jnp.array(HEAD_ORDER)][:, None, None]
    t = jnp.arange(WINDOW, dtype=jnp.int32)[:, None]
    s = jnp.arange(2 * WINDOW, dtype=jnp.int32)[None, :]
    dist = t + WINDOW - s
    valid = (dist >= 0) & (dist <= WINDOW)
    later = jnp.where(valid[None], -slopes * dist.astype(F32)[None], MASKED)
    first = jnp.where((s >= WINDOW)[None], later, MASKED)
    prompt = jnp.stack([first, later]) * LOG2E
    tq = jnp.arange(l_sample, dtype=jnp.int32)[:, None]
    sk = jnp.arange(WINDOW + NEW_KEY_PAD, dtype=jnp.int32)[None, :]
    dist = tq + WINDOW - sk
    valid = (dist >= 0) & (dist <= WINDOW) & (sk < WINDOW + l_sample)
    sample = jnp.where(valid[None], -slopes * dist.astype(F32)[None], MASKED)
    sample = sample.reshape(N_HEADS * l_sample, WINDOW + NEW_KEY_PAD)
    return prompt, sample[:, :WINDOW], sample[:, WINDOW:]


def _log_sigmoid(x):
    return jnp.minimum(x, 0.0) - jnp.log(1.0 + jnp.exp(-jnp.abs(x)))


def _chunk_cumsum(x, c):
    pos = jnp.bitwise_and(lax.broadcasted_iota(jnp.int32, x.shape, 0), c - 1)
    shift = 1
    while shift < c:
        x = x + jnp.where(pos >= shift, pltpu.roll(x, shift, axis=0), 0.0)
        shift *= 2
    return x


def _gla_proj_kernel(chunk, x_ref, g_ref, wm_ref, wgd_ref, wg_ref, bg_ref, qk_ref, vg_ref, b_ref):
    h = _rms(x_ref[...], g_ref[...]).astype(BF16)
    gd = jnp.dot(h, wgd_ref[...], preferred_element_type=F32)
    pre = jnp.dot(gd.astype(BF16), wg_ref[...].astype(BF16), preferred_element_type=F32) + bg_ref[...]
    b_ref[...] = _chunk_cumsum(_log_sigmoid(pre) / GLA_GATE_NORMALIZER, chunk)
    n_qk, n_qkv = 2 * GLA_DK, 2 * GLA_DK + GLA_DV
    qk_ref[...] = jnp.dot(h, wm_ref[:, :n_qk], preferred_element_type=F32)
    vg_ref[:, :GLA_DV] = jnp.dot(h, wm_ref[:, n_qk:n_qkv], preferred_element_type=F32).astype(BF16)
    vg_ref[:, GLA_DV:] = _silu(
        jnp.dot(h, wm_ref[:, n_qkv:], preferred_element_type=F32)).astype(BF16)


def _gla_proj(x, g, w_main, w_gd, w_gate, b_gate, layers, chunk, tm):
    t = x.shape[0]
    lg, lw = layers
    n_main = 2 * GLA_DK + 2 * GLA_DV
    return pl.pallas_call(
        functools.partial(_gla_proj_kernel, chunk),
        grid=(t // tm,),
        in_specs=[pl.BlockSpec((tm, D_MODEL), lambda i: (i, 0)),
                  pl.BlockSpec((None, 1, D_MODEL), lambda i: (lg, 0, 0)),
                  pl.BlockSpec((None, D_MODEL, n_main), lambda i: (lw, 0, 0)),
                  pl.BlockSpec((None, D_MODEL, GLA_GATE_RANK), lambda i: (lw, 0, 0)),
                  pl.BlockSpec((None, GLA_GATE_RANK, GLA_DK), lambda i: (lw, 0, 0)),
                  pl.BlockSpec((None, 1, GLA_DK), lambda i: (lw, 0, 0))],
        out_specs=[pl.BlockSpec((tm, 2 * GLA_DK), lambda i: (i, 0)),
                   pl.BlockSpec((tm, 2 * GLA_DV), lambda i: (i, 0)),
                   pl.BlockSpec((tm, GLA_DK), lambda i: (i, 0))],
        out_shape=[jax.ShapeDtypeStruct((t, 2 * GLA_DK), F32),
                   jax.ShapeDtypeStruct((t, 2 * GLA_DV), BF16),
                   jax.ShapeDtypeStruct((t, GLA_DK), F32)],
        compiler_params=_params("parallel"), name="gla_proj",
    )(x, g, w_main, w_gd, w_gate, b_gate)


def _gla_gate_out(o, gate, gn):
    return (_rms(o, gn) * gate).astype(BF16)


def _gla_prompt_kernel(q_ref, k_ref, v_ref, r_ref, b_ref, gn_ref, o_ref, s_ref, st_ref):
    c = GLA_CHUNK
    rows = GLA_GROUP * c
    n_groups = q_ref.shape[1] // rows
    row = lax.broadcasted_iota(jnp.int32, (rows, rows), 0)
    col = lax.broadcasted_iota(jnp.int32, (rows, rows), 1)
    causal = (row >= col) & (col >= row - jnp.bitwise_and(row, c - 1))
    gn = gn_ref[...]
    st_ref[...] = jnp.zeros_like(st_ref)

    def body(g, carry):
        sl = pl.ds(pl.multiple_of(g * rows, rows), rows)
        q = q_ref[0, sl, :] * (GLA_DK_HEAD ** -0.5)
        k = k_ref[0, sl, :]
        v_bf = v_ref[0, sl, :]
        b = b_ref[0, sl, :]
        per_chunk = lambda r0: jnp.concatenate(
            [jnp.broadcast_to(b[j * c + r0:j * c + r0 + 1, :], (c, GLA_DK_HEAD))
             for j in range(GLA_GROUP)], axis=0)
        b_mid = per_chunk(c // 2)
        b_end = per_chunk(c - 1)
        qa = (q * jnp.exp(b - b_mid)).astype(BF16)
        ka = (k * jnp.exp(b_mid - b)).astype(BF16)
        a = lax.dot_general(qa, ka, NT_DIMS, preferred_element_type=F32)
        a = jnp.where(causal, a, 0.0).astype(BF16)
        o_intra = jnp.dot(a, v_bf, preferred_element_type=F32)
        qb = (q * jnp.exp(b)).astype(BF16)
        k_end = (k * jnp.exp(b_end - b)).astype(BF16)
        st = st_ref[...]
        o_inter = []
        for j in range(GLA_GROUP):
            rs = slice(j * c, (j + 1) * c)
            o_inter.append(lax.dot_general(qb[rs], st.astype(BF16), NT_DIMS,
                                           preferred_element_type=F32))
            upd = lax.dot_general(v_bf[rs], k_end[rs], TN_DIMS, preferred_element_type=F32)
            st = st * jnp.exp(b[(j + 1) * c - 1:(j + 1) * c, :]) + upd
        st_ref[...] = st
        o = o_intra + jnp.concatenate(o_inter, axis=0)
        o_ref[0, sl, :] = _gla_gate_out(o, r_ref[0, sl, :], gn)
        return carry

    lax.fori_loop(0, n_groups, body, 0, unroll=8)
    s_ref[0, 0] = st_ref[...].T


def _gla_prompt(qk, vg, decay, g_norm, layer):
    b, l, _ = qk.shape
    kcol = GLA_DK // GLA_DK_HEAD
    gcol = GLA_DV // GLA_DV_HEAD
    return pl.pallas_call(
        _gla_prompt_kernel,
        grid=(b, GLA_HEADS),
        in_specs=[pl.BlockSpec((1, l, GLA_DK_HEAD), lambda b, h: (b, 0, h)),
                  pl.BlockSpec((1, l, GLA_DK_HEAD), lambda b, h: (b, 0, kcol + h)),
                  pl.BlockSpec((1, l, GLA_DV_HEAD), lambda b, h: (b, 0, h)),
                  pl.BlockSpec((1, l, GLA_DV_HEAD), lambda b, h: (b, 0, gcol + h)),
                  pl.BlockSpec((1, l, GLA_DK_HEAD), lambda b, h: (b, 0, h)),
                  pl.BlockSpec((None, 1, GLA_DV_HEAD), lambda b, h: (layer, 0, 0))],
        out_specs=[pl.BlockSpec((1, l, GLA_DV_HEAD), lambda b, h: (b, 0, h)),
                   pl.BlockSpec((1, 1, GLA_DK_HEAD, GLA_DV_HEAD), lambda b, h: (b, h, 0, 0))],
        out_shape=[jax.ShapeDtypeStruct((b, l, GLA_DV), BF16),
                   jax.ShapeDtypeStruct((b, GLA_HEADS, GLA_DK_HEAD, GLA_DV_HEAD), F32)],
        scratch_shapes=[pltpu.VMEM((GLA_DV_HEAD, GLA_DK_HEAD), F32)],
        compiler_params=_params("parallel", "parallel"), name="gla_prompt",
    )(qk, qk, vg, vg, decay, g_norm)


def _gla_sample_kernel(layer, owner_of_all, q_ref, k_ref, v_ref, r_ref, b_ref, s0_ref, gn_ref, *rest):
    o_ref, s_ref = rest[-2:]
    s_ref = _own_slab(s_ref, owner_of_all, layer)
    bt, c, _ = q_ref.shape
    row = lax.broadcasted_iota(jnp.int32, (c, c), 0)
    col = lax.broadcasted_iota(jnp.int32, (c, c), 1)
    causal = row >= col
    gn = gn_ref[...]

    def body(i, carry):
        q = q_ref[i] * (GLA_DK_HEAD ** -0.5)
        k = k_ref[i]
        v = v_ref[i].astype(F32)
        r = r_ref[i]
        b = b_ref[i]
        b_mid = b[c // 2:c // 2 + 1, :]
        b_end = b[c - 1:c, :]
        qa = q * jnp.exp(b - b_mid)
        ka = k * jnp.exp(b_mid - b)
        qb = (q * jnp.exp(b)).astype(BF16)
        k_end_t = (k * jnp.exp(b_end - b)).T
        dec_col = jnp.broadcast_to(jnp.exp(b_end), (c, GLA_DK)).T[:, 0:1]
        for h in range(GLA_HEADS):
            ks = slice(h * GLA_DK_HEAD, (h + 1) * GLA_DK_HEAD)
            vs = slice(h * GLA_DV_HEAD, (h + 1) * GLA_DV_HEAD)
            s0 = s0_ref[i, h]
            a = lax.dot_general(qa[:, ks], ka[:, ks], NT_DIMS, preferred_element_type=F32)
            a = jnp.where(causal, a, 0.0)
            o = jnp.dot(a, v[:, vs], preferred_element_type=F32)
            o = o + jnp.dot(qb[:, ks], s0.astype(BF16), preferred_element_type=F32)
            upd = jnp.dot(k_end_t[ks, :], v[:, vs], preferred_element_type=F32)
            s_ref[i, h] = s0 * dec_col[ks, :] + upd
            o_ref[i, :, vs] = _gla_gate_out(o, r[:, vs], gn)
        return carry

    lax.fori_loop(0, bt, body, 0, unroll=2)


def _gla_sample(qk, vg, decay, state, g_norm, layer, prev_state):
    b, l, _ = qk.shape
    bt = min(SAMPLE_BATCH_TILE, b)
    state_block = (bt, GLA_HEADS, GLA_DK_HEAD, GLA_DV_HEAD)
    in_specs = [pl.BlockSpec((bt, l, GLA_DK), lambda b: (b, 0, 0)),
                pl.BlockSpec((bt, l, GLA_DK), lambda b: (b, 0, 1)),
                pl.BlockSpec((bt, l, GLA_DV), lambda b: (b, 0, 0)),
                pl.BlockSpec((bt, l, GLA_DV), lambda b: (b, 0, 1)),
                pl.BlockSpec((bt, l, GLA_DK), lambda b: (b, 0, 0)),
                pl.BlockSpec((None,) + state_block, lambda b: (layer, b, 0, 0, 0)),
                pl.BlockSpec((None, 1, GLA_DV_HEAD), lambda b: (layer, 0, 0))]
    inputs = [qk, qk, vg, vg, decay, state, g_norm]
    spec, shape, extra_specs, extra_inputs = _stacked_out_specs(
        prev_state, state.shape[0], layer, state.shape[1:], state_block, lambda b: (b, 0, 0, 0))
    return pl.pallas_call(
        functools.partial(_gla_sample_kernel, layer, prev_state is None),
        grid=(b // bt,),
        in_specs=in_specs + extra_specs,
        out_specs=[pl.BlockSpec((bt, l, GLA_DV), lambda b: (b, 0, 0)), spec],
        out_shape=[jax.ShapeDtypeStruct((b, l, GLA_DV), BF16), shape],
        input_output_aliases={len(inputs): 1} if extra_inputs else {},
        compiler_params=_params("parallel"), name="gla_sample",
    )(*inputs, *extra_inputs)


def kernel(x_prompt, x_sample, cache_k, cache_v, state_gla, norm_mix, norm_ffn, norm_final,
           w_attn_in, w_attn_out, attn_sinks, w_gla_in, w_gla_gate_up, b_gla_gate, gla_out_norm,
           w_gla_out, w_ffn_in, w_ffn_out):
    bp, lp, _ = x_prompt.shape
    bs, ls, _ = x_sample.shape
    depth = norm_mix.shape[0]
    n_attn = cache_k.shape[0]
    yp = x_prompt.reshape(bp * lp, D_MODEL)
    ys = x_sample.reshape(bs * ls, D_MODEL)
    tp = min(TOKEN_TILE, bp * lp)
    ts = min(TOKEN_TILE, bs * ls)
    bias_p, bias_old, bias_new = _attn_bias_tables(ls)
    n_main = 2 * GLA_DK + 2 * GLA_DV

    g_mix = norm_mix.reshape(depth, 1, D_MODEL)
    g_ffn = norm_ffn.reshape(depth, 1, D_MODEL)
    g_fin = norm_final.reshape(1, D_MODEL)
    w_attn_bf = w_attn_in.astype(BF16)
    w_attn_in_bf = jnp.concatenate(
        [_regroup_heads(w_attn_bf[:, :, :Q_DIM], 2, HEAD_DIM), w_attn_bf[:, :, Q_DIM:]], axis=2)
    w_attn_out_bf = _regroup_heads(w_attn_out.astype(BF16), 1, HEAD_DIM)
    sinks = _regroup_heads(attn_sinks, 1, 1)
    w_gla_main_bf = w_gla_in.astype(BF16)
    w_gla_gd_bf = w_gla_in[:, :, n_main:].astype(BF16)
    w_gla_out_bf = w_gla_out.astype(BF16)
    w_fi, w_fo = w_ffn_in[0].astype(BF16), w_ffn_out[0].astype(BF16)
    b_gate = b_gla_gate.reshape(-1, 1, GLA_DK)
    g_gla = gla_out_norm.reshape(-1, 1, GLA_DV_HEAD)
    cache_kt = jnp.transpose(cache_k, (0, 1, 3, 4, 2)).reshape(n_attn, bs, KV_DIM, WINDOW)
    cache_vt = jnp.transpose(cache_v, (0, 1, 3, 4, 2)).reshape(n_attn, bs, KV_DIM, WINDOW)

    kp_l, vp_l, sp_l = [], [], []
    new_kv, new_state = None, None
    for i in range(depth):
        j = i // 2
        if i % 2 == 0:
            w_mix = w_attn_out_bf
            (qkv_p,) = _norm_matmul(yp, g_mix, [w_attn_in_bf], (i, j), tp)
            (qkv_s,) = _norm_matmul(ys, g_mix, [w_attn_in_bf], (i, j), ts)
            qkv_p = qkv_p.reshape(bp, lp, Q_DIM + 2 * KV_DIM)
            qkv_s = qkv_s.reshape(bs, ls, Q_DIM + 2 * KV_DIM)
            op = _attn_prompt(qkv_p, sinks[j] * LOG2E, bias_p)
            sink_col = jnp.repeat(sinks[j], ls).reshape(N_HEADS * ls, 1)
            os_, *new_kv = _attn_sample(qkv_s, cache_kt, cache_vt, j, bias_old, bias_new, sink_col,
                                        new_kv)
            kv_shape = (bp, WINDOW, N_KV_HEADS, HEAD_DIM)
            kp_l.append(qkv_p[:, lp - WINDOW:, Q_DIM:Q_DIM + KV_DIM].reshape(kv_shape))
            vp_l.append(qkv_p[:, lp - WINDOW:, Q_DIM + KV_DIM:].reshape(kv_shape))
        else:
            w_mix = w_gla_out_bf
            gla_p = _gla_proj(yp, g_mix, w_gla_main_bf, w_gla_gd_bf, w_gla_gate_up, b_gate,
                              (i, j), GLA_CHUNK, tp)
            gla_s = _gla_proj(ys, g_mix, w_gla_main_bf, w_gla_gd_bf, w_gla_gate_up, b_gate,
                              (i, j), ls, ts)
            op, sp = _gla_prompt(*[t.reshape(bp, lp, -1) for t in gla_p], g_gla, j)
            os_, new_state = _gla_sample(*[t.reshape(bs, ls, -1) for t in gla_s], state_gla, g_gla, j,
                                         new_state)
            sp_l.append(sp)
        last = i == depth - 1
        yp, *w_next = _mix_out_ffn(yp, op.reshape(bp * lp, D_MODEL), w_mix, g_ffn, w_fi, w_fo,
                                   g_fin, (j, i), last, min(FFN_TOKEN_TILE, bp * lp),
                                   None if last else (w_ffn_in, w_ffn_out))
        (ys,) = _mix_out_ffn(ys, os_.reshape(bs * ls, D_MODEL), w_mix, g_ffn, w_fi, w_fo,
                             g_fin, (j, i), last, min(FFN_TOKEN_TILE, bs * ls))
        if w_next:
            w_fi, w_fo = w_next

    def window_major(xt):
        return jnp.transpose(xt.reshape(n_attn, bs, N_KV_HEADS, HEAD_DIM, WINDOW), (0, 1, 4, 2, 3))

    return (yp.reshape(bp, lp, D_MODEL), ys.reshape(bs, ls, D_MODEL),
            jnp.stack(kp_l), jnp.stack(vp_l), jnp.stack(sp_l),
            window_major(new_kv[0]), window_major(new_kv[1]), new_state)
```

```python
import functools

import jax
import jax.numpy as jnp
from jax import lax
from jax.experimental import pallas as pl
from jax.experimental.pallas import tpu as pltpu

F32 = jnp.float32
BF16 = jnp.bfloat16

D_MODEL = 1024
EPS = 1e-6
WINDOW = 128
HEAD_DIM = 64
N_HEADS = 16
N_KV_HEADS = 4
GROUP = 4
Q_DIM = 1024
KV_DIM = 256
GLA_HEADS = 4
GLA_DK = 512
GLA_DV = 1024
GLA_DK_HEAD = 128
GLA_DV_HEAD = 256
GLA_GATE_RANK = 16
GLA_GATE_NORMALIZER = 16.0
GLA_CHUNK = 64
D_FF = 2816
MASKED = -1e30
LOG2E = 1.4426950408889634

VMEM_LIMIT_BYTES = 58 * 1024 * 1024
LANES = 128
BF16_SUBLANES = 16
TOKEN_TILE = 1024
FFN_TOKEN_TILE = 1024
FFN_TILE = 256
ATTN_Q_BLOCKS = 8
GLA_GROUP = 4
SAMPLE_BATCH_TILE = 8
NEW_KEY_PAD = 16

NT_DIMS = (((1,), (1,)), ((), ()))
TN_DIMS = (((0,), (0,)), ((), ()))


def _params(*sem):
    return pltpu.CompilerParams(dimension_semantics=sem, vmem_limit_bytes=VMEM_LIMIT_BYTES)


def _rms(x, g):
    r = lax.rsqrt(jnp.mean(x * x, axis=-1, keepdims=True) + EPS)
    return x * r * g


def _silu(x):
    return x / (1.0 + jnp.exp(-x))


def _norm_matmul_kernel(n, x_ref, g_ref, *refs):
    h = _rms(x_ref[...], g_ref[...]).astype(BF16)
    for w_ref, o_ref in zip(refs[:n], refs[n:]):
        o_ref[...] = jnp.dot(h, w_ref[...], preferred_element_type=F32)


def _norm_matmul(x, g, ws, layers, tm):
    t = x.shape[0]
    lg, lw = layers
    in_specs = [pl.BlockSpec((tm, D_MODEL), lambda i: (i, 0)),
                pl.BlockSpec((None, 1, D_MODEL), lambda i: (lg, 0, 0))]
    in_specs += [pl.BlockSpec((None,) + w.shape[1:], lambda i: (lw, 0, 0)) for w in ws]
    out_specs = [pl.BlockSpec((tm, w.shape[2]), lambda i: (i, 0)) for w in ws]
    out_shape = [jax.ShapeDtypeStruct((t, w.shape[2]), F32) for w in ws]
    return pl.pallas_call(
        functools.partial(_norm_matmul_kernel, len(ws)),
        grid=(t // tm,), in_specs=in_specs, out_specs=out_specs, out_shape=out_shape,
        compiler_params=_params("parallel"), name="norm_matmul",
    )(x, g, *ws)


def _ffn_kernel(final_norm, convert_next, x_ref, a_ref, wm_ref, g_ref, wi_ref, wo_ref, gf_ref, *rest):
    if convert_next:
        nwi_ref, nwo_ref, o_ref, owi_ref, owo_ref, act_ref = rest
        owi_ref[...] = nwi_ref[...].astype(BF16)
        owo_ref[...] = nwo_ref[...].astype(BF16)
    else:
        o_ref, act_ref = rest
    y = x_ref[...] + jnp.dot(a_ref[...], wm_ref[...], preferred_element_type=F32)
    h = _rms(y, g_ref[...]).astype(BF16)
    for j in range(D_FF // FFN_TILE):
        gate = jnp.dot(h, wi_ref[:, j * FFN_TILE:(j + 1) * FFN_TILE], preferred_element_type=F32)
        up = jnp.dot(h, wi_ref[:, D_FF + j * FFN_TILE:D_FF + (j + 1) * FFN_TILE],
                     preferred_element_type=F32)
        act_ref[:, j * FFN_TILE:(j + 1) * FFN_TILE] = (_silu(gate) * up).astype(BF16)
    y = y + jnp.dot(act_ref[...], wo_ref[...], preferred_element_type=F32)
    if final_norm:
        y = _rms(y, gf_ref[...])
    o_ref[...] = y


def _resident(block_shape, index_map):
    return pl.BlockSpec(block_shape, index_map, pipeline_mode=pl.Buffered(1))


def _slab_rows(total, steps):
    rows = BF16_SUBLANES
    while total % rows or total // rows > steps:
        rows += BF16_SUBLANES
    return rows


def _mix_out_ffn(x, a, w_mix, g, w_in, w_out, g_final, layers, final_norm, tm, next_f32=None):
    t = x.shape[0]
    lm, lf = layers
    steps = t // tm
    row = lambda i: (i, 0)
    in_specs = [pl.BlockSpec((tm, D_MODEL), row),
                pl.BlockSpec((tm, D_MODEL), row),
                _resident((None, D_MODEL, D_MODEL), lambda i: (lm, 0, 0)),
                _resident((None, 1, D_MODEL), lambda i: (lf, 0, 0)),
                _resident((D_MODEL, 2 * D_FF), lambda i: (0, 0)),
                _resident((D_FF, D_MODEL), lambda i: (0, 0)),
                _resident((1, D_MODEL), lambda i: (0, 0))]
    inputs = [x, a, w_mix, g, w_in, w_out, g_final]
    out_specs = [pl.BlockSpec((tm, D_MODEL), row)]
    out_shape = [jax.ShapeDtypeStruct((t, D_MODEL), F32)]
    if next_f32 is not None:
        for w_all in next_f32:
            rows = _slab_rows(w_all.shape[1], steps)
            last = w_all.shape[1] // rows - 1
            slab = lambda i, last=last: (jnp.minimum(i, last), 0)
            in_specs.append(pl.BlockSpec((None, rows, w_all.shape[2]),
                                         lambda i, slab=slab: (lf + 1,) + slab(i)))
            inputs.append(w_all)
            out_specs.append(pl.BlockSpec((rows, w_all.shape[2]), slab))
            out_shape.append(jax.ShapeDtypeStruct(w_all.shape[1:], BF16))
    return pl.pallas_call(
        functools.partial(_ffn_kernel, final_norm, next_f32 is not None),
        grid=(steps,), in_specs=in_specs, out_specs=out_specs, out_shape=out_shape,
        scratch_shapes=[pltpu.VMEM((tm, D_FF), BF16)],
        compiler_params=_params("arbitrary"), name="mix_out_ffn",
    )(*inputs)


HALVES = LANES // HEAD_DIM
HEAD_ORDER = tuple((HALVES * (p // (HALVES * GROUP)) + p % HALVES) * GROUP + (p % (HALVES * GROUP)) // HALVES
                   for p in range(N_HEADS))


def _regroup_heads(x, axis, width):
    lead, tail = x.shape[:axis], x.shape[axis + 1:]
    x = x.reshape(lead + (N_KV_HEADS // HALVES, HALVES, GROUP, width) + tail)
    x = jnp.swapaxes(x, axis + 1, axis + 2)
    return x.reshape(lead + (N_HEADS * width,) + tail)


def _half_masks(rows):
    low = lax.broadcasted_iota(jnp.int32, (rows, LANES), 1) < HEAD_DIM
    return low, jnp.logical_not(low)


def _attn_prompt_kernel(sink_ref, q_ref, kp_ref, kc_ref, vp_ref, vc_ref, bias_ref, o_ref):
    step = pl.program_id(1)
    q_all = (q_ref[0] * (LOG2E * HEAD_DIM ** -0.5)).astype(BF16)
    k_all = jnp.concatenate([kp_ref[0], kc_ref[0]], axis=0).astype(BF16)
    v_f32 = jnp.concatenate([vp_ref[0], vc_ref[0]], axis=0)
    masks = _half_masks(WINDOW)
    key_masks = _half_masks(2 * WINDOW)
    zero = jnp.zeros((), BF16)
    half_ones = [jnp.where(mask, 1.0, 0.0).astype(BF16) for mask in key_masks]
    for qb in range(ATTN_Q_BLOCKS):
        rows = slice(qb * WINDOW, (qb + 1) * WINDOW)
        keys = slice(qb * WINDOW, (qb + 2) * WINDOW)
        table = 1 if qb > 0 else jnp.minimum(step, 1)
        for blk in range(KV_DIM // LANES):
            lanes = slice(blk * LANES, (blk + 1) * LANES)
            k2 = k_all[keys, lanes]
            v2 = v_f32[keys, lanes]
            v_pair = jnp.concatenate(
                [jnp.concatenate([jnp.where(key_masks[half], v2, 0.0).astype(BF16), half_ones[half]],
                                 axis=1)
                 for half in range(HALVES)], axis=0)
            for c in range(blk * GROUP, (blk + 1) * GROUP):
                qc = q_all[rows, c * LANES:(c + 1) * LANES]
                probs, sink_terms = [], []
                for half in range(HALVES):
                    p = HALVES * c + half
                    qm = jnp.where(masks[half], qc, zero)
                    s = lax.dot_general(qm, k2, NT_DIMS, preferred_element_type=F32)
                    s = s + bias_ref[table, p]
                    sink = sink_ref[p]
                    m = jnp.maximum(jnp.max(s, axis=-1, keepdims=True), sink)
                    probs.append(jnp.exp2(s - m).astype(BF16))
                    sink_terms.append(jnp.exp2(sink - m))
                ov = jnp.dot(jnp.concatenate(probs, axis=1), v_pair,
                             preferred_element_type=F32)
                denom = ov[:, LANES:] + jnp.where(
                    masks[0], *[jnp.broadcast_to(t, (WINDOW, LANES)) for t in sink_terms])
                o_ref[0, rows, c * LANES:(c + 1) * LANES] = (ov[:, :LANES] / denom).astype(BF16)


def _attn_prompt(qkv, sinks, bias):
    b, l, _ = qkv.shape
    rows = ATTN_Q_BLOCKS * WINDOW
    kcol, vcol = Q_DIM // KV_DIM, Q_DIM // KV_DIM + 1
    prev = lambda i: jnp.maximum(ATTN_Q_BLOCKS * i - 1, 0)
    return pl.pallas_call(
        _attn_prompt_kernel,
        grid=(b, l // rows),
        in_specs=[pl.BlockSpec(memory_space=pltpu.SMEM),
                  pl.BlockSpec((1, rows, Q_DIM), lambda b, i: (b, i, 0)),
                  pl.BlockSpec((1, WINDOW, KV_DIM), lambda b, i: (b, prev(i), kcol)),
                  pl.BlockSpec((1, rows, KV_DIM), lambda b, i: (b, i, kcol)),
                  pl.BlockSpec((1, WINDOW, KV_DIM), lambda b, i: (b, prev(i), vcol)),
                  pl.BlockSpec((1, rows, KV_DIM), lambda b, i: (b, i, vcol)),
                  pl.BlockSpec(bias.shape, lambda b, i: (0, 0, 0, 0))],
        out_specs=pl.BlockSpec((1, rows, Q_DIM), lambda b, i: (b, i, 0)),
        out_shape=jax.ShapeDtypeStruct((b, l, Q_DIM), BF16),
        compiler_params=_params("parallel", "parallel"), name="attn_prompt",
    )(sinks, qkv, qkv, qkv, qkv, qkv, bias)


def _own_slab(ref, owner_of_all, layer):
    if not owner_of_all:
        return ref
    for other in range(ref.shape[0]):
        if other != layer:
            ref[other] = jnp.zeros(ref.shape[1:], ref.dtype)
    return ref.at[layer]


def _stacked_out_specs(prev, n_layers, layer, slab_shape, block, index):
    shape = jax.ShapeDtypeStruct((n_layers,) + slab_shape, F32)
    if prev is None:
        return pl.BlockSpec((n_layers,) + block, lambda b: (0,) + index(b)), shape, [], []
    spec = pl.BlockSpec((None,) + block, lambda b: (layer,) + index(b))
    return spec, shape, [pl.BlockSpec(memory_space=pl.ANY)], [prev]


def _attn_sample_kernel(layer, owner_of_all, q_ref, kn_ref, vn_ref, kt_ref, vt_ref, bo_ref, bn_ref,
                        sink_ref, *rest):
    o_ref, kto_ref, vto_ref = rest[-3:]
    kto_ref = _own_slab(kto_ref, owner_of_all, layer)
    vto_ref = _own_slab(vto_ref, owner_of_all, layer)
    bt, l, _ = q_ref.shape
    masks = _half_masks(l)
    keep = lax.broadcasted_iota(jnp.int32, (KV_DIM, WINDOW), 1) < WINDOW - l
    zero_block = jnp.zeros((l, LANES), F32)
    pad_rows = jnp.zeros((WINDOW - l, KV_DIM), F32)
    pad_keys = jnp.zeros((NEW_KEY_PAD - l, KV_DIM), F32)
    ones_old = jnp.ones((LANES, WINDOW), BF16)
    ones_new = jnp.ones((NEW_KEY_PAD, LANES), BF16)
    bias_old = bo_ref[...]
    bias_new = bn_ref[...]
    sink = sink_ref[...]

    def body(b, carry):
        q = q_ref[b] * (HEAD_DIM ** -0.5)
        pieces = []
        for p in range(N_HEADS):
            c = p // HALVES
            src = jnp.where(masks[p % HALVES], q[:, c * LANES:(c + 1) * LANES], 0.0)
            pieces.append(jnp.concatenate(
                [src, zero_block] if c // GROUP == 0 else [zero_block, src], axis=1))
        qt = jnp.concatenate(pieces, axis=0).astype(BF16)

        k_old, v_old = kt_ref[b], vt_ref[b]
        k_new, v_new = kn_ref[b], vn_ref[b]
        k_pad = jnp.concatenate([k_new, pad_keys], axis=0).astype(BF16)
        v_pad = jnp.concatenate([v_new, pad_keys], axis=0).astype(BF16)
        s_old = jnp.dot(qt, k_old.astype(BF16), preferred_element_type=F32) + bias_old
        s_new = lax.dot_general(qt, k_pad, NT_DIMS, preferred_element_type=F32) + bias_new
        m = jnp.maximum(sink, jnp.maximum(jnp.max(s_old, axis=-1, keepdims=True),
                                          jnp.max(s_new, axis=-1, keepdims=True)))
        p_old = jnp.exp(s_old - m).astype(BF16)
        p_new = jnp.exp(s_new - m).astype(BF16)
        v_old_1 = jnp.concatenate([v_old.astype(BF16), ones_old], axis=0)
        v_new_1 = jnp.concatenate([v_pad, ones_new], axis=1)
        o = lax.dot_general(p_old, v_old_1, NT_DIMS, preferred_element_type=F32)
        o = o + jnp.dot(p_new, v_new_1, preferred_element_type=F32)
        inv = 1.0 / (o[:, KV_DIM:] + jnp.exp(sink - m))
        cols = []
        for c in range(N_HEADS // HALVES):
            lanes = slice((c // GROUP) * LANES, (c // GROUP + 1) * LANES)
            parts = [o[(HALVES * c + half) * l:(HALVES * c + half + 1) * l, lanes]
                     * inv[(HALVES * c + half) * l:(HALVES * c + half + 1) * l, :]
                     for half in range(HALVES)]
            cols.append(jnp.where(masks[0], parts[0], parts[1]))
        o_ref[b] = jnp.concatenate(cols, axis=1).astype(BF16)

        for old, new, out_ref in ((k_old, k_new, kto_ref), (v_old, v_new, vto_ref)):
            shifted = pltpu.roll(old, WINDOW - l, axis=1)
            placed = jnp.concatenate([pad_rows, new], axis=0).T
            out_ref[b] = jnp.where(keep, shifted, placed)
        return carry

    lax.fori_loop(0, bt, body, 0, unroll=8)


def _attn_sample(qkv, cache_kt, cache_vt, layer, bias_old, bias_new, sink_col, prev_kv):
    b, l, _ = qkv.shape
    n_layers = cache_kt.shape[0]
    bt = min(SAMPLE_BATCH_TILE, b)
    kcol, vcol = Q_DIM // KV_DIM, Q_DIM // KV_DIM + 1
    const = lambda b: (0, 0)
    in_specs = [pl.BlockSpec((bt, l, Q_DIM), lambda b: (b, 0, 0)),
                pl.BlockSpec((bt, l, KV_DIM), lambda b: (b, 0, kcol)),
                pl.BlockSpec((bt, l, KV_DIM), lambda b: (b, 0, vcol)),
                pl.BlockSpec((None, bt, KV_DIM, WINDOW), lambda b: (layer, b, 0, 0)),
                pl.BlockSpec((None, bt, KV_DIM, WINDOW), lambda b: (layer, b, 0, 0)),
                pl.BlockSpec(bias_old.shape, const),
                pl.BlockSpec(bias_new.shape, const),
                pl.BlockSpec(sink_col.shape, const)]
    inputs = [qkv, qkv, qkv, cache_kt, cache_vt, bias_old, bias_new, sink_col]
    out_specs = [pl.BlockSpec((bt, l, Q_DIM), lambda b: (b, 0, 0))]
    out_shape = [jax.ShapeDtypeStruct((b, l, Q_DIM), BF16)]
    aliases = {}
    for prev in (prev_kv if prev_kv is not None else (None, None)):
        spec, shape, extra_specs, extra_inputs = _stacked_out_specs(
            prev, n_layers, layer, (b, KV_DIM, WINDOW), (bt, KV_DIM, WINDOW), lambda b: (b, 0, 0))
        if extra_inputs:
            aliases[len(inputs)] = len(out_specs)
        in_specs += extra_specs
        inputs += extra_inputs
        out_specs.append(spec)
        out_shape.append(shape)
    return pl.pallas_call(
        functools.partial(_attn_sample_kernel, layer, prev_kv is None),
        grid=(b // bt,), in_specs=in_specs, out_specs=out_specs, out_shape=out_shape,
        input_output_aliases=aliases,
        compiler_params=_params("parallel"), name="attn_sample",
    )(*inputs)


def _attn_bias_tables(l_sample):
    slopes = jnp.exp2(-8.0 * jnp.arange(1, N_HEADS + 1, dtype=F32) / N_HEADS)
    slopes = slopes[jnp.array(HEAD_ORDER)][:, None, None]
    t = jnp.arange(WINDOW, dtype=jnp.int32)[:, None]
    s = jnp.arange(2 * WINDOW, dtype=jnp.int32)[None, :]
    dist = t + WINDOW - s
    valid = (dist >= 0) & (dist <= WINDOW)
    later = jnp.where(valid[None], -slopes * dist.astype(F32)[None], MASKED)
    first = jnp.where((s >= WINDOW)[None], later, MASKED)
    prompt = jnp.stack([first, later]) * LOG2E
    tq = jnp.arange(l_sample, dtype=jnp.int32)[:, None]
    sk = jnp.arange(WINDOW + NEW_KEY_PAD, dtype=jnp.int32)[None, :]
    dist = tq + WINDOW - sk
    valid = (dist >= 0) & (dist <= WINDOW) & (sk < WINDOW + l_sample)
    sample = jnp.where(valid[None], -slopes * dist.astype(F32)[None], MASKED)
    sample = sample.reshape(N_HEADS * l_sample, WINDOW + NEW_KEY_PAD)
    return prompt, sample[:, :WINDOW], sample[:, WINDOW:]


def _log_sigmoid(x):
    return jnp.minimum(x, 0.0) - jnp.log(1.0 + jnp.exp(-jnp.abs(x)))


def _chunk_cumsum(x, c):
    pos = jnp.bitwise_and(lax.broadcasted_iota(jnp.int32, x.shape, 0), c - 1)
    shift = 1
    while shift < c:
        x = x + jnp.where(pos >= shift, pltpu.roll(x, shift, axis=0), 0.0)
        shift *= 2
    return x


def _gla_proj_kernel(chunk, x_ref, g_ref, wm_ref, wgd_ref, wg_ref, bg_ref, qk_ref, vg_ref, b_ref):
    h = _rms(x_ref[...], g_ref[...]).astype(BF16)
    gd = jnp.dot(h, wgd_ref[...], preferred_element_type=F32)
    pre = jnp.dot(gd.astype(BF16), wg_ref[...].astype(BF16), preferred_element_type=F32) + bg_ref[...]
    b_ref[...] = _chunk_cumsum(_log_sigmoid(pre) / GLA_GATE_NORMALIZER, chunk)
    n_qk, n_qkv = 2 * GLA_DK, 2 * GLA_DK + GLA_DV
    qk_ref[...] = jnp.dot(h, wm_ref[:, :n_qk], preferred_element_type=F32)
    vg_ref[:, :GLA_DV] = jnp.dot(h, wm_ref[:, n_qk:n_qkv], preferred_element_type=F32).astype(BF16)
    vg_ref[:, GLA_DV:] = _silu(
        jnp.dot(h, wm_ref[:, n_qkv:], preferred_element_type=F32)).astype(BF16)


def _gla_proj(x, g, w_main, w_gd, w_gate, b_gate, layers, chunk, tm):
    t = x.shape[0]
    lg, lw = layers
    n_main = 2 * GLA_DK + 2 * GLA_DV
    return pl.pallas_call(
        functools.partial(_gla_proj_kernel, chunk),
        grid=(t // tm,),
        in_specs=[pl.BlockSpec((tm, D_MODEL), lambda i: (i, 0)),
                  pl.BlockSpec((None, 1, D_MODEL), lambda i: (lg, 0, 0)),
                  pl.BlockSpec((None, D_MODEL, n_main), lambda i: (lw, 0, 0)),
                  pl.BlockSpec((None, D_MODEL, GLA_GATE_RANK), lambda i: (lw, 0, 0)),
                  pl.BlockSpec((None, GLA_GATE_RANK, GLA_DK), lambda i: (lw, 0, 0)),
                  pl.BlockSpec((None, 1, GLA_DK), lambda i: (lw, 0, 0))],
        out_specs=[pl.BlockSpec((tm, 2 * GLA_DK), lambda i: (i, 0)),
                   pl.BlockSpec((tm, 2 * GLA_DV), lambda i: (i, 0)),
                   pl.BlockSpec((tm, GLA_DK), lambda i: (i, 0))],
        out_shape=[jax.ShapeDtypeStruct((t, 2 * GLA_DK), F32),
                   jax.ShapeDtypeStruct((t, 2 * GLA_DV), BF16),
                   jax.ShapeDtypeStruct((t, GLA_DK), F32)],
        compiler_params=_params("parallel"), name="gla_proj",
    )(x, g, w_main, w_gd, w_gate, b_gate)


def _gla_gate_out(o, gate, gn):
    return (_rms(o, gn) * gate).astype(BF16)


def _gla_prompt_kernel(q_ref, k_ref, v_ref, r_ref, b_ref, gn_ref, o_ref, s_ref, st_ref):
    c = GLA_CHUNK
    rows = GLA_GROUP * c
    n_groups = q_ref.shape[1] // rows
    row = lax.broadcasted_iota(jnp.int32, (rows, rows), 0)
    col = lax.broadcasted_iota(jnp.int32, (rows, rows), 1)
    causal = (row >= col) & (col >= row - jnp.bitwise_and(row, c - 1))
    gn = gn_ref[...]
    st_ref[...] = jnp.zeros_like(st_ref)

    def body(g, carry):
        sl = pl.ds(pl.multiple_of(g * rows, rows), rows)
        q = q_ref[0, sl, :] * (GLA_DK_HEAD ** -0.5)
        k = k_ref[0, sl, :]
        v_bf = v_ref[0, sl, :]
        b = b_ref[0, sl, :]
        per_chunk = lambda r0: jnp.concatenate(
            [jnp.broadcast_to(b[j * c + r0:j * c + r0 + 1, :], (c, GLA_DK_HEAD))
             for j in range(GLA_GROUP)], axis=0)
        b_mid = per_chunk(c // 2)
        b_end = per_chunk(c - 1)
        qa = (q * jnp.exp(b - b_mid)).astype(BF16)
        ka = (k * jnp.exp(b_mid - b)).astype(BF16)
        a = lax.dot_general(qa, ka, NT_DIMS, preferred_element_type=F32)
        a = jnp.where(causal, a, 0.0).astype(BF16)
        o_intra = jnp.dot(a, v_bf, preferred_element_type=F32)
        qb = (q * jnp.exp(b)).astype(BF16)
        k_end = (k * jnp.exp(b_end - b)).astype(BF16)
        st = st_ref[...]
        o_inter = []
        for j in range(GLA_GROUP):
            rs = slice(j * c, (j + 1) * c)
            o_inter.append(lax.dot_general(qb[rs], st.astype(BF16), NT_DIMS,
                                           preferred_element_type=F32))
            upd = lax.dot_general(v_bf[rs], k_end[rs], TN_DIMS, preferred_element_type=F32)
            st = st * jnp.exp(b[(j + 1) * c - 1:(j + 1) * c, :]) + upd
        st_ref[...] = st
        o = o_intra + jnp.concatenate(o_inter, axis=0)
        o_ref[0, sl, :] = _gla_gate_out(o, r_ref[0, sl, :], gn)
        return carry

    lax.fori_loop(0, n_groups, body, 0, unroll=8)
    s_ref[0, 0] = st_ref[...].T


def _gla_prompt(qk, vg, decay, g_norm, layer):
    b, l, _ = qk.shape
    kcol = GLA_DK // GLA_DK_HEAD
    gcol = GLA_DV // GLA_DV_HEAD
    return pl.pallas_call(
        _gla_prompt_kernel,
        grid=(b, GLA_HEADS),
        in_specs=[pl.BlockSpec((1, l, GLA_DK_HEAD), lambda b, h: (b, 0, h)),
                  pl.BlockSpec((1, l, GLA_DK_HEAD), lambda b, h: (b, 0, kcol + h)),
                  pl.BlockSpec((1, l, GLA_DV_HEAD), lambda b, h: (b, 0, h)),
                  pl.BlockSpec((1, l, GLA_DV_HEAD), lambda b, h: (b, 0, gcol + h)),
                  pl.BlockSpec((1, l, GLA_DK_HEAD), lambda b, h: (b, 0, h)),
                  pl.BlockSpec((None, 1, GLA_DV_HEAD), lambda b, h: (layer, 0, 0))],
        out_specs=[pl.BlockSpec((1, l, GLA_DV_HEAD), lambda b, h: (b, 0, h)),
                   pl.BlockSpec((1, 1, GLA_DK_HEAD, GLA_DV_HEAD), lambda b, h: (b, h, 0, 0))],
        out_shape=[jax.ShapeDtypeStruct((b, l, GLA_DV), BF16),
                   jax.ShapeDtypeStruct((b, GLA_HEADS, GLA_DK_HEAD, GLA_DV_HEAD), F32)],
        scratch_shapes=[pltpu.VMEM((GLA_DV_HEAD, GLA_DK_HEAD), F32)],
        compiler_params=_params("parallel", "parallel"), name="gla_prompt",
    )(qk, qk, vg, vg, decay, g_norm)


def _gla_sample_kernel(layer, owner_of_all, q_ref, k_ref, v_ref, r_ref, b_ref, s0_ref, gn_ref, *rest):
    o_ref, s_ref = rest[-2:]
    s_ref = _own_slab(s_ref, owner_of_all, layer)
    bt, c, _ = q_ref.shape
    row = lax.broadcasted_iota(jnp.int32, (c, c), 0)
    col = lax.broadcasted_iota(jnp.int32, (c, c), 1)
    causal = row >= col
    gn = gn_ref[...]

    def body(i, carry):
        q = q_ref[i] * (GLA_DK_HEAD ** -0.5)
        k = k_ref[i]
        v = v_ref[i].astype(F32)
        r = r_ref[i]
        b = b_ref[i]
        b_mid = b[c // 2:c // 2 + 1, :]
        b_end = b[c - 1:c, :]
        qa = q * jnp.exp(b - b_mid)
        ka = k * jnp.exp(b_mid - b)
        qb = (q * jnp.exp(b)).astype(BF16)
        k_end_t = (k * jnp.exp(b_end - b)).T
        dec_col = jnp.broadcast_to(jnp.exp(b_end), (c, GLA_DK)).T[:, 0:1]
        for h in range(GLA_HEADS):
            ks = slice(h * GLA_DK_HEAD, (h + 1) * GLA_DK_HEAD)
            vs = slice(h * GLA_DV_HEAD, (h + 1) * GLA_DV_HEAD)
            s0 = s0_ref[i, h]
            a = lax.dot_general(qa[:, ks], ka[:, ks], NT_DIMS, preferred_element_type=F32)
            a = jnp.where(causal, a, 0.0)
            o = jnp.dot(a, v[:, vs], preferred_element_type=F32)
            o = o + jnp.dot(qb[:, ks], s0.astype(BF16), preferred_element_type=F32)
            upd = jnp.dot(k_end_t[ks, :], v[:, vs], preferred_element_type=F32)
            s_ref[i, h] = s0 * dec_col[ks, :] + upd
            o_ref[i, :, vs] = _gla_gate_out(o, r[:, vs], gn)
        return carry

    lax.fori_loop(0, bt, body, 0, unroll=2)


def _gla_sample(qk, vg, decay, state, g_norm, layer, prev_state):
    b, l, _ = qk.shape
    bt = min(SAMPLE_BATCH_TILE, b)
    state_block = (bt, GLA_HEADS, GLA_DK_HEAD, GLA_DV_HEAD)
    in_specs = [pl.BlockSpec((bt, l, GLA_DK), lambda b: (b, 0, 0)),
                pl.BlockSpec((bt, l, GLA_DK), lambda b: (b, 0, 1)),
                pl.BlockSpec((bt, l, GLA_DV), lambda b: (b, 0, 0)),
                pl.BlockSpec((bt, l, GLA_DV), lambda b: (b, 0, 1)),
                pl.BlockSpec((bt, l, GLA_DK), lambda b: (b, 0, 0)),
                pl.BlockSpec((None,) + state_block, lambda b: (layer, b, 0, 0, 0)),
                pl.BlockSpec((None, 1, GLA_DV_HEAD), lambda b: (layer, 0, 0))]
    inputs = [qk, qk, vg, vg, decay, state, g_norm]
    spec, shape, extra_specs, extra_inputs = _stacked_out_specs(
        prev_state, state.shape[0], layer, state.shape[1:], state_block, lambda b: (b, 0, 0, 0))
    return pl.pallas_call(
        functools.partial(_gla_sample_kernel, layer, prev_state is None),
        grid=(b // bt,),
        in_specs=in_specs + extra_specs,
        out_specs=[pl.BlockSpec((bt, l, GLA_DV), lambda b: (b, 0, 0)), spec],
        out_shape=[jax.ShapeDtypeStruct((b, l, GLA_DV), BF16), shape],
        input_output_aliases={len(inputs): 1} if extra_inputs else {},
        compiler_params=_params("parallel"), name="gla_sample",
    )(*inputs, *extra_inputs)


def kernel(x_prompt, x_sample, cache_k, cache_v, state_gla, norm_mix, norm_ffn, norm_final,
           w_attn_in, w_attn_out, attn_sinks, w_gla_in, w_gla_gate_up, b_gla_gate, gla_out_norm,
           w_gla_out, w_ffn_in, w_ffn_out):
    bp, lp, _ = x_prompt.shape
    bs, ls, _ = x_sample.shape
    depth = norm_mix.shape[0]
    n_attn = cache_k.shape[0]
    yp = x_prompt.reshape(bp * lp, D_MODEL)
    ys = x_sample.reshape(bs * ls, D_MODEL)
    tp = min(TOKEN_TILE, bp * lp)
    ts = min(TOKEN_TILE, bs * ls)
    bias_p, bias_old, bias_new = _attn_bias_tables(ls)
    n_main = 2 * GLA_DK + 2 * GLA_DV

    g_mix = norm_mix.reshape(depth, 1, D_MODEL)
    g_ffn = norm_ffn.reshape(depth, 1, D_MODEL)
    g_fin = norm_final.reshape(1, D_MODEL)
    w_attn_bf = w_attn_in.astype(BF16)
    w_attn_in_bf = jnp.concatenate(
        [_regroup_heads(w_attn_bf[:, :, :Q_DIM], 2, HEAD_DIM), w_attn_bf[:, :, Q_DIM:]], axis=2)
    w_attn_out_bf = _regroup_heads(w_attn_out.astype(BF16), 1, HEAD_DIM)
    sinks = _regroup_heads(attn_sinks, 1, 1)
    w_gla_main_bf = w_gla_in.astype(BF16)
    w_gla_gd_bf = w_gla_in[:, :, n_main:].astype(BF16)
    w_gla_out_bf = w_gla_out.astype(BF16)
    w_fi, w_fo = w_ffn_in[0].astype(BF16), w_ffn_out[0].astype(BF16)
    b_gate = b_gla_gate.reshape(-1, 1, GLA_DK)
    g_gla = gla_out_norm.reshape(-1, 1, GLA_DV_HEAD)
    cache_kt = jnp.transpose(cache_k, (0, 1, 3, 4, 2)).reshape(n_attn, bs, KV_DIM, WINDOW)
    cache_vt = jnp.transpose(cache_v, (0, 1, 3, 4, 2)).reshape(n_attn, bs, KV_DIM, WINDOW)

    kp_l, vp_l, sp_l = [], [], []
    new_kv, new_state = None, None
    for i in range(depth):
        j = i // 2
        if i % 2 == 0:
            w_mix = w_attn_out_bf
            (qkv_p,) = _norm_matmul(yp, g_mix, [w_attn_in_bf], (i, j), tp)
            (qkv_s,) = _norm_matmul(ys, g_mix, [w_attn_in_bf], (i, j), ts)
            qkv_p = qkv_p.reshape(bp, lp, Q_DIM + 2 * KV_DIM)
            qkv_s = qkv_s.reshape(bs, ls, Q_DIM + 2 * KV_DIM)
            op = _attn_prompt(qkv_p, sinks[j] * LOG2E, bias_p)
            sink_col = jnp.repeat(sinks[j], ls).reshape(N_HEADS * ls, 1)
            os_, *new_kv = _attn_sample(qkv_s, cache_kt, cache_vt, j, bias_old, bias_new, sink_col,
                                        new_kv)
            kv_shape = (bp, WINDOW, N_KV_HEADS, HEAD_DIM)
            kp_l.append(qkv_p[:, lp - WINDOW:, Q_DIM:Q_DIM + KV_DIM].reshape(kv_shape))
            vp_l.append(qkv_p[:, lp - WINDOW:, Q_DIM + KV_DIM:].reshape(kv_shape))
        else:
            w_mix = w_gla_out_bf
            gla_p = _gla_proj(yp, g_mix, w_gla_main_bf, w_gla_gd_bf, w_gla_gate_up, b_gate,
                              (i, j), GLA_CHUNK, tp)
            gla_s = _gla_proj(ys, g_mix, w_gla_main_bf, w_gla_gd_bf, w_gla_gate_up, b_gate,
                              (i, j), ls, ts)
            op, sp = _gla_prompt(*[t.reshape(bp, lp, -1) for t in gla_p], g_gla, j)
            os_, new_state = _gla_sample(*[t.reshape(bs, ls, -1) for t in gla_s], state_gla, g_gla, j,
                                         new_state)
            sp_l.append(sp)
        last = i == depth - 1
        yp, *w_next = _mix_out_ffn(yp, op.reshape(bp * lp, D_MODEL), w_mix, g_ffn, w_fi, w_fo,
                                   g_fin, (j, i), last, min(FFN_TOKEN_TILE, bp * lp),
                                   None if last else (w_ffn_in, w_ffn_out))
        (ys,) = _mix_out_ffn(ys, os_.reshape(bs * ls, D_MODEL), w_mix, g_ffn, w_fi, w_fo,
                             g_fin, (j, i), last, min(FFN_TOKEN_TILE, bs * ls))
        if w_next:
            w_fi, w_fo = w_next

    def window_major(xt):
        return jnp.transpose(xt.reshape(n_attn, bs, N_KV_HEADS, HEAD_DIM, WINDOW), (0, 1, 4, 2, 3))

    return (yp.reshape(bp, lp, D_MODEL), ys.reshape(bs, ls, D_MODEL),
            jnp.stack(kp_l), jnp.stack(vp_l), jnp.stack(sp_l),
            window_major(new_kv[0]), window_major(new_kv[1]), new_state)
```

```python
import functools

import jax
import jax.numpy as jnp
from jax import lax
from jax.experimental import pallas as pl
from jax.experimental.pallas import tpu as pltpu

F32 = jnp.float32
BF16 = jnp.bfloat16

D_MODEL = 1024
EPS = 1e-6
WINDOW = 128
HEAD_DIM = 64
N_HEADS = 16
N_KV_HEADS = 4
GROUP = 4
Q_DIM = 1024
KV_DIM = 256
GLA_HEADS = 4
GLA_DK = 512
GLA_DV = 1024
GLA_DK_HEAD = 128
GLA_DV_HEAD = 256
GLA_GATE_RANK = 16
GLA_GATE_NORMALIZER = 16.0
GLA_CHUNK = 64
D_FF = 2816
MASKED = -1e30
LOG2E = 1.4426950408889634

VMEM_LIMIT_BYTES = 58 * 1024 * 1024
LANES = 128
BF16_SUBLANES = 16
TOKEN_TILE = 1024
FFN_TOKEN_TILE = 1024
FFN_TILE = 256
ATTN_Q_BLOCKS = 8
GLA_GROUP = 4
SAMPLE_BATCH_TILE = 8
NEW_KEY_PAD = 16

NT_DIMS = (((1,), (1,)), ((), ()))
TN_DIMS = (((0,), (0,)), ((), ()))


def _params(*sem):
    return pltpu.CompilerParams(dimension_semantics=sem, vmem_limit_bytes=VMEM_LIMIT_BYTES)


def _rms(x, g):
    r = lax.rsqrt(jnp.mean(x * x, axis=-1, keepdims=True) + EPS)
    return x * r * g


def _silu(x):
    return x / (1.0 + jnp.exp(-x))


def _norm_matmul_kernel(n, x_ref, g_ref, *refs):
    h = _rms(x_ref[...], g_ref[...]).astype(BF16)
    for w_ref, o_ref in zip(refs[:n], refs[n:]):
        o_ref[...] = jnp.dot(h, w_ref[...], preferred_element_type=F32)


def _norm_matmul(x, g, ws, layers, tm):
    t = x.shape[0]
    lg, lw = layers
    in_specs = [pl.BlockSpec((tm, D_MODEL), lambda i: (i, 0)),
                pl.BlockSpec((None, 1, D_MODEL), lambda i: (lg, 0, 0))]
    in_specs += [pl.BlockSpec((None,) + w.shape[1:], lambda i: (lw, 0, 0)) for w in ws]
    out_specs = [pl.BlockSpec((tm, w.shape[2]), lambda i: (i, 0)) for w in ws]
    out_shape = [jax.ShapeDtypeStruct((t, w.shape[2]), F32) for w in ws]
    return pl.pallas_call(
        functools.partial(_norm_matmul_kernel, len(ws)),
        grid=(t // tm,), in_specs=in_specs, out_specs=out_specs, out_shape=out_shape,
        compiler_params=_params("parallel"), name="norm_matmul",
    )(x, g, *ws)


def _slab_rows(total, steps):
    rows = BF16_SUBLANES
    while total % rows or total // rows > steps:
        rows += BF16_SUBLANES
    return rows


def _cast_specs(sources, steps, step_of):
    in_specs, out_specs, out_shape = [], [], []
    for w_all, layer in sources:
        rows = _slab_rows(w_all.shape[1], steps)
        last = w_all.shape[1] // rows - 1
        slab = lambda *g, last=last: (jnp.minimum(step_of(*g), last), 0)
        in_specs.append(pl.BlockSpec((None, rows, w_all.shape[2]),
                                     lambda *g, slab=slab, layer=layer: (layer,) + slab(*g)))
        out_specs.append(pl.BlockSpec((rows, w_all.shape[2]), slab))
        out_shape.append(jax.ShapeDtypeStruct(w_all.shape[1:], BF16))
    return in_specs, out_specs, out_shape


def _cast_slabs(src_refs, dst_refs):
    for src, dst in zip(src_refs, dst_refs):
        dst[...] = src[...].astype(BF16)


def _ffn_kernel(final_norm, n_cast, x_ref, a_ref, wm_ref, g_ref, wi_ref, wo_ref, gf_ref, *rest):
    cast_src, rest = rest[:n_cast], rest[n_cast:]
    o_ref, cast_dst, act_ref = rest[0], rest[1:1 + n_cast], rest[-1]
    _cast_slabs(cast_src, cast_dst)
    y = x_ref[...] + jnp.dot(a_ref[...], wm_ref[...], preferred_element_type=F32)
    h = _rms(y, g_ref[...]).astype(BF16)
    for j in range(D_FF // FFN_TILE):
        gate = jnp.dot(h, wi_ref[:, j * FFN_TILE:(j + 1) * FFN_TILE], preferred_element_type=F32)
        up = jnp.dot(h, wi_ref[:, D_FF + j * FFN_TILE:D_FF + (j + 1) * FFN_TILE],
                     preferred_element_type=F32)
        act_ref[:, j * FFN_TILE:(j + 1) * FFN_TILE] = (_silu(gate) * up).astype(BF16)
    y = y + jnp.dot(act_ref[...], wo_ref[...], preferred_element_type=F32)
    if final_norm:
        y = _rms(y, gf_ref[...])
    o_ref[...] = y


def _resident(block_shape, index_map):
    return pl.BlockSpec(block_shape, index_map, pipeline_mode=pl.Buffered(1))


def _mix_out_ffn(x, a, w_mix, g, w_in, w_out, g_final, layers, final_norm, tm, cast=()):
    t = x.shape[0]
    lm, lf = layers
    steps = t // tm
    row = lambda i: (i, 0)
    cast_in, cast_out, cast_shape = _cast_specs(cast, steps, lambda i: i)
    in_specs = [pl.BlockSpec((tm, D_MODEL), row),
                pl.BlockSpec((tm, D_MODEL), row),
                _resident((None, D_MODEL, D_MODEL), lambda i: (lm, 0, 0)),
                _resident((None, 1, D_MODEL), lambda i: (lf, 0, 0)),
                _resident((D_MODEL, 2 * D_FF), lambda i: (0, 0)),
                _resident((D_FF, D_MODEL), lambda i: (0, 0)),
                _resident((1, D_MODEL), lambda i: (0, 0))]
    return pl.pallas_call(
        functools.partial(_ffn_kernel, final_norm, len(cast)),
        grid=(steps,),
        in_specs=in_specs + cast_in,
        out_specs=[pl.BlockSpec((tm, D_MODEL), row)] + cast_out,
        out_shape=[jax.ShapeDtypeStruct((t, D_MODEL), F32)] + cast_shape,
        scratch_shapes=[pltpu.VMEM((tm, D_FF), BF16)],
        compiler_params=_params("arbitrary"), name="mix_out_ffn",
    )(x, a, w_mix, g, w_in, w_out, g_final, *[w for w, _ in cast])


HALVES = LANES // HEAD_DIM
HEAD_ORDER = tuple((HALVES * (p // (HALVES * GROUP)) + p % HALVES) * GROUP + (p % (HALVES * GROUP)) // HALVES
                   for p in range(N_HEADS))


def _regroup_heads(x, axis, width):
    lead, tail = x.shape[:axis], x.shape[axis + 1:]
    x = x.reshape(lead + (N_KV_HEADS // HALVES, HALVES, GROUP, width) + tail)
    x = jnp.swapaxes(x, axis + 1, axis + 2)
    return x.reshape(lead + (N_HEADS * width,) + tail)


def _half_masks(rows):
    low = lax.broadcasted_iota(jnp.int32, (rows, LANES), 1) < HEAD_DIM
    return low, jnp.logical_not(low)


def _attn_prompt_kernel(n_cast, sink_ref, q_ref, kp_ref, kc_ref, vp_ref, vc_ref, bias_ref, *rest):
    o_ref = rest[n_cast]
    _cast_slabs(rest[:n_cast], rest[n_cast + 1:])
    step = pl.program_id(1)
    q_all = (q_ref[0] * (LOG2E * HEAD_DIM ** -0.5)).astype(BF16)
    k_all = jnp.concatenate([kp_ref[0], kc_ref[0]], axis=0).astype(BF16)
    v_f32 = jnp.concatenate([vp_ref[0], vc_ref[0]], axis=0)
    masks = _half_masks(WINDOW)
    key_masks = _half_masks(2 * WINDOW)
    zero = jnp.zeros((), BF16)
    half_ones = [jnp.where(mask, 1.0, 0.0).astype(BF16) for mask in key_masks]
    for qb in range(ATTN_Q_BLOCKS):
        rows = slice(qb * WINDOW, (qb + 1) * WINDOW)
        keys = slice(qb * WINDOW, (qb + 2) * WINDOW)
        table = 1 if qb > 0 else jnp.minimum(step, 1)
        for blk in range(KV_DIM // LANES):
            lanes = slice(blk * LANES, (blk + 1) * LANES)
            k2 = k_all[keys, lanes]
            v2 = v_f32[keys, lanes]
            v_pair = jnp.concatenate(
                [jnp.concatenate([jnp.where(key_masks[half], v2, 0.0).astype(BF16), half_ones[half]],
                                 axis=1)
                 for half in range(HALVES)], axis=0)
            for c in range(blk * GROUP, (blk + 1) * GROUP):
                qc = q_all[rows, c * LANES:(c + 1) * LANES]
                probs, sink_terms = [], []
                for half in range(HALVES):
                    p = HALVES * c + half
                    qm = jnp.where(masks[half], qc, zero)
                    s = lax.dot_general(qm, k2, NT_DIMS, preferred_element_type=F32)
                    s = s + bias_ref[table, p]
                    sink = sink_ref[p]
                    m = jnp.maximum(jnp.max(s, axis=-1, keepdims=True), sink)
                    probs.append(jnp.exp2(s - m).astype(BF16))
                    sink_terms.append(jnp.exp2(sink - m))
                ov = jnp.dot(jnp.concatenate(probs, axis=1), v_pair,
                             preferred_element_type=F32)
                denom = ov[:, LANES:] + jnp.where(
                    masks[0], *[jnp.broadcast_to(t, (WINDOW, LANES)) for t in sink_terms])
                o_ref[0, rows, c * LANES:(c + 1) * LANES] = (ov[:, :LANES] / denom).astype(BF16)


def _attn_prompt(qkv, sinks, bias, cast=()):
    b, l, _ = qkv.shape
    rows = ATTN_Q_BLOCKS * WINDOW
    steps = l // rows
    kcol, vcol = Q_DIM // KV_DIM, Q_DIM // KV_DIM + 1
    prev = lambda i: jnp.maximum(ATTN_Q_BLOCKS * i - 1, 0)
    cast_in, cast_out, cast_shape = _cast_specs(cast, b * steps, lambda b, i: b * steps + i)
    return pl.pallas_call(
        functools.partial(_attn_prompt_kernel, len(cast)),
        grid=(b, steps),
        in_specs=[pl.BlockSpec(memory_space=pltpu.SMEM),
                  pl.BlockSpec((1, rows, Q_DIM), lambda b, i: (b, i, 0)),
                  pl.BlockSpec((1, WINDOW, KV_DIM), lambda b, i: (b, prev(i), kcol)),
                  pl.BlockSpec((1, rows, KV_DIM), lambda b, i: (b, i, kcol)),
                  pl.BlockSpec((1, WINDOW, KV_DIM), lambda b, i: (b, prev(i), vcol)),
                  pl.BlockSpec((1, rows, KV_DIM), lambda b, i: (b, i, vcol)),
                  pl.BlockSpec(bias.shape, lambda b, i: (0, 0, 0, 0))] + cast_in,
        out_specs=[pl.BlockSpec((1, rows, Q_DIM), lambda b, i: (b, i, 0))] + cast_out,
        out_shape=[jax.ShapeDtypeStruct((b, l, Q_DIM), BF16)] + cast_shape,
        compiler_params=_params("arbitrary", "arbitrary"), name="attn_prompt",
    )(sinks, qkv, qkv, qkv, qkv, qkv, bias, *[w for w, _ in cast])


def _own_slab(ref, owner_of_all, layer):
    if not owner_of_all:
        return ref
    for other in range(ref.shape[0]):
        if other != layer:
            ref[other] = jnp.zeros(ref.shape[1:], ref.dtype)
    return ref.at[layer]


def _stacked_out_specs(prev, n_layers, layer, slab_shape, block, index):
    shape = jax.ShapeDtypeStruct((n_layers,) + slab_shape, F32)
    if prev is None:
        return pl.BlockSpec((n_layers,) + block, lambda b: (0,) + index(b)), shape, [], []
    spec = pl.BlockSpec((None,) + block, lambda b: (layer,) + index(b))
    return spec, shape, [pl.BlockSpec(memory_space=pl.ANY)], [prev]


def _attn_sample_kernel(layer, owner_of_all, q_ref, kn_ref, vn_ref, kt_ref, vt_ref, bo_ref, bn_ref,
                        sink_ref, *rest):
    o_ref, kto_ref, vto_ref = rest[-3:]
    kto_ref = _own_slab(kto_ref, owner_of_all, layer)
    vto_ref = _own_slab(vto_ref, owner_of_all, layer)
    bt, l, _ = q_ref.shape
    masks = _half_masks(l)
    keep = lax.broadcasted_iota(jnp.int32, (KV_DIM, WINDOW), 1) < WINDOW - l
    zero_block = jnp.zeros((l, LANES), F32)
    pad_rows = jnp.zeros((WINDOW - l, KV_DIM), F32)
    pad_keys = jnp.zeros((NEW_KEY_PAD - l, KV_DIM), F32)
    ones_old = jnp.ones((LANES, WINDOW), BF16)
    ones_new = jnp.ones((NEW_KEY_PAD, LANES), BF16)
    bias_old = bo_ref[...]
    bias_new = bn_ref[...]
    sink = sink_ref[...]

    def body(b, carry):
        q = q_ref[b] * (HEAD_DIM ** -0.5)
        pieces = []
        for p in range(N_HEADS):
            c = p // HALVES
            src = jnp.where(masks[p % HALVES], q[:, c * LANES:(c + 1) * LANES], 0.0)
            pieces.append(jnp.concatenate(
                [src, zero_block] if c // GROUP == 0 else [zero_block, src], axis=1))
        qt = jnp.concatenate(pieces, axis=0).astype(BF16)

        k_old, v_old = kt_ref[b], vt_ref[b]
        k_new, v_new = kn_ref[b], vn_ref[b]
        k_pad = jnp.concatenate([k_new, pad_keys], axis=0).astype(BF16)
        v_pad = jnp.concatenate([v_new, pad_keys], axis=0).astype(BF16)
        s_old = jnp.dot(qt, k_old.astype(BF16), preferred_element_type=F32) + bias_old
        s_new = lax.dot_general(qt, k_pad, NT_DIMS, preferred_element_type=F32) + bias_new
        m = jnp.maximum(sink, jnp.maximum(jnp.max(s_old, axis=-1, keepdims=True),
                                          jnp.max(s_new, axis=-1, keepdims=True)))
        p_old = jnp.exp(s_old - m).astype(BF16)
        p_new = jnp.exp(s_new - m).astype(BF16)
        v_old_1 = jnp.concatenate([v_old.astype(BF16), ones_old], axis=0)
        v_new_1 = jnp.concatenate([v_pad, ones_new], axis=1)
        o = lax.dot_general(p_old, v_old_1, NT_DIMS, preferred_element_type=F32)
        o = o + jnp.dot(p_new, v_new_1, preferred_element_type=F32)
        inv = 1.0 / (o[:, KV_DIM:] + jnp.exp(sink - m))
        cols = []
        for c in range(N_HEADS // HALVES):
            lanes = slice((c // GROUP) * LANES, (c // GROUP + 1) * LANES)
            parts = [o[(HALVES * c + half) * l:(HALVES * c + half + 1) * l, lanes]
                     * inv[(HALVES * c + half) * l:(HALVES * c + half + 1) * l, :]
                     for half in range(HALVES)]
            cols.append(jnp.where(masks[0], parts[0], parts[1]))
        o_ref[b] = jnp.concatenate(cols, axis=1).astype(BF16)

        for old, new, out_ref in ((k_old, k_new, kto_ref), (v_old, v_new, vto_ref)):
            shifted = pltpu.roll(old, WINDOW - l, axis=1)
            placed = jnp.concatenate([pad_rows, new], axis=0).T
            out_ref[b] = jnp.where(keep, shifted, placed)
        return carry

    lax.fori_loop(0, bt, body, 0, unroll=8)


def _attn_sample(qkv, cache_kt, cache_vt, layer, bias_old, bias_new, sink_col, prev_kv):
    b, l, _ = qkv.shape
    n_layers = cache_kt.shape[0]
    bt = min(SAMPLE_BATCH_TILE, b)
    kcol, vcol = Q_DIM // KV_DIM, Q_DIM // KV_DIM + 1
    const = lambda b: (0, 0)
    in_specs = [pl.BlockSpec((bt, l, Q_DIM), lambda b: (b, 0, 0)),
                pl.BlockSpec((bt, l, KV_DIM), lambda b: (b, 0, kcol)),
                pl.BlockSpec((bt, l, KV_DIM), lambda b: (b, 0, vcol)),
                pl.BlockSpec((None, bt, KV_DIM, WINDOW), lambda b: (layer, b, 0, 0)),
                pl.BlockSpec((None, bt, KV_DIM, WINDOW), lambda b: (layer, b, 0, 0)),
                pl.BlockSpec(bias_old.shape, const),
                pl.BlockSpec(bias_new.shape, const),
                pl.BlockSpec(sink_col.shape, const)]
    inputs = [qkv, qkv, qkv, cache_kt, cache_vt, bias_old, bias_new, sink_col]
    out_specs = [pl.BlockSpec((bt, l, Q_DIM), lambda b: (b, 0, 0))]
    out_shape = [jax.ShapeDtypeStruct((b, l, Q_DIM), BF16)]
    aliases = {}
    for prev in (prev_kv if prev_kv is not None else (None, None)):
        spec, shape, extra_specs, extra_inputs = _stacked_out_specs(
            prev, n_layers, layer, (b, KV_DIM, WINDOW), (bt, KV_DIM, WINDOW), lambda b: (b, 0, 0))
        if extra_inputs:
            aliases[len(inputs)] = len(out_specs)
        in_specs += extra_specs
        inputs += extra_inputs
        out_specs.append(spec)
        out_shape.append(shape)
    return pl.pallas_call(
        functools.partial(_attn_sample_kernel, layer, prev_kv is None),
        grid=(b // bt,), in_specs=in_specs, out_specs=out_specs, out_shape=out_shape,
        input_output_aliases=aliases,
        compiler_params=_params("parallel"), name="attn_sample",
    )(*inputs)


def _attn_bias_tables(l_sample):
    slopes = jnp.exp2(-8.0 * jnp.arange(1, N_HEADS + 1, dtype=F32) / N_HEADS)
    slopes = slopes[jnp.array(HEAD_ORDER)][:, None, None]
    t = jnp.arange(WINDOW, dtype=jnp.int32)[:, None]
    s = jnp.arange(2 * WINDOW, dtype=jnp.int32)[None, :]
    dist = t + WINDOW - s
    valid = (dist >= 0) & (dist <= WINDOW)
    later = jnp.where(valid[None], -slopes * dist.astype(F32)[None], MASKED)
    first = jnp.where((s >= WINDOW)[None], later, MASKED)
    prompt = jnp.stack([first, later]) * LOG2E
    tq = jnp.arange(l_sample, dtype=jnp.int32)[:, None]
    sk = jnp.arange(WINDOW + NEW_KEY_PAD, dtype=jnp.int32)[None, :]
    dist = tq + WINDOW - sk
    valid = (dist >= 0) & (dist <= WINDOW) & (sk < WINDOW + l_sample)
    sample = jnp.where(valid[None], -slopes * dist.astype(F32)[None], MASKED)
    sample = sample.reshape(N_HEADS * l_sample, WINDOW + NEW_KEY_PAD)
    return prompt, sample[:, :WINDOW], sample[:, WINDOW:]


def _log_sigmoid(x):
    return jnp.minimum(x, 0.0) - jnp.log(1.0 + jnp.exp(-jnp.abs(x)))


def _chunk_cumsum(x, c):
    pos = jnp.bitwise_and(lax.broadcasted_iota(jnp.int32, x.shape, 0), c - 1)
    shift = 1
    while shift < c:
        x = x + jnp.where(pos >= shift, pltpu.roll(x, shift, axis=0), 0.0)
        shift *= 2
    return x


def _gla_proj_kernel(chunk, x_ref, g_ref, wm_ref, wgd_ref, wg_ref, bg_ref, qk_ref, vg_ref, b_ref):
    h = _rms(x_ref[...], g_ref[...]).astype(BF16)
    gd = jnp.dot(h, wgd_ref[...], preferred_element_type=F32)
    pre = jnp.dot(gd.astype(BF16), wg_ref[...].astype(BF16), preferred_element_type=F32) + bg_ref[...]
    b_ref[...] = _chunk_cumsum(_log_sigmoid(pre) / GLA_GATE_NORMALIZER, chunk)
    n_qk, n_qkv = 2 * GLA_DK, 2 * GLA_DK + GLA_DV
    qk_ref[...] = jnp.dot(h, wm_ref[:, :n_qk], preferred_element_type=F32)
    vg_ref[:, :GLA_DV] = jnp.dot(h, wm_ref[:, n_qk:n_qkv], preferred_element_type=F32).astype(BF16)
    vg_ref[:, GLA_DV:] = _silu(
        jnp.dot(h, wm_ref[:, n_qkv:], preferred_element_type=F32)).astype(BF16)


def _gla_proj(x, g, w_main, w_gd, w_gate, b_gate, layers, chunk, tm):
    t = x.shape[0]
    lg, lw = layers
    n_main = 2 * GLA_DK + 2 * GLA_DV
    return pl.pallas_call(
        functools.partial(_gla_proj_kernel, chunk),
        grid=(t // tm,),
        in_specs=[pl.BlockSpec((tm, D_MODEL), lambda i: (i, 0)),
                  pl.BlockSpec((None, 1, D_MODEL), lambda i: (lg, 0, 0)),
                  pl.BlockSpec((None, D_MODEL, n_main), lambda i: (lw, 0, 0)),
                  pl.BlockSpec((None, D_MODEL, GLA_GATE_RANK), lambda i: (lw, 0, 0)),
                  pl.BlockSpec((None, GLA_GATE_RANK, GLA_DK), lambda i: (lw, 0, 0)),
                  pl.BlockSpec((None, 1, GLA_DK), lambda i: (lw, 0, 0))],
        out_specs=[pl.BlockSpec((tm, 2 * GLA_DK), lambda i: (i, 0)),
                   pl.BlockSpec((tm, 2 * GLA_DV), lambda i: (i, 0)),
                   pl.BlockSpec((tm, GLA_DK), lambda i: (i, 0))],
        out_shape=[jax.ShapeDtypeStruct((t, 2 * GLA_DK), F32),
                   jax.ShapeDtypeStruct((t, 2 * GLA_DV), BF16),
                   jax.ShapeDtypeStruct((t, GLA_DK), F32)],
        compiler_params=_params("parallel"), name="gla_proj",
    )(x, g, w_main, w_gd, w_gate, b_gate)


def _gla_gate_out(o, gate, gn):
    return (_rms(o, gn) * gate).astype(BF16)


def _gla_prompt_kernel(q_ref, k_ref, v_ref, r_ref, b_ref, gn_ref, o_ref, s_ref, st_ref):
    c = GLA_CHUNK
    rows = GLA_GROUP * c
    n_groups = q_ref.shape[1] // rows
    row = lax.broadcasted_iota(jnp.int32, (rows, rows), 0)
    col = lax.broadcasted_iota(jnp.int32, (rows, rows), 1)
    causal = (row >= col) & (col >= row - jnp.bitwise_and(row, c - 1))
    gn = gn_ref[...]
    st_ref[...] = jnp.zeros_like(st_ref)

    def body(g, carry):
        sl = pl.ds(pl.multiple_of(g * rows, rows), rows)
        q = q_ref[0, sl, :] * (GLA_DK_HEAD ** -0.5)
        k = k_ref[0, sl, :]
        v_bf = v_ref[0, sl, :]
        b = b_ref[0, sl, :]
        per_chunk = lambda r0: jnp.concatenate(
            [jnp.broadcast_to(b[j * c + r0:j * c + r0 + 1, :], (c, GLA_DK_HEAD))
             for j in range(GLA_GROUP)], axis=0)
        b_mid = per_chunk(c // 2)
        b_end = per_chunk(c - 1)
        qa = (q * jnp.exp(b - b_mid)).astype(BF16)
        ka = (k * jnp.exp(b_mid - b)).astype(BF16)
        a = lax.dot_general(qa, ka, NT_DIMS, preferred_element_type=F32)
        a = jnp.where(causal, a, 0.0).astype(BF16)
        o_intra = jnp.dot(a, v_bf, preferred_element_type=F32)
        qb = (q * jnp.exp(b)).astype(BF16)
        k_end = (k * jnp.exp(b_end - b)).astype(BF16)
        st = st_ref[...]
        o_inter = []
        for j in range(GLA_GROUP):
            rs = slice(j * c, (j + 1) * c)
            o_inter.append(lax.dot_general(qb[rs], st.astype(BF16), NT_DIMS,
                                           preferred_element_type=F32))
            upd = lax.dot_general(v_bf[rs], k_end[rs], TN_DIMS, preferred_element_type=F32)
            st = st * jnp.exp(b[(j + 1) * c - 1:(j + 1) * c, :]) + upd
        st_ref[...] = st
        o = o_intra + jnp.concatenate(o_inter, axis=0)
        o_ref[0, sl, :] = _gla_gate_out(o, r_ref[0, sl, :], gn)
        return carry

    lax.fori_loop(0, n_groups, body, 0, unroll=8)
    s_ref[0, 0] = st_ref[...].T


def _gla_prompt(qk, vg, decay, g_norm, layer):
    b, l, _ = qk.shape
    kcol = GLA_DK // GLA_DK_HEAD
    gcol = GLA_DV // GLA_DV_HEAD
    return pl.pallas_call(
        _gla_prompt_kernel,
        grid=(b, GLA_HEADS),
        in_specs=[pl.BlockSpec((1, l, GLA_DK_HEAD), lambda b, h: (b, 0, h)),
                  pl.BlockSpec((1, l, GLA_DK_HEAD), lambda b, h: (b, 0, kcol + h)),
                  pl.BlockSpec((1, l, GLA_DV_HEAD), lambda b, h: (b, 0, h)),
                  pl.BlockSpec((1, l, GLA_DV_HEAD), lambda b, h: (b, 0, gcol + h)),
                  pl.BlockSpec((1, l, GLA_DK_HEAD), lambda b, h: (b, 0, h)),
                  pl.BlockSpec((None, 1, GLA_DV_HEAD), lambda b, h: (layer, 0, 0))],
        out_specs=[pl.BlockSpec((1, l, GLA_DV_HEAD), lambda b, h: (b, 0, h)),
                   pl.BlockSpec((1, 1, GLA_DK_HEAD, GLA_DV_HEAD), lambda b, h: (b, h, 0, 0))],
        out_shape=[jax.ShapeDtypeStruct((b, l, GLA_DV), BF16),
                   jax.ShapeDtypeStruct((b, GLA_HEADS, GLA_DK_HEAD, GLA_DV_HEAD), F32)],
        scratch_shapes=[pltpu.VMEM((GLA_DV_HEAD, GLA_DK_HEAD), F32)],
        compiler_params=_params("parallel", "parallel"), name="gla_prompt",
    )(qk, qk, vg, vg, decay, g_norm)


def _gla_sample_kernel(layer, owner_of_all, q_ref, k_ref, v_ref, r_ref, b_ref, s0_ref, gn_ref, *rest):
    o_ref, s_ref = rest[-2:]
    s_ref = _own_slab(s_ref, owner_of_all, layer)
    bt, c, _ = q_ref.shape
    row = lax.broadcasted_iota(jnp.int32, (c, c), 0)
    col = lax.broadcasted_iota(jnp.int32, (c, c), 1)
    causal = row >= col
    gn = gn_ref[...]

    def body(i, carry):
        q = q_ref[i] * (GLA_DK_HEAD ** -0.5)
        k = k_ref[i]
        v = v_ref[i].astype(F32)
        r = r_ref[i]
        b = b_ref[i]
        b_mid = b[c // 2:c // 2 + 1, :]
        b_end = b[c - 1:c, :]
        qa = q * jnp.exp(b - b_mid)
        ka = k * jnp.exp(b_mid - b)
        qb = (q * jnp.exp(b)).astype(BF16)
        k_end_t = (k * jnp.exp(b_end - b)).T
        dec_col = jnp.broadcast_to(jnp.exp(b_end), (c, GLA_DK)).T[:, 0:1]
        for h in range(GLA_HEADS):
            ks = slice(h * GLA_DK_HEAD, (h + 1) * GLA_DK_HEAD)
            vs = slice(h * GLA_DV_HEAD, (h + 1) * GLA_DV_HEAD)
            s0 = s0_ref[i, h]
            a = lax.dot_general(qa[:, ks], ka[:, ks], NT_DIMS, preferred_element_type=F32)
            a = jnp.where(causal, a, 0.0)
            o = jnp.dot(a, v[:, vs], preferred_element_type=F32)
            o = o + jnp.dot(qb[:, ks], s0.astype(BF16), preferred_element_type=F32)
            upd = jnp.dot(k_end_t[ks, :], v[:, vs], preferred_element_type=F32)
            s_ref[i, h] = s0 * dec_col[ks, :] + upd
            o_ref[i, :, vs] = _gla_gate_out(o, r[:, vs], gn)
        return carry

    lax.fori_loop(0, bt, body, 0, unroll=2)


def _gla_sample(qk, vg, decay, state, g_norm, layer, prev_state):
    b, l, _ = qk.shape
    bt = min(SAMPLE_BATCH_TILE, b)
    state_block = (bt, GLA_HEADS, GLA_DK_HEAD, GLA_DV_HEAD)
    in_specs = [pl.BlockSpec((bt, l, GLA_DK), lambda b: (b, 0, 0)),
                pl.BlockSpec((bt, l, GLA_DK), lambda b: (b, 0, 1)),
                pl.BlockSpec((bt, l, GLA_DV), lambda b: (b, 0, 0)),
                pl.BlockSpec((bt, l, GLA_DV), lambda b: (b, 0, 1)),
                pl.BlockSpec((bt, l, GLA_DK), lambda b: (b, 0, 0)),
                pl.BlockSpec((None,) + state_block, lambda b: (layer, b, 0, 0, 0)),
                pl.BlockSpec((None, 1, GLA_DV_HEAD), lambda b: (layer, 0, 0))]
    inputs = [qk, qk, vg, vg, decay, state, g_norm]
    spec, shape, extra_specs, extra_inputs = _stacked_out_specs(
        prev_state, state.shape[0], layer, state.shape[1:], state_block, lambda b: (b, 0, 0, 0))
    return pl.pallas_call(
        functools.partial(_gla_sample_kernel, layer, prev_state is None),
        grid=(b // bt,),
        in_specs=in_specs + extra_specs,
        out_specs=[pl.BlockSpec((bt, l, GLA_DV), lambda b: (b, 0, 0)), spec],
        out_shape=[jax.ShapeDtypeStruct((b, l, GLA_DV), BF16), shape],
        input_output_aliases={len(inputs): 1} if extra_inputs else {},
        compiler_params=_params("parallel"), name="gla_sample",
    )(*inputs, *extra_inputs)


def kernel(x_prompt, x_sample, cache_k, cache_v, state_gla, norm_mix, norm_ffn, norm_final,
           w_attn_in, w_attn_out, attn_sinks, w_gla_in, w_gla_gate_up, b_gla_gate, gla_out_norm,
           w_gla_out, w_ffn_in, w_ffn_out):
    bp, lp, _ = x_prompt.shape
    bs, ls, _ = x_sample.shape
    depth = norm_mix.shape[0]
    n_attn = cache_k.shape[0]
    yp = x_prompt.reshape(bp * lp, D_MODEL)
    ys = x_sample.reshape(bs * ls, D_MODEL)
    tp = min(TOKEN_TILE, bp * lp)
    ts = min(TOKEN_TILE, bs * ls)
    bias_p, bias_old, bias_new = _attn_bias_tables(ls)
    n_main = 2 * GLA_DK + 2 * GLA_DV

    g_mix = norm_mix.reshape(depth, 1, D_MODEL)
    g_ffn = norm_ffn.reshape(depth, 1, D_MODEL)
    g_fin = norm_final.reshape(1, D_MODEL)
    w_attn_bf = w_attn_in.astype(BF16)
    w_attn_in_bf = jnp.concatenate(
        [_regroup_heads(w_attn_bf[:, :, :Q_DIM], 2, HEAD_DIM), w_attn_bf[:, :, Q_DIM:]], axis=2)
    w_attn_out_bf = _regroup_heads(w_attn_out.astype(BF16), 1, HEAD_DIM)
    sinks = _regroup_heads(attn_sinks, 1, 1)
    w_gla_main_bf = w_gla_in.astype(BF16)
    w_gla_gd_bf = w_gla_in[:, :, n_main:].astype(BF16)
    w_fi = w_fo = w_gla_out_bf = None
    b_gate = b_gla_gate.reshape(-1, 1, GLA_DK)
    g_gla = gla_out_norm.reshape(-1, 1, GLA_DV_HEAD)
    cache_kt = jnp.transpose(cache_k, (0, 1, 3, 4, 2)).reshape(n_attn, bs, KV_DIM, WINDOW)
    cache_vt = jnp.transpose(cache_v, (0, 1, 3, 4, 2)).reshape(n_attn, bs, KV_DIM, WINDOW)

    kp_l, vp_l, sp_l = [], [], []
    new_kv, new_state = None, None
    for i in range(depth):
        j = i // 2
        if i % 2 == 0:
            w_mix, lm = w_attn_out_bf, j
            (qkv_p,) = _norm_matmul(yp, g_mix, [w_attn_in_bf], (i, j), tp)
            (qkv_s,) = _norm_matmul(ys, g_mix, [w_attn_in_bf], (i, j), ts)
            qkv_p = qkv_p.reshape(bp, lp, Q_DIM + 2 * KV_DIM)
            qkv_s = qkv_s.reshape(bs, ls, Q_DIM + 2 * KV_DIM)
            cast = [(w_gla_out, j)] + ([(w_ffn_in, 0), (w_ffn_out, 0)] if i == 0 else [])
            op, w_gla_out_bf, *w_ffn0 = _attn_prompt(qkv_p, sinks[j] * LOG2E, bias_p, cast)
            if w_ffn0:
                w_fi, w_fo = w_ffn0
            sink_col = jnp.repeat(sinks[j], ls).reshape(N_HEADS * ls, 1)
            os_, *new_kv = _attn_sample(qkv_s, cache_kt, cache_vt, j, bias_old, bias_new, sink_col,
                                        new_kv)
            kv_shape = (bp, WINDOW, N_KV_HEADS, HEAD_DIM)
            kp_l.append(qkv_p[:, lp - WINDOW:, Q_DIM:Q_DIM + KV_DIM].reshape(kv_shape))
            vp_l.append(qkv_p[:, lp - WINDOW:, Q_DIM + KV_DIM:].reshape(kv_shape))
        else:
            w_mix, lm = w_gla_out_bf[None], 0
            gla_p = _gla_proj(yp, g_mix, w_gla_main_bf, w_gla_gd_bf, w_gla_gate_up, b_gate,
                              (i, j), GLA_CHUNK, tp)
            gla_s = _gla_proj(ys, g_mix, w_gla_main_bf, w_gla_gd_bf, w_gla_gate_up, b_gate,
                              (i, j), ls, ts)
            op, sp = _gla_prompt(*[t.reshape(bp, lp, -1) for t in gla_p], g_gla, j)
            os_, new_state = _gla_sample(*[t.reshape(bs, ls, -1) for t in gla_s], state_gla, g_gla, j,
                                         new_state)
            sp_l.append(sp)
        last = i == depth - 1
        yp, *w_next = _mix_out_ffn(yp, op.reshape(bp * lp, D_MODEL), w_mix, g_ffn, w_fi, w_fo,
                                   g_fin, (lm, i), last, min(FFN_TOKEN_TILE, bp * lp),
                                   [] if last else [(w_ffn_in, i + 1), (w_ffn_out, i + 1)])
        (ys,) = _mix_out_ffn(ys, os_.reshape(bs * ls, D_MODEL), w_mix, g_ffn, w_fi, w_fo,
                             g_fin, (lm, i), last, min(FFN_TOKEN_TILE, bs * ls))
        if w_next:
            w_fi, w_fo = w_next

    def window_major(xt):
        return jnp.transpose(xt.reshape(n_attn, bs, N_KV_HEADS, HEAD_DIM, WINDOW), (0, 1, 4, 2, 3))

    return (yp.reshape(bp, lp, D_MODEL), ys.reshape(bs, ls, D_MODEL),
            jnp.stack(kp_l), jnp.stack(vp_l), jnp.stack(sp_l),
            window_major(new_kv[0]), window_major(new_kv[1]), new_state)
```

```python
import functools

import jax
import jax.numpy as jnp
from jax import lax
from jax.experimental import pallas as pl
from jax.experimental.pallas import tpu as pltpu

F32 = jnp.float32
BF16 = jnp.bfloat16

D_MODEL = 1024
EPS = 1e-6
WINDOW = 128
HEAD_DIM = 64
N_HEADS = 16
N_KV_HEADS = 4
GROUP = 4
Q_DIM = 1024
KV_DIM = 256
GLA_HEADS = 4
GLA_DK = 512
GLA_DV = 1024
GLA_DK_HEAD = 128
GLA_DV_HEAD = 256
GLA_GATE_RANK = 16
GLA_GATE_NORMALIZER = 16.0
GLA_CHUNK = 64
D_FF = 2816
MASKED = -1e30
LOG2E = 1.4426950408889634

VMEM_LIMIT_BYTES = 58 * 1024 * 1024
LANES = 128
BF16_SUBLANES = 16
TOKEN_TILE = 1024
FFN_TOKEN_TILE = 1024
FFN_TILE = 256
ATTN_Q_BLOCKS = 8
GLA_GROUP = 4
SAMPLE_BATCH_TILE = 8
NEW_KEY_PAD = 16

NT_DIMS = (((1,), (1,)), ((), ()))
TN_DIMS = (((0,), (0,)), ((), ()))


def _params(*sem):
    return pltpu.CompilerParams(dimension_semantics=sem, vmem_limit_bytes=VMEM_LIMIT_BYTES)


def _rms(x, g):
    r = lax.rsqrt(jnp.mean(x * x, axis=-1, keepdims=True) + EPS)
    return x * r * g


def _silu(x):
    return x / (1.0 + jnp.exp(-x))


def _norm_matmul_kernel(n, x_ref, g_ref, *refs):
    h = _rms(x_ref[...], g_ref[...]).astype(BF16)
    for w_ref, o_ref in zip(refs[:n], refs[n:]):
        o_ref[...] = jnp.dot(h, w_ref[...], preferred_element_type=F32)


def _norm_matmul(x, g, ws, layers, tm):
    t = x.shape[0]
    lg, lw = layers
    in_specs = [pl.BlockSpec((tm, D_MODEL), lambda i: (i, 0)),
                pl.BlockSpec((None, 1, D_MODEL), lambda i: (lg, 0, 0))]
    in_specs += [pl.BlockSpec((None,) + w.shape[1:], lambda i: (lw, 0, 0)) for w in ws]
    out_specs = [pl.BlockSpec((tm, w.shape[2]), lambda i: (i, 0)) for w in ws]
    out_shape = [jax.ShapeDtypeStruct((t, w.shape[2]), F32) for w in ws]
    return pl.pallas_call(
        functools.partial(_norm_matmul_kernel, len(ws)),
        grid=(t // tm,), in_specs=in_specs, out_specs=out_specs, out_shape=out_shape,
        compiler_params=_params("parallel"), name="norm_matmul",
    )(x, g, *ws)


def _slab_rows(total, steps):
    rows = BF16_SUBLANES
    while total % rows or total // rows > steps:
        rows += BF16_SUBLANES
    return rows


def _cast_specs(sources, steps, step_of):
    in_specs, out_specs, out_shape = [], [], []
    for w_all, layer in sources:
        rows = _slab_rows(w_all.shape[1], steps)
        last = w_all.shape[1] // rows - 1
        slab = lambda *g, last=last: (jnp.minimum(step_of(*g), last), 0)
        in_specs.append(pl.BlockSpec((None, rows, w_all.shape[2]),
                                     lambda *g, slab=slab, layer=layer: (layer,) + slab(*g)))
        out_specs.append(pl.BlockSpec((rows, w_all.shape[2]), slab))
        out_shape.append(jax.ShapeDtypeStruct(w_all.shape[1:], BF16))
    return in_specs, out_specs, out_shape


def _cast_slabs(src_refs, dst_refs):
    for src, dst in zip(src_refs, dst_refs):
        dst[...] = src[...].astype(BF16)


def _ffn_kernel(final_norm, n_cast, x_ref, a_ref, wm_ref, g_ref, wi_ref, wo_ref, gf_ref, *rest):
    cast_src, rest = rest[:n_cast], rest[n_cast:]
    o_ref, cast_dst, act_ref = rest[0], rest[1:1 + n_cast], rest[-1]
    _cast_slabs(cast_src, cast_dst)
    y = x_ref[...] + jnp.dot(a_ref[...], wm_ref[...], preferred_element_type=F32)
    h = _rms(y, g_ref[...]).astype(BF16)
    for j in range(D_FF // FFN_TILE):
        gate = jnp.dot(h, wi_ref[:, j * FFN_TILE:(j + 1) * FFN_TILE], preferred_element_type=F32)
        up = jnp.dot(h, wi_ref[:, D_FF + j * FFN_TILE:D_FF + (j + 1) * FFN_TILE],
                     preferred_element_type=F32)
        act_ref[:, j * FFN_TILE:(j + 1) * FFN_TILE] = (_silu(gate) * up).astype(BF16)
    y = y + jnp.dot(act_ref[...], wo_ref[...], preferred_element_type=F32)
    if final_norm:
        y = _rms(y, gf_ref[...])
    o_ref[...] = y


def _resident(block_shape, index_map):
    return pl.BlockSpec(block_shape, index_map, pipeline_mode=pl.Buffered(1))


def _mix_out_ffn(x, a, w_mix, g, w_in, w_out, g_final, layers, final_norm, tm, cast=()):
    t = x.shape[0]
    lm, lf = layers
    steps = t // tm
    row = lambda i: (i, 0)
    cast_in, cast_out, cast_shape = _cast_specs(cast, steps, lambda i: i)
    in_specs = [pl.BlockSpec((tm, D_MODEL), row),
                pl.BlockSpec((tm, D_MODEL), row),
                _resident((None, D_MODEL, D_MODEL), lambda i: (lm, 0, 0)),
                _resident((None, 1, D_MODEL), lambda i: (lf, 0, 0)),
                _resident((D_MODEL, 2 * D_FF), lambda i: (0, 0)),
                _resident((D_FF, D_MODEL), lambda i: (0, 0)),
                _resident((1, D_MODEL), lambda i: (0, 0))]
    return pl.pallas_call(
        functools.partial(_ffn_kernel, final_norm, len(cast)),
        grid=(steps,),
        in_specs=in_specs + cast_in,
        out_specs=[pl.BlockSpec((tm, D_MODEL), row)] + cast_out,
        out_shape=[jax.ShapeDtypeStruct((t, D_MODEL), F32)] + cast_shape,
        scratch_shapes=[pltpu.VMEM((tm, D_FF), BF16)],
        compiler_params=_params("arbitrary"), name="mix_out_ffn",
    )(x, a, w_mix, g, w_in, w_out, g_final, *[w for w, _ in cast])


HALVES = LANES // HEAD_DIM
HEAD_ORDER = tuple((HALVES * (p // (HALVES * GROUP)) + p % HALVES) * GROUP + (p % (HALVES * GROUP)) // HALVES
                   for p in range(N_HEADS))


def _regroup_heads(x, axis, width):
    lead, tail = x.shape[:axis], x.shape[axis + 1:]
    x = x.reshape(lead + (N_KV_HEADS // HALVES, HALVES, GROUP, width) + tail)
    x = jnp.swapaxes(x, axis + 1, axis + 2)
    return x.reshape(lead + (N_HEADS * width,) + tail)


def _half_masks(rows):
    low = lax.broadcasted_iota(jnp.int32, (rows, LANES), 1) < HEAD_DIM
    return low, jnp.logical_not(low)


def _attn_prompt_kernel(n_cast, sink_ref, q_ref, kp_ref, kc_ref, vp_ref, vc_ref, bias_ref, *rest):
    o_ref = rest[n_cast]
    _cast_slabs(rest[:n_cast], rest[n_cast + 1:])
    step = pl.program_id(1)
    q_all = (q_ref[0] * (LOG2E * HEAD_DIM ** -0.5)).astype(BF16)
    k_all = jnp.concatenate([kp_ref[0], kc_ref[0]], axis=0).astype(BF16)
    v_f32 = jnp.concatenate([vp_ref[0], vc_ref[0]], axis=0)
    masks = _half_masks(WINDOW)
    key_masks = _half_masks(2 * WINDOW)
    zero = jnp.zeros((), BF16)
    half_ones = [jnp.where(mask, 1.0, 0.0).astype(BF16) for mask in key_masks]
    for qb in range(ATTN_Q_BLOCKS):
        rows = slice(qb * WINDOW, (qb + 1) * WINDOW)
        keys = slice(qb * WINDOW, (qb + 2) * WINDOW)
        table = 1 if qb > 0 else jnp.minimum(step, 1)
        for blk in range(KV_DIM // LANES):
            lanes = slice(blk * LANES, (blk + 1) * LANES)
            k2 = k_all[keys, lanes]
            v2 = v_f32[keys, lanes]
            v_pair = jnp.concatenate(
                [jnp.concatenate([jnp.where(key_masks[half], v2, 0.0).astype(BF16), half_ones[half]],
                                 axis=1)
                 for half in range(HALVES)], axis=0)
            for c in range(blk * GROUP, (blk + 1) * GROUP):
                qc = q_all[rows, c * LANES:(c + 1) * LANES]
                probs, sink_terms = [], []
                for half in range(HALVES):
                    p = HALVES * c + half
                    qm = jnp.where(masks[half], qc, zero)
                    s = lax.dot_general(qm, k2, NT_DIMS, preferred_element_type=F32)
                    s = s + bias_ref[table, p]
                    sink = sink_ref[p]
                    m = jnp.maximum(jnp.max(s, axis=-1, keepdims=True), sink)
                    probs.append(jnp.exp2(s - m).astype(BF16))
                    sink_terms.append(jnp.exp2(sink - m))
                ov = jnp.dot(jnp.concatenate(probs, axis=1), v_pair,
                             preferred_element_type=F32)
                denom = ov[:, LANES:] + jnp.where(
                    masks[0], *[jnp.broadcast_to(t, (WINDOW, LANES)) for t in sink_terms])
                o_ref[0, rows, c * LANES:(c + 1) * LANES] = (ov[:, :LANES] / denom).astype(BF16)


def _attn_prompt(qkv, sinks, bias, cast=()):
    b, l, _ = qkv.shape
    rows = ATTN_Q_BLOCKS * WINDOW
    steps = l // rows
    kcol, vcol = Q_DIM // KV_DIM, Q_DIM // KV_DIM + 1
    prev = lambda i: jnp.maximum(ATTN_Q_BLOCKS * i - 1, 0)
    cast_in, cast_out, cast_shape = _cast_specs(cast, b * steps, lambda b, i: b * steps + i)
    return pl.pallas_call(
        functools.partial(_attn_prompt_kernel, len(cast)),
        grid=(b, steps),
        in_specs=[pl.BlockSpec(memory_space=pltpu.SMEM),
                  pl.BlockSpec((1, rows, Q_DIM), lambda b, i: (b, i, 0)),
                  pl.BlockSpec((1, WINDOW, KV_DIM), lambda b, i: (b, prev(i), kcol)),
                  pl.BlockSpec((1, rows, KV_DIM), lambda b, i: (b, i, kcol)),
                  pl.BlockSpec((1, WINDOW, KV_DIM), lambda b, i: (b, prev(i), vcol)),
                  pl.BlockSpec((1, rows, KV_DIM), lambda b, i: (b, i, vcol)),
                  pl.BlockSpec(bias.shape, lambda b, i: (0, 0, 0, 0))] + cast_in,
        out_specs=[pl.BlockSpec((1, rows, Q_DIM), lambda b, i: (b, i, 0))] + cast_out,
        out_shape=[jax.ShapeDtypeStruct((b, l, Q_DIM), BF16)] + cast_shape,
        compiler_params=_params("arbitrary", "arbitrary"), name="attn_prompt",
    )(sinks, qkv, qkv, qkv, qkv, qkv, bias, *[w for w, _ in cast])


def _own_slab(ref, owner_of_all, layer):
    if not owner_of_all:
        return ref
    for other in range(ref.shape[0]):
        if other != layer:
            ref[other] = jnp.zeros(ref.shape[1:], ref.dtype)
    return ref.at[layer]


def _stacked_out_specs(prev, n_layers, layer, slab_shape, block, index):
    shape = jax.ShapeDtypeStruct((n_layers,) + slab_shape, F32)
    if prev is None:
        return pl.BlockSpec((n_layers,) + block, lambda b: (0,) + index(b)), shape, [], []
    spec = pl.BlockSpec((None,) + block, lambda b: (layer,) + index(b))
    return spec, shape, [pl.BlockSpec(memory_space=pl.ANY)], [prev]


def _attn_sample_kernel(layer, owner_of_all, q_ref, kn_ref, vn_ref, kt_ref, vt_ref, bo_ref, bn_ref,
                        sink_ref, *rest):
    o_ref, kto_ref, vto_ref = rest[-3:]
    kto_ref = _own_slab(kto_ref, owner_of_all, layer)
    vto_ref = _own_slab(vto_ref, owner_of_all, layer)
    bt, l, _ = q_ref.shape
    masks = _half_masks(l)
    keep = lax.broadcasted_iota(jnp.int32, (KV_DIM, WINDOW), 1) < WINDOW - l
    zero_block = jnp.zeros((l, LANES), F32)
    pad_rows = jnp.zeros((WINDOW - l, KV_DIM), F32)
    pad_keys = jnp.zeros((NEW_KEY_PAD - l, KV_DIM), F32)
    ones_old = jnp.ones((LANES, WINDOW), BF16)
    ones_new = jnp.ones((NEW_KEY_PAD, LANES), BF16)
    bias_old = bo_ref[...]
    bias_new = bn_ref[...]
    sink = sink_ref[...]

    def body(b, carry):
        q = q_ref[b] * (HEAD_DIM ** -0.5)
        pieces = []
        for p in range(N_HEADS):
            c = p // HALVES
            src = jnp.where(masks[p % HALVES], q[:, c * LANES:(c + 1) * LANES], 0.0)
            pieces.append(jnp.concatenate(
                [src, zero_block] if c // GROUP == 0 else [zero_block, src], axis=1))
        qt = jnp.concatenate(pieces, axis=0).astype(BF16)

        k_old, v_old = kt_ref[b], vt_ref[b]
        k_new, v_new = kn_ref[b], vn_ref[b]
        k_pad = jnp.concatenate([k_new, pad_keys], axis=0).astype(BF16)
        v_pad = jnp.concatenate([v_new, pad_keys], axis=0).astype(BF16)
        s_old = jnp.dot(qt, k_old.astype(BF16), preferred_element_type=F32) + bias_old
        s_new = lax.dot_general(qt, k_pad, NT_DIMS, preferred_element_type=F32) + bias_new
        m = jnp.maximum(sink, jnp.maximum(jnp.max(s_old, axis=-1, keepdims=True),
                                          jnp.max(s_new, axis=-1, keepdims=True)))
        p_old = jnp.exp(s_old - m).astype(BF16)
        p_new = jnp.exp(s_new - m).astype(BF16)
        v_old_1 = jnp.concatenate([v_old.astype(BF16), ones_old], axis=0)
        v_new_1 = jnp.concatenate([v_pad, ones_new], axis=1)
        o = lax.dot_general(p_old, v_old_1, NT_DIMS, preferred_element_type=F32)
        o = o + jnp.dot(p_new, v_new_1, preferred_element_type=F32)
        inv = 1.0 / (o[:, KV_DIM:] + jnp.exp(sink - m))
        cols = []
        for c in range(N_HEADS // HALVES):
            lanes = slice((c // GROUP) * LANES, (c // GROUP + 1) * LANES)
            parts = [o[(HALVES * c + half) * l:(HALVES * c + half + 1) * l, lanes]
                     * inv[(HALVES * c + half) * l:(HALVES * c + half + 1) * l, :]
                     for half in range(HALVES)]
            cols.append(jnp.where(masks[0], parts[0], parts[1]))
        o_ref[b] = jnp.concatenate(cols, axis=1).astype(BF16)

        for old, new, out_ref in ((k_old, k_new, kto_ref), (v_old, v_new, vto_ref)):
            shifted = pltpu.roll(old, WINDOW - l, axis=1)
            placed = jnp.concatenate([pad_rows, new], axis=0).T
            out_ref[b] = jnp.where(keep, shifted, placed)
        return carry

    lax.fori_loop(0, bt, body, 0, unroll=8)


def _attn_sample(qkv, cache_kt, cache_vt, layer, bias_old, bias_new, sink_col, prev_kv):
    b, l, _ = qkv.shape
    n_layers = cache_kt.shape[0]
    bt = min(SAMPLE_BATCH_TILE, b)
    kcol, vcol = Q_DIM // KV_DIM, Q_DIM // KV_DIM + 1
    const = lambda b: (0, 0)
    in_specs = [pl.BlockSpec((bt, l, Q_DIM), lambda b: (b, 0, 0)),
                pl.BlockSpec((bt, l, KV_DIM), lambda b: (b, 0, kcol)),
                pl.BlockSpec((bt, l, KV_DIM), lambda b: (b, 0, vcol)),
                pl.BlockSpec((None, bt, KV_DIM, WINDOW), lambda b: (layer, b, 0, 0)),
                pl.BlockSpec((None, bt, KV_DIM, WINDOW), lambda b: (layer, b, 0, 0)),
                pl.BlockSpec(bias_old.shape, const),
                pl.BlockSpec(bias_new.shape, const),
                pl.BlockSpec(sink_col.shape, const)]
    inputs = [qkv, qkv, qkv, cache_kt, cache_vt, bias_old, bias_new, sink_col]
    out_specs = [pl.BlockSpec((bt, l, Q_DIM), lambda b: (b, 0, 0))]
    out_shape = [jax.ShapeDtypeStruct((b, l, Q_DIM), BF16)]
    aliases = {}
    for prev in (prev_kv if prev_kv is not None else (None, None)):
        spec, shape, extra_specs, extra_inputs = _stacked_out_specs(
            prev, n_layers, layer, (b, KV_DIM, WINDOW), (bt, KV_DIM, WINDOW), lambda b: (b, 0, 0))
        if extra_inputs:
            aliases[len(inputs)] = len(out_specs)
        in_specs += extra_specs
        inputs += extra_inputs
        out_specs.append(spec)
        out_shape.append(shape)
    return pl.pallas_call(
        functools.partial(_attn_sample_kernel, layer, prev_kv is None),
        grid=(b // bt,), in_specs=in_specs, out_specs=out_specs, out_shape=out_shape,
        input_output_aliases=aliases,
        compiler_params=_params("parallel"), name="attn_sample",
    )(*inputs)


def _attn_bias_tables(l_sample):
    slopes = jnp.exp2(-8.0 * jnp.arange(1, N_HEADS + 1, dtype=F32) / N_HEADS)
    slopes = slopes[jnp.array(HEAD_ORDER)][:, None, None]
    t = jnp.arange(WINDOW, dtype=jnp.int32)[:, None]
    s = jnp.arange(2 * WINDOW, dtype=jnp.int32)[None, :]
    dist = t + WINDOW - s
    valid = (dist >= 0) & (dist <= WINDOW)
    later = jnp.where(valid[None], -slopes * dist.astype(F32)[None], MASKED)
    first = jnp.where((s >= WINDOW)[None], later, MASKED)
    prompt = jnp.stack([first, later]) * LOG2E
    tq = jnp.arange(l_sample, dtype=jnp.int32)[:, None]
    sk = jnp.arange(WINDOW + NEW_KEY_PAD, dtype=jnp.int32)[None, :]
    dist = tq + WINDOW - sk
    valid = (dist >= 0) & (dist <= WINDOW) & (sk < WINDOW + l_sample)
    sample = jnp.where(valid[None], -slopes * dist.astype(F32)[None], MASKED)
    sample = sample.reshape(N_HEADS * l_sample, WINDOW + NEW_KEY_PAD)
    return prompt, sample[:, :WINDOW], sample[:, WINDOW:]


def _log_sigmoid(x):
    return jnp.minimum(x, 0.0) - jnp.log(1.0 + jnp.exp(-jnp.abs(x)))


def _chunk_cumsum(x, c):
    pos = jnp.bitwise_and(lax.broadcasted_iota(jnp.int32, x.shape, 0), c - 1)
    shift = 1
    while shift < c:
        x = x + jnp.where(pos >= shift, pltpu.roll(x, shift, axis=0), 0.0)
        shift *= 2
    return x


def _gla_proj_kernel(chunk, x_ref, g_ref, wm_ref, wgd_ref, wg_ref, bg_ref, qk_ref, vg_ref, b_ref):
    h = _rms(x_ref[...], g_ref[...]).astype(BF16)
    gd = lax.dot_general(h, wgd_ref[...], NT_DIMS, preferred_element_type=F32)
    pre = jnp.dot(gd.astype(BF16), wg_ref[...].astype(BF16), preferred_element_type=F32) + bg_ref[...]
    b_ref[...] = _chunk_cumsum(_log_sigmoid(pre) / GLA_GATE_NORMALIZER, chunk)
    n_qk, n_qkv = 2 * GLA_DK, 2 * GLA_DK + GLA_DV
    proj = lambda rows: lax.dot_general(h, wm_ref[rows, :], NT_DIMS, preferred_element_type=F32)
    qk_ref[...] = proj(slice(0, n_qk))
    vg_ref[:, :GLA_DV] = proj(slice(n_qk, n_qkv)).astype(BF16)
    vg_ref[:, GLA_DV:] = _silu(proj(slice(n_qkv, n_qkv + GLA_DV))).astype(BF16)


def _gla_proj(x, g, w_in_t, w_gate, b_gate, layers, chunk, tm):
    t = x.shape[0]
    lg, lw = layers
    n_main = 2 * GLA_DK + 2 * GLA_DV
    return pl.pallas_call(
        functools.partial(_gla_proj_kernel, chunk),
        grid=(t // tm,),
        in_specs=[pl.BlockSpec((tm, D_MODEL), lambda i: (i, 0)),
                  pl.BlockSpec((None, 1, D_MODEL), lambda i: (lg, 0, 0)),
                  pl.BlockSpec((None, n_main, D_MODEL), lambda i: (lw, 0, 0)),
                  pl.BlockSpec((None, GLA_GATE_RANK, D_MODEL),
                               lambda i: (lw, n_main // GLA_GATE_RANK, 0)),
                  pl.BlockSpec((None, GLA_GATE_RANK, GLA_DK), lambda i: (lw, 0, 0)),
                  pl.BlockSpec((None, 1, GLA_DK), lambda i: (lw, 0, 0))],
        out_specs=[pl.BlockSpec((tm, 2 * GLA_DK), lambda i: (i, 0)),
                   pl.BlockSpec((tm, 2 * GLA_DV), lambda i: (i, 0)),
                   pl.BlockSpec((tm, GLA_DK), lambda i: (i, 0))],
        out_shape=[jax.ShapeDtypeStruct((t, 2 * GLA_DK), F32),
                   jax.ShapeDtypeStruct((t, 2 * GLA_DV), BF16),
                   jax.ShapeDtypeStruct((t, GLA_DK), F32)],
        compiler_params=_params("parallel"), name="gla_proj",
    )(x, g, w_in_t, w_in_t, w_gate, b_gate)


def _gla_gate_out(o, gate, gn):
    return (_rms(o, gn) * gate).astype(BF16)


def _gla_prompt_kernel(q_ref, k_ref, v_ref, r_ref, b_ref, gn_ref, o_ref, s_ref, st_ref):
    c = GLA_CHUNK
    rows = GLA_GROUP * c
    n_groups = q_ref.shape[1] // rows
    row = lax.broadcasted_iota(jnp.int32, (rows, rows), 0)
    col = lax.broadcasted_iota(jnp.int32, (rows, rows), 1)
    causal = (row >= col) & (col >= row - jnp.bitwise_and(row, c - 1))
    gn = gn_ref[...]
    st_ref[...] = jnp.zeros_like(st_ref)

    def body(g, carry):
        sl = pl.ds(pl.multiple_of(g * rows, rows), rows)
        q = q_ref[0, sl, :] * (GLA_DK_HEAD ** -0.5)
        k = k_ref[0, sl, :]
        v_bf = v_ref[0, sl, :]
        b = b_ref[0, sl, :]
        per_chunk = lambda r0: jnp.concatenate(
            [jnp.broadcast_to(b[j * c + r0:j * c + r0 + 1, :], (c, GLA_DK_HEAD))
             for j in range(GLA_GROUP)], axis=0)
        b_mid = per_chunk(c // 2)
        b_end = per_chunk(c - 1)
        qa = (q * jnp.exp(b - b_mid)).astype(BF16)
        ka = (k * jnp.exp(b_mid - b)).astype(BF16)
        a = lax.dot_general(qa, ka, NT_DIMS, preferred_element_type=F32)
        a = jnp.where(causal, a, 0.0).astype(BF16)
        o_intra = jnp.dot(a, v_bf, preferred_element_type=F32)
        qb = (q * jnp.exp(b)).astype(BF16)
        k_end = (k * jnp.exp(b_end - b)).astype(BF16)
        st = st_ref[...]
        o_inter = []
        for j in range(GLA_GROUP):
            rs = slice(j * c, (j + 1) * c)
            o_inter.append(lax.dot_general(qb[rs], st.astype(BF16), NT_DIMS,
                                           preferred_element_type=F32))
            upd = lax.dot_general(v_bf[rs], k_end[rs], TN_DIMS, preferred_element_type=F32)
            st = st * jnp.exp(b[(j + 1) * c - 1:(j + 1) * c, :]) + upd
        st_ref[...] = st
        o = o_intra + jnp.concatenate(o_inter, axis=0)
        o_ref[0, sl, :] = _gla_gate_out(o, r_ref[0, sl, :], gn)
        return carry

    lax.fori_loop(0, n_groups, body, 0, unroll=8)
    s_ref[0, 0] = st_ref[...].T


def _gla_prompt(qk, vg, decay, g_norm, layer):
    b, l, _ = qk.shape
    kcol = GLA_DK // GLA_DK_HEAD
    gcol = GLA_DV // GLA_DV_HEAD
    return pl.pallas_call(
        _gla_prompt_kernel,
        grid=(b, GLA_HEADS),
        in_specs=[pl.BlockSpec((1, l, GLA_DK_HEAD), lambda b, h: (b, 0, h)),
                  pl.BlockSpec((1, l, GLA_DK_HEAD), lambda b, h: (b, 0, kcol + h)),
                  pl.BlockSpec((1, l, GLA_DV_HEAD), lambda b, h: (b, 0, h)),
                  pl.BlockSpec((1, l, GLA_DV_HEAD), lambda b, h: (b, 0, gcol + h)),
                  pl.BlockSpec((1, l, GLA_DK_HEAD), lambda b, h: (b, 0, h)),
                  pl.BlockSpec((None, 1, GLA_DV_HEAD), lambda b, h: (layer, 0, 0))],
        out_specs=[pl.BlockSpec((1, l, GLA_DV_HEAD), lambda b, h: (b, 0, h)),
                   pl.BlockSpec((1, 1, GLA_DK_HEAD, GLA_DV_HEAD), lambda b, h: (b, h, 0, 0))],
        out_shape=[jax.ShapeDtypeStruct((b, l, GLA_DV), BF16),
                   jax.ShapeDtypeStruct((b, GLA_HEADS, GLA_DK_HEAD, GLA_DV_HEAD), F32)],
        scratch_shapes=[pltpu.VMEM((GLA_DV_HEAD, GLA_DK_HEAD), F32)],
        compiler_params=_params("parallel", "parallel"), name="gla_prompt",
    )(qk, qk, vg, vg, decay, g_norm)


def _gla_sample_kernel(layer, owner_of_all, q_ref, k_ref, v_ref, r_ref, b_ref, s0_ref, gn_ref, *rest):
    o_ref, s_ref = rest[-2:]
    s_ref = _own_slab(s_ref, owner_of_all, layer)
    bt, c, _ = q_ref.shape
    row = lax.broadcasted_iota(jnp.int32, (c, c), 0)
    col = lax.broadcasted_iota(jnp.int32, (c, c), 1)
    causal = row >= col
    gn = gn_ref[...]

    def body(i, carry):
        q = q_ref[i] * (GLA_DK_HEAD ** -0.5)
        k = k_ref[i]
        v = v_ref[i].astype(F32)
        r = r_ref[i]
        b = b_ref[i]
        b_mid = b[c // 2:c // 2 + 1, :]
        b_end = b[c - 1:c, :]
        qa = q * jnp.exp(b - b_mid)
        ka = k * jnp.exp(b_mid - b)
        qb = (q * jnp.exp(b)).astype(BF16)
        k_end_t = (k * jnp.exp(b_end - b)).T
        dec_col = jnp.broadcast_to(jnp.exp(b_end), (c, GLA_DK)).T[:, 0:1]
        for h in range(GLA_HEADS):
            ks = slice(h * GLA_DK_HEAD, (h + 1) * GLA_DK_HEAD)
            vs = slice(h * GLA_DV_HEAD, (h + 1) * GLA_DV_HEAD)
            s0 = s0_ref[i, h]
            a = lax.dot_general(qa[:, ks], ka[:, ks], NT_DIMS, preferred_element_type=F32)
            a = jnp.where(causal, a, 0.0)
            o = jnp.dot(a, v[:, vs], preferred_element_type=F32)
            o = o + jnp.dot(qb[:, ks], s0.astype(BF16), preferred_element_type=F32)
            upd = jnp.dot(k_end_t[ks, :], v[:, vs], preferred_element_type=F32)
            s_ref[i, h] = s0 * dec_col[ks, :] + upd
            o_ref[i, :, vs] = _gla_gate_out(o, r[:, vs], gn)
        return carry

    lax.fori_loop(0, bt, body, 0, unroll=2)


def _gla_sample(qk, vg, decay, state, g_norm, layer, prev_state):
    b, l, _ = qk.shape
    bt = min(SAMPLE_BATCH_TILE, b)
    state_block = (bt, GLA_HEADS, GLA_DK_HEAD, GLA_DV_HEAD)
    in_specs = [pl.BlockSpec((bt, l, GLA_DK), lambda b: (b, 0, 0)),
                pl.BlockSpec((bt, l, GLA_DK), lambda b: (b, 0, 1)),
                pl.BlockSpec((bt, l, GLA_DV), lambda b: (b, 0, 0)),
                pl.BlockSpec((bt, l, GLA_DV), lambda b: (b, 0, 1)),
                pl.BlockSpec((bt, l, GLA_DK), lambda b: (b, 0, 0)),
                pl.BlockSpec((None,) + state_block, lambda b: (layer, b, 0, 0, 0)),
                pl.BlockSpec((None, 1, GLA_DV_HEAD), lambda b: (layer, 0, 0))]
    inputs = [qk, qk, vg, vg, decay, state, g_norm]
    spec, shape, extra_specs, extra_inputs = _stacked_out_specs(
        prev_state, state.shape[0], layer, state.shape[1:], state_block, lambda b: (b, 0, 0, 0))
    return pl.pallas_call(
        functools.partial(_gla_sample_kernel, layer, prev_state is None),
        grid=(b // bt,),
        in_specs=in_specs + extra_specs,
        out_specs=[pl.BlockSpec((bt, l, GLA_DV), lambda b: (b, 0, 0)), spec],
        out_shape=[jax.ShapeDtypeStruct((b, l, GLA_DV), BF16), shape],
        input_output_aliases={len(inputs): 1} if extra_inputs else {},
        compiler_params=_params("parallel"), name="gla_sample",
    )(*inputs, *extra_inputs)


def kernel(x_prompt, x_sample, cache_k, cache_v, state_gla, norm_mix, norm_ffn, norm_final,
           w_attn_in, w_attn_out, attn_sinks, w_gla_in, w_gla_gate_up, b_gla_gate, gla_out_norm,
           w_gla_out, w_ffn_in, w_ffn_out):
    bp, lp, _ = x_prompt.shape
    bs, ls, _ = x_sample.shape
    depth = norm_mix.shape[0]
    n_attn = cache_k.shape[0]
    yp = x_prompt.reshape(bp * lp, D_MODEL)
    ys = x_sample.reshape(bs * ls, D_MODEL)
    tp = min(TOKEN_TILE, bp * lp)
    ts = min(TOKEN_TILE, bs * ls)
    bias_p, bias_old, bias_new = _attn_bias_tables(ls)

    g_mix = norm_mix.reshape(depth, 1, D_MODEL)
    g_ffn = norm_ffn.reshape(depth, 1, D_MODEL)
    g_fin = norm_final.reshape(1, D_MODEL)
    w_attn_bf = w_attn_in.astype(BF16)
    w_attn_in_bf = jnp.concatenate(
        [_regroup_heads(w_attn_bf[:, :, :Q_DIM], 2, HEAD_DIM), w_attn_bf[:, :, Q_DIM:]], axis=2)
    w_attn_out_bf = _regroup_heads(w_attn_out.astype(BF16), 1, HEAD_DIM)
    sinks = _regroup_heads(attn_sinks, 1, 1)
    w_gla_in_t = jnp.transpose(w_gla_in, (0, 2, 1)).astype(BF16)
    w_fi = w_fo = w_gla_out_bf = None
    b_gate = b_gla_gate.reshape(-1, 1, GLA_DK)
    g_gla = gla_out_norm.reshape(-1, 1, GLA_DV_HEAD)
    cache_kt = jnp.transpose(cache_k, (0, 1, 3, 4, 2)).reshape(n_attn, bs, KV_DIM, WINDOW)
    cache_vt = jnp.transpose(cache_v, (0, 1, 3, 4, 2)).reshape(n_attn, bs, KV_DIM, WINDOW)

    kp_l, vp_l, sp_l = [], [], []
    new_kv, new_state = None, None
    for i in range(depth):
        j = i // 2
        if i % 2 == 0:
            w_mix, lm = w_attn_out_bf, j
            (qkv_p,) = _norm_matmul(yp, g_mix, [w_attn_in_bf], (i, j), tp)
            (qkv_s,) = _norm_matmul(ys, g_mix, [w_attn_in_bf], (i, j), ts)
            qkv_p = qkv_p.reshape(bp, lp, Q_DIM + 2 * KV_DIM)
            qkv_s = qkv_s.reshape(bs, ls, Q_DIM + 2 * KV_DIM)
            cast = [(w_gla_out, j)] + ([(w_ffn_in, 0), (w_ffn_out, 0)] if i == 0 else [])
            op, w_gla_out_bf, *w_ffn0 = _attn_prompt(qkv_p, sinks[j] * LOG2E, bias_p, cast)
            if w_ffn0:
                w_fi, w_fo = w_ffn0
            sink_col = jnp.repeat(sinks[j], ls).reshape(N_HEADS * ls, 1)
            os_, *new_kv = _attn_sample(qkv_s, cache_kt, cache_vt, j, bias_old, bias_new, sink_col,
                                        new_kv)
            kv_shape = (bp, WINDOW, N_KV_HEADS, HEAD_DIM)
            kp_l.append(qkv_p[:, lp - WINDOW:, Q_DIM:Q_DIM + KV_DIM].reshape(kv_shape))
            vp_l.append(qkv_p[:, lp - WINDOW:, Q_DIM + KV_DIM:].reshape(kv_shape))
        else:
            w_mix, lm = w_gla_out_bf[None], 0
            gla_p = _gla_proj(yp, g_mix, w_gla_in_t, w_gla_gate_up, b_gate, (i, j), GLA_CHUNK, tp)
            gla_s = _gla_proj(ys, g_mix, w_gla_in_t, w_gla_gate_up, b_gate, (i, j), ls, ts)
            op, sp = _gla_prompt(*[t.reshape(bp, lp, -1) for t in gla_p], g_gla, j)
            os_, new_state = _gla_sample(*[t.reshape(bs, ls, -1) for t in gla_s], state_gla, g_gla, j,
                                         new_state)
            sp_l.append(sp)
        last = i == depth - 1
        yp, *w_next = _mix_out_ffn(yp, op.reshape(bp * lp, D_MODEL), w_mix, g_ffn, w_fi, w_fo,
                                   g_fin, (lm, i), last, min(FFN_TOKEN_TILE, bp * lp),
                                   [] if last else [(w_ffn_in, i + 1), (w_ffn_out, i + 1)])
        (ys,) = _mix_out_ffn(ys, os_.reshape(bs * ls, D_MODEL), w_mix, g_ffn, w_fi, w_fo,
                             g_fin, (lm, i), last, min(FFN_TOKEN_TILE, bs * ls))
        if w_next:
            w_fi, w_fo = w_next

    def window_major(xt):
        return jnp.transpose(xt.reshape(n_attn, bs, N_KV_HEADS, HEAD_DIM, WINDOW), (0, 1, 4, 2, 3))

    return (yp.reshape(bp, lp, D_MODEL), ys.reshape(bs, ls, D_MODEL),
            jnp.stack(kp_l), jnp.stack(vp_l), jnp.stack(sp_l),
            window_major(new_kv[0]), window_major(new_kv[1]), new_state)
```

```python
import functools

import jax
import jax.numpy as jnp
from jax import lax
from jax.experimental import pallas as pl
from jax.experimental.pallas import tpu as pltpu

F32 = jnp.float32
BF16 = jnp.bfloat16

D_MODEL = 1024
EPS = 1e-6
WINDOW = 128
HEAD_DIM = 64
N_HEADS = 16
N_KV_HEADS = 4
GROUP = 4
Q_DIM = 1024
KV_DIM = 256
GLA_HEADS = 4
GLA_DK = 512
GLA_DV = 1024
GLA_DK_HEAD = 128
GLA_DV_HEAD = 256
GLA_GATE_RANK = 16
GLA_GATE_NORMALIZER = 16.0
GLA_CHUNK = 64
D_FF = 2816
MASKED = -1e30
LOG2E = 1.4426950408889634

VMEM_LIMIT_BYTES = 58 * 1024 * 1024
LANES = 128
BF16_SUBLANES = 16
TOKEN_TILE = 1024
FFN_TOKEN_TILE = 1024
FFN_TILE = 256
ATTN_Q_BLOCKS = 8
GLA_GROUP = 4
GLA_HEADS_PER_STEP = 2
SAMPLE_BATCH_TILE = 8
NEW_KEY_PAD = 16

NT_DIMS = (((1,), (1,)), ((), ()))
TN_DIMS = (((0,), (0,)), ((), ()))


def _params(*sem):
    return pltpu.CompilerParams(dimension_semantics=sem, vmem_limit_bytes=VMEM_LIMIT_BYTES)


def _rms(x, g):
    r = lax.rsqrt(jnp.mean(x * x, axis=-1, keepdims=True) + EPS)
    return x * r * g


def _silu(x):
    return x / (1.0 + jnp.exp(-x))


def _norm_matmul_kernel(n, x_ref, g_ref, *refs):
    h = _rms(x_ref[...], g_ref[...]).astype(BF16)
    for w_ref, o_ref in zip(refs[:n], refs[n:]):
        o_ref[...] = jnp.dot(h, w_ref[...], preferred_element_type=F32)


def _norm_matmul(x, g, ws, layers, tm):
    t = x.shape[0]
    lg, lw = layers
    in_specs = [pl.BlockSpec((tm, D_MODEL), lambda i: (i, 0)),
                pl.BlockSpec((None, 1, D_MODEL), lambda i: (lg, 0, 0))]
    in_specs += [pl.BlockSpec((None,) + w.shape[1:], lambda i: (lw, 0, 0)) for w in ws]
    out_specs = [pl.BlockSpec((tm, w.shape[2]), lambda i: (i, 0)) for w in ws]
    out_shape = [jax.ShapeDtypeStruct((t, w.shape[2]), F32) for w in ws]
    return pl.pallas_call(
        functools.partial(_norm_matmul_kernel, len(ws)),
        grid=(t // tm,), in_specs=in_specs, out_specs=out_specs, out_shape=out_shape,
        compiler_params=_params("parallel"), name="norm_matmul",
    )(x, g, *ws)


def _slab_rows(total, steps):
    rows = BF16_SUBLANES
    while total % rows or total // rows > steps:
        rows += BF16_SUBLANES
    return rows


def _cast_specs(sources, steps, step_of):
    in_specs, out_specs, out_shape = [], [], []
    for w_all, layer in sources:
        rows = _slab_rows(w_all.shape[1], steps)
        last = w_all.shape[1] // rows - 1
        slab = lambda *g, last=last: (jnp.minimum(step_of(*g), last), 0)
        in_specs.append(pl.BlockSpec((None, rows, w_all.shape[2]),
                                     lambda *g, slab=slab, layer=layer: (layer,) + slab(*g)))
        out_specs.append(pl.BlockSpec((rows, w_all.shape[2]), slab))
        out_shape.append(jax.ShapeDtypeStruct(w_all.shape[1:], BF16))
    return in_specs, out_specs, out_shape


def _cast_slabs(src_refs, dst_refs):
    for src, dst in zip(src_refs, dst_refs):
        dst[...] = src[...].astype(BF16)


def _ffn_kernel(final_norm, n_cast, x_ref, a_ref, wm_ref, g_ref, wi_ref, wo_ref, gf_ref, *rest):
    cast_src, rest = rest[:n_cast], rest[n_cast:]
    o_ref, cast_dst, act_ref = rest[0], rest[1:1 + n_cast], rest[-1]
    _cast_slabs(cast_src, cast_dst)
    y = x_ref[...] + jnp.dot(a_ref[...], wm_ref[...], preferred_element_type=F32)
    h = _rms(y, g_ref[...]).astype(BF16)
    for j in range(D_FF // FFN_TILE):
        gate = jnp.dot(h, wi_ref[:, j * FFN_TILE:(j + 1) * FFN_TILE], preferred_element_type=F32)
        up = jnp.dot(h, wi_ref[:, D_FF + j * FFN_TILE:D_FF + (j + 1) * FFN_TILE],
                     preferred_element_type=F32)
        act_ref[:, j * FFN_TILE:(j + 1) * FFN_TILE] = (_silu(gate) * up).astype(BF16)
    y = y + jnp.dot(act_ref[...], wo_ref[...], preferred_element_type=F32)
    if final_norm:
        y = _rms(y, gf_ref[...])
    o_ref[...] = y


def _resident(block_shape, index_map):
    return pl.BlockSpec(block_shape, index_map, pipeline_mode=pl.Buffered(1))


def _mix_out_ffn(x, a, w_mix, g, w_in, w_out, g_final, layers, final_norm, tm, cast=()):
    t = x.shape[0]
    lm, lf = layers
    steps = t // tm
    row = lambda i: (i, 0)
    cast_in, cast_out, cast_shape = _cast_specs(cast, steps, lambda i: i)
    in_specs = [pl.BlockSpec((tm, D_MODEL), row),
                pl.BlockSpec((tm, D_MODEL), row),
                _resident((None, D_MODEL, D_MODEL), lambda i: (lm, 0, 0)),
                _resident((None, 1, D_MODEL), lambda i: (lf, 0, 0)),
                _resident((D_MODEL, 2 * D_FF), lambda i: (0, 0)),
                _resident((D_FF, D_MODEL), lambda i: (0, 0)),
                _resident((1, D_MODEL), lambda i: (0, 0))]
    return pl.pallas_call(
        functools.partial(_ffn_kernel, final_norm, len(cast)),
        grid=(steps,),
        in_specs=in_specs + cast_in,
        out_specs=[pl.BlockSpec((tm, D_MODEL), row)] + cast_out,
        out_shape=[jax.ShapeDtypeStruct((t, D_MODEL), F32)] + cast_shape,
        scratch_shapes=[pltpu.VMEM((tm, D_FF), BF16)],
        compiler_params=_params("arbitrary"), name="mix_out_ffn",
    )(x, a, w_mix, g, w_in, w_out, g_final, *[w for w, _ in cast])


HALVES = LANES // HEAD_DIM
HEAD_ORDER = tuple((HALVES * (p // (HALVES * GROUP)) + p % HALVES) * GROUP + (p % (HALVES * GROUP)) // HALVES
                   for p in range(N_HEADS))


def _regroup_heads(x, axis, width):
    lead, tail = x.shape[:axis], x.shape[axis + 1:]
    x = x.reshape(lead + (N_KV_HEADS // HALVES, HALVES, GROUP, width) + tail)
    x = jnp.swapaxes(x, axis + 1, axis + 2)
    return x.reshape(lead + (N_HEADS * width,) + tail)


def _half_masks(rows):
    low = lax.broadcasted_iota(jnp.int32, (rows, LANES), 1) < HEAD_DIM
    return low, jnp.logical_not(low)


def _attn_prompt_kernel(n_cast, sink_ref, q_ref, kp_ref, kc_ref, vp_ref, vc_ref, bias_ref, *rest):
    o_ref = rest[n_cast]
    _cast_slabs(rest[:n_cast], rest[n_cast + 1:])
    step = pl.program_id(1)
    q_all = (q_ref[0] * (LOG2E * HEAD_DIM ** -0.5)).astype(BF16)
    k_all = jnp.concatenate([kp_ref[0], kc_ref[0]], axis=0).astype(BF16)
    v_f32 = jnp.concatenate([vp_ref[0], vc_ref[0]], axis=0)
    masks = _half_masks(WINDOW)
    key_masks = _half_masks(2 * WINDOW)
    zero = jnp.zeros((), BF16)
    half_ones = [jnp.where(mask, 1.0, 0.0).astype(BF16) for mask in key_masks]
    for qb in range(ATTN_Q_BLOCKS):
        rows = slice(qb * WINDOW, (qb + 1) * WINDOW)
        keys = slice(qb * WINDOW, (qb + 2) * WINDOW)
        table = 1 if qb > 0 else jnp.minimum(step, 1)
        for blk in range(KV_DIM // LANES):
            lanes = slice(blk * LANES, (blk + 1) * LANES)
            k2 = k_all[keys, lanes]
            v2 = v_f32[keys, lanes]
            v_pair = jnp.concatenate(
                [jnp.concatenate([jnp.where(key_masks[half], v2, 0.0).astype(BF16), half_ones[half]],
                                 axis=1)
                 for half in range(HALVES)], axis=0)
            for c in range(blk * GROUP, (blk + 1) * GROUP):
                qc = q_all[rows, c * LANES:(c + 1) * LANES]
                probs, sink_terms = [], []
                for half in range(HALVES):
                    p = HALVES * c + half
                    qm = jnp.where(masks[half], qc, zero)
                    s = lax.dot_general(qm, k2, NT_DIMS, preferred_element_type=F32)
                    s = s + bias_ref[table, p]
                    sink = sink_ref[p]
                    m = jnp.maximum(jnp.max(s, axis=-1, keepdims=True), sink)
                    probs.append(jnp.exp2(s - m).astype(BF16))
                    sink_terms.append(jnp.exp2(sink - m))
                ov = jnp.dot(jnp.concatenate(probs, axis=1), v_pair,
                             preferred_element_type=F32)
                denom = ov[:, LANES:] + jnp.where(
                    masks[0], *[jnp.broadcast_to(t, (WINDOW, LANES)) for t in sink_terms])
                o_ref[0, rows, c * LANES:(c + 1) * LANES] = (ov[:, :LANES] / denom).astype(BF16)


def _attn_prompt(qkv, sinks, bias, cast=()):
    b, l, _ = qkv.shape
    rows = ATTN_Q_BLOCKS * WINDOW
    steps = l // rows
    kcol, vcol = Q_DIM // KV_DIM, Q_DIM // KV_DIM + 1
    prev = lambda i: jnp.maximum(ATTN_Q_BLOCKS * i - 1, 0)
    cast_in, cast_out, cast_shape = _cast_specs(cast, b * steps, lambda b, i: b * steps + i)
    return pl.pallas_call(
        functools.partial(_attn_prompt_kernel, len(cast)),
        grid=(b, steps),
        in_specs=[pl.BlockSpec(memory_space=pltpu.SMEM),
                  pl.BlockSpec((1, rows, Q_DIM), lambda b, i: (b, i, 0)),
                  pl.BlockSpec((1, WINDOW, KV_DIM), lambda b, i: (b, prev(i), kcol)),
                  pl.BlockSpec((1, rows, KV_DIM), lambda b, i: (b, i, kcol)),
                  pl.BlockSpec((1, WINDOW, KV_DIM), lambda b, i: (b, prev(i), vcol)),
                  pl.BlockSpec((1, rows, KV_DIM), lambda b, i: (b, i, vcol)),
                  pl.BlockSpec(bias.shape, lambda b, i: (0, 0, 0, 0))] + cast_in,
        out_specs=[pl.BlockSpec((1, rows, Q_DIM), lambda b, i: (b, i, 0))] + cast_out,
        out_shape=[jax.ShapeDtypeStruct((b, l, Q_DIM), BF16)] + cast_shape,
        compiler_params=_params("arbitrary", "arbitrary"), name="attn_prompt",
    )(sinks, qkv, qkv, qkv, qkv, qkv, bias, *[w for w, _ in cast])


def _own_slab(ref, owner_of_all, layer):
    if not owner_of_all:
        return ref
    for other in range(ref.shape[0]):
        if other != layer:
            ref[other] = jnp.zeros(ref.shape[1:], ref.dtype)
    return ref.at[layer]


def _stacked_out_specs(prev, n_layers, layer, slab_shape, block, index):
    shape = jax.ShapeDtypeStruct((n_layers,) + slab_shape, F32)
    if prev is None:
        return pl.BlockSpec((n_layers,) + block, lambda b: (0,) + index(b)), shape, [], []
    spec = pl.BlockSpec((None,) + block, lambda b: (layer,) + index(b))
    return spec, shape, [pl.BlockSpec(memory_space=pl.ANY)], [prev]


def _attn_sample_kernel(layer, owner_of_all, q_ref, kn_ref, vn_ref, kt_ref, vt_ref, bo_ref, bn_ref,
                        sink_ref, *rest):
    o_ref, kto_ref, vto_ref = rest[-3:]
    kto_ref = _own_slab(kto_ref, owner_of_all, layer)
    vto_ref = _own_slab(vto_ref, owner_of_all, layer)
    bt, l, _ = q_ref.shape
    masks = _half_masks(l)
    keep = lax.broadcasted_iota(jnp.int32, (KV_DIM, WINDOW), 1) < WINDOW - l
    zero_block = jnp.zeros((l, LANES), F32)
    pad_rows = jnp.zeros((WINDOW - l, KV_DIM), F32)
    pad_keys = jnp.zeros((NEW_KEY_PAD - l, KV_DIM), F32)
    ones_old = jnp.ones((LANES, WINDOW), BF16)
    ones_new = jnp.ones((NEW_KEY_PAD, LANES), BF16)
    bias_old = bo_ref[...]
    bias_new = bn_ref[...]
    sink = sink_ref[...]

    def body(b, carry):
        q = q_ref[b] * (HEAD_DIM ** -0.5)
        pieces = []
        for p in range(N_HEADS):
            c = p // HALVES
            src = jnp.where(masks[p % HALVES], q[:, c * LANES:(c + 1) * LANES], 0.0)
            pieces.append(jnp.concatenate(
                [src, zero_block] if c // GROUP == 0 else [zero_block, src], axis=1))
        qt = jnp.concatenate(pieces, axis=0).astype(BF16)

        k_old, v_old = kt_ref[b], vt_ref[b]
        k_new, v_new = kn_ref[b], vn_ref[b]
        k_pad = jnp.concatenate([k_new, pad_keys], axis=0).astype(BF16)
        v_pad = jnp.concatenate([v_new, pad_keys], axis=0).astype(BF16)
        s_old = jnp.dot(qt, k_old.astype(BF16), preferred_element_type=F32) + bias_old
        s_new = lax.dot_general(qt, k_pad, NT_DIMS, preferred_element_type=F32) + bias_new
        m = jnp.maximum(sink, jnp.maximum(jnp.max(s_old, axis=-1, keepdims=True),
                                          jnp.max(s_new, axis=-1, keepdims=True)))
        p_old = jnp.exp(s_old - m).astype(BF16)
        p_new = jnp.exp(s_new - m).astype(BF16)
        v_old_1 = jnp.concatenate([v_old.astype(BF16), ones_old], axis=0)
        v_new_1 = jnp.concatenate([v_pad, ones_new], axis=1)
        o = lax.dot_general(p_old, v_old_1, NT_DIMS, preferred_element_type=F32)
        o = o + jnp.dot(p_new, v_new_1, preferred_element_type=F32)
        inv = 1.0 / (o[:, KV_DIM:] + jnp.exp(sink - m))
        cols = []
        for c in range(N_HEADS // HALVES):
            lanes = slice((c // GROUP) * LANES, (c // GROUP + 1) * LANES)
            parts = [o[(HALVES * c + half) * l:(HALVES * c + half + 1) * l, lanes]
                     * inv[(HALVES * c + half) * l:(HALVES * c + half + 1) * l, :]
                     for half in range(HALVES)]
            cols.append(jnp.where(masks[0], parts[0], parts[1]))
        o_ref[b] = jnp.concatenate(cols, axis=1).astype(BF16)

        for old, new, out_ref in ((k_old, k_new, kto_ref), (v_old, v_new, vto_ref)):
            shifted = pltpu.roll(old, WINDOW - l, axis=1)
            placed = jnp.concatenate([pad_rows, new], axis=0).T
            out_ref[b] = jnp.where(keep, shifted, placed)
        return carry

    lax.fori_loop(0, bt, body, 0, unroll=8)


def _attn_sample(qkv, cache_kt, cache_vt, layer, bias_old, bias_new, sink_col, prev_kv):
    b, l, _ = qkv.shape
    n_layers = cache_kt.shape[0]
    bt = min(SAMPLE_BATCH_TILE, b)
    kcol, vcol = Q_DIM // KV_DIM, Q_DIM // KV_DIM + 1
    const = lambda b: (0, 0)
    in_specs = [pl.BlockSpec((bt, l, Q_DIM), lambda b: (b, 0, 0)),
                pl.BlockSpec((bt, l, KV_DIM), lambda b: (b, 0, kcol)),
                pl.BlockSpec((bt, l, KV_DIM), lambda b: (b, 0, vcol)),
                pl.BlockSpec((None, bt, KV_DIM, WINDOW), lambda b: (layer, b, 0, 0)),
                pl.BlockSpec((None, bt, KV_DIM, WINDOW), lambda b: (layer, b, 0, 0)),
                pl.BlockSpec(bias_old.shape, const),
                pl.BlockSpec(bias_new.shape, const),
                pl.BlockSpec(sink_col.shape, const)]
    inputs = [qkv, qkv, qkv, cache_kt, cache_vt, bias_old, bias_new, sink_col]
    out_specs = [pl.BlockSpec((bt, l, Q_DIM), lambda b: (b, 0, 0))]
    out_shape = [jax.ShapeDtypeStruct((b, l, Q_DIM), BF16)]
    aliases = {}
    for prev in (prev_kv if prev_kv is not None else (None, None)):
        spec, shape, extra_specs, extra_inputs = _stacked_out_specs(
            prev, n_layers, layer, (b, KV_DIM, WINDOW), (bt, KV_DIM, WINDOW), lambda b: (b, 0, 0))
        if extra_inputs:
            aliases[len(inputs)] = len(out_specs)
        in_specs += extra_specs
        inputs += extra_inputs
        out_specs.append(spec)
        out_shape.append(shape)
    return pl.pallas_call(
        functools.partial(_attn_sample_kernel, layer, prev_kv is None),
        grid=(b // bt,), in_specs=in_specs, out_specs=out_specs, out_shape=out_shape,
        input_output_aliases=aliases,
        compiler_params=_params("parallel"), name="attn_sample",
    )(*inputs)


def _attn_bias_tables(l_sample):
    slopes = jnp.exp2(-8.0 * jnp.arange(1, N_HEADS + 1, dtype=F32) / N_HEADS)
    slopes = slopes[jnp.array(HEAD_ORDER)][:, None, None]
    t = jnp.arange(WINDOW, dtype=jnp.int32)[:, None]
    s = jnp.arange(2 * WINDOW, dtype=jnp.int32)[None, :]
    dist = t + WINDOW - s
    valid = (dist >= 0) & (dist <= WINDOW)
    later = jnp.where(valid[None], -slopes * dist.astype(F32)[None], MASKED)
    first = jnp.where((s >= WINDOW)[None], later, MASKED)
    prompt = jnp.stack([first, later]) * LOG2E
    tq = jnp.arange(l_sample, dtype=jnp.int32)[:, None]
    sk = jnp.arange(WINDOW + NEW_KEY_PAD, dtype=jnp.int32)[None, :]
    dist = tq + WINDOW - sk
    valid = (dist >= 0) & (dist <= WINDOW) & (sk < WINDOW + l_sample)
    sample = jnp.where(valid[None], -slopes * dist.astype(F32)[None], MASKED)
    sample = sample.reshape(N_HEADS * l_sample, WINDOW + NEW_KEY_PAD)
    return prompt, sample[:, :WINDOW], sample[:, WINDOW:]


def _log_sigmoid(x):
    return jnp.minimum(x, 0.0) - jnp.log(1.0 + jnp.exp(-jnp.abs(x)))


def _chunk_cumsum(x, c):
    pos = jnp.bitwise_and(lax.broadcasted_iota(jnp.int32, x.shape, 0), c - 1)
    shift = 1
    while shift < c:
        x = x + jnp.where(pos >= shift, pltpu.roll(x, shift, axis=0), 0.0)
        shift *= 2
    return x


def _gla_proj_kernel(chunk, x_ref, g_ref, wm_ref, wgd_ref, wg_ref, bg_ref, qk_ref, vg_ref, b_ref):
    h = _rms(x_ref[...], g_ref[...]).astype(BF16)
    gd = lax.dot_general(h, wgd_ref[...], NT_DIMS, preferred_element_type=F32)
    pre = jnp.dot(gd.astype(BF16), wg_ref[...].astype(BF16), preferred_element_type=F32) + bg_ref[...]
    b_ref[...] = _chunk_cumsum(_log_sigmoid(pre) / GLA_GATE_NORMALIZER, chunk)
    n_qk, n_qkv = 2 * GLA_DK, 2 * GLA_DK + GLA_DV
    proj = lambda rows: lax.dot_general(h, wm_ref[rows, :], NT_DIMS, preferred_element_type=F32)
    qk_ref[...] = proj(slice(0, n_qk))
    vg_ref[:, :GLA_DV] = proj(slice(n_qk, n_qkv)).astype(BF16)
    vg_ref[:, GLA_DV:] = _silu(proj(slice(n_qkv, n_qkv + GLA_DV))).astype(BF16)


def _gla_proj(x, g, w_in_t, w_gate, b_gate, layers, chunk, tm):
    t = x.shape[0]
    lg, lw = layers
    n_main = 2 * GLA_DK + 2 * GLA_DV
    return pl.pallas_call(
        functools.partial(_gla_proj_kernel, chunk),
        grid=(t // tm,),
        in_specs=[pl.BlockSpec((tm, D_MODEL), lambda i: (i, 0)),
                  pl.BlockSpec((None, 1, D_MODEL), lambda i: (lg, 0, 0)),
                  pl.BlockSpec((None, n_main, D_MODEL), lambda i: (lw, 0, 0)),
                  pl.BlockSpec((None, GLA_GATE_RANK, D_MODEL),
                               lambda i: (lw, n_main // GLA_GATE_RANK, 0)),
                  pl.BlockSpec((None, GLA_GATE_RANK, GLA_DK), lambda i: (lw, 0, 0)),
                  pl.BlockSpec((None, 1, GLA_DK), lambda i: (lw, 0, 0))],
        out_specs=[pl.BlockSpec((tm, 2 * GLA_DK), lambda i: (i, 0)),
                   pl.BlockSpec((tm, 2 * GLA_DV), lambda i: (i, 0)),
                   pl.BlockSpec((tm, GLA_DK), lambda i: (i, 0))],
        out_shape=[jax.ShapeDtypeStruct((t, 2 * GLA_DK), F32),
                   jax.ShapeDtypeStruct((t, 2 * GLA_DV), BF16),
                   jax.ShapeDtypeStruct((t, GLA_DK), F32)],
        compiler_params=_params("parallel"), name="gla_proj",
    )(x, g, w_in_t, w_in_t, w_gate, b_gate)


def _gla_gate_out(o, gate, gn):
    return (_rms(o, gn) * gate).astype(BF16)


def _gla_prompt_kernel(q_ref, k_ref, v_ref, r_ref, b_ref, gn_ref, o_ref, s_ref, st_ref):
    c = GLA_CHUNK
    rows = GLA_GROUP * c
    n_groups = q_ref.shape[1] // rows
    row = lax.broadcasted_iota(jnp.int32, (rows, rows), 0)
    col = lax.broadcasted_iota(jnp.int32, (rows, rows), 1)
    causal = (row >= col) & (col >= row - jnp.bitwise_and(row, c - 1))
    gn = gn_ref[...]

    for hd in range(GLA_HEADS_PER_STEP):
        ks = slice(hd * GLA_DK_HEAD, (hd + 1) * GLA_DK_HEAD)
        vs = slice(hd * GLA_DV_HEAD, (hd + 1) * GLA_DV_HEAD)
        st_ref[...] = jnp.zeros_like(st_ref)

        def body(g, carry):
            sl = pl.ds(pl.multiple_of(g * rows, rows), rows)
            q = q_ref[0, sl, ks] * (GLA_DK_HEAD ** -0.5)
            k = k_ref[0, sl, ks]
            v_bf = v_ref[0, sl, vs]
            b = b_ref[0, sl, ks]
            per_chunk = lambda r0: jnp.concatenate(
                [jnp.broadcast_to(b[j * c + r0:j * c + r0 + 1, :], (c, GLA_DK_HEAD))
                 for j in range(GLA_GROUP)], axis=0)
            b_mid = per_chunk(c // 2)
            b_end = per_chunk(c - 1)
            qa = (q * jnp.exp(b - b_mid)).astype(BF16)
            ka = (k * jnp.exp(b_mid - b)).astype(BF16)
            a = lax.dot_general(qa, ka, NT_DIMS, preferred_element_type=F32)
            a = jnp.where(causal, a, 0.0).astype(BF16)
            o_intra = jnp.dot(a, v_bf, preferred_element_type=F32)
            qb = (q * jnp.exp(b)).astype(BF16)
            k_end = (k * jnp.exp(b_end - b)).astype(BF16)
            st = st_ref[...]
            o_inter = []
            for j in range(GLA_GROUP):
                rs = slice(j * c, (j + 1) * c)
                o_inter.append(lax.dot_general(qb[rs], st.astype(BF16), NT_DIMS,
                                               preferred_element_type=F32))
                upd = lax.dot_general(v_bf[rs], k_end[rs], TN_DIMS, preferred_element_type=F32)
                st = st * jnp.exp(b[(j + 1) * c - 1:(j + 1) * c, :]) + upd
            st_ref[...] = st
            o = o_intra + jnp.concatenate(o_inter, axis=0)
            o_ref[0, sl, vs] = _gla_gate_out(o, r_ref[0, sl, vs], gn)
            return carry

        lax.fori_loop(0, n_groups, body, 0, unroll=8)
        s_ref[0, hd] = st_ref[...].T


def _gla_prompt(qk, vg, decay, g_norm, layer):
    b, l, _ = qk.shape
    hp = GLA_HEADS_PER_STEP
    dk, dv = hp * GLA_DK_HEAD, hp * GLA_DV_HEAD
    kcol = GLA_DK // dk
    gcol = GLA_DV // dv
    return pl.pallas_call(
        _gla_prompt_kernel,
        grid=(b, GLA_HEADS // hp),
        in_specs=[pl.BlockSpec((1, l, dk), lambda b, h: (b, 0, h)),
                  pl.BlockSpec((1, l, dk), lambda b, h: (b, 0, kcol + h)),
                  pl.BlockSpec((1, l, dv), lambda b, h: (b, 0, h)),
                  pl.BlockSpec((1, l, dv), lambda b, h: (b, 0, gcol + h)),
                  pl.BlockSpec((1, l, dk), lambda b, h: (b, 0, h)),
                  pl.BlockSpec((None, 1, GLA_DV_HEAD), lambda b, h: (layer, 0, 0))],
        out_specs=[pl.BlockSpec((1, l, dv), lambda b, h: (b, 0, h)),
                   pl.BlockSpec((1, hp, GLA_DK_HEAD, GLA_DV_HEAD), lambda b, h: (b, h, 0, 0))],
        out_shape=[jax.ShapeDtypeStruct((b, l, GLA_DV), BF16),
                   jax.ShapeDtypeStruct((b, GLA_HEADS, GLA_DK_HEAD, GLA_DV_HEAD), F32)],
        scratch_shapes=[pltpu.VMEM((GLA_DV_HEAD, GLA_DK_HEAD), F32)],
        compiler_params=_params("parallel", "parallel"), name="gla_prompt",
    )(qk, qk, vg, vg, decay, g_norm)


def _gla_sample_kernel(layer, owner_of_all, q_ref, k_ref, v_ref, r_ref, b_ref, s0_ref, gn_ref, *rest):
    o_ref, s_ref = rest[-2:]
    s_ref = _own_slab(s_ref, owner_of_all, layer)
    bt, c, _ = q_ref.shape
    row = lax.broadcasted_iota(jnp.int32, (c, c), 0)
    col = lax.broadcasted_iota(jnp.int32, (c, c), 1)
    causal = row >= col
    gn = gn_ref[...]

    def body(i, carry):
        q = q_ref[i] * (GLA_DK_HEAD ** -0.5)
        k = k_ref[i]
        v = v_ref[i].astype(F32)
        r = r_ref[i]
        b = b_ref[i]
        b_mid = b[c // 2:c // 2 + 1, :]
        b_end = b[c - 1:c, :]
        qa = q * jnp.exp(b - b_mid)
        ka = k * jnp.exp(b_mid - b)
        qb = (q * jnp.exp(b)).astype(BF16)
        k_end_t = (k * jnp.exp(b_end - b)).T
        dec_col = jnp.broadcast_to(jnp.exp(b_end), (c, GLA_DK)).T[:, 0:1]
        for h in range(GLA_HEADS):
            ks = slice(h * GLA_DK_HEAD, (h + 1) * GLA_DK_HEAD)
            vs = slice(h * GLA_DV_HEAD, (h + 1) * GLA_DV_HEAD)
            s0 = s0_ref[i, h]
            a = lax.dot_general(qa[:, ks], ka[:, ks], NT_DIMS, preferred_element_type=F32)
            a = jnp.where(causal, a, 0.0)
            o = jnp.dot(a, v[:, vs], preferred_element_type=F32)
            o = o + jnp.dot(qb[:, ks], s0.astype(BF16), preferred_element_type=F32)
            upd = jnp.dot(k_end_t[ks, :], v[:, vs], preferred_element_type=F32)
            s_ref[i, h] = s0 * dec_col[ks, :] + upd
            o_ref[i, :, vs] = _gla_gate_out(o, r[:, vs], gn)
        return carry

    lax.fori_loop(0, bt, body, 0, unroll=2)


def _gla_sample(qk, vg, decay, state, g_norm, layer, prev_state):
    b, l, _ = qk.shape
    bt = min(SAMPLE_BATCH_TILE, b)
    state_block = (bt, GLA_HEADS, GLA_DK_HEAD, GLA_DV_HEAD)
    in_specs = [pl.BlockSpec((bt, l, GLA_DK), lambda b: (b, 0, 0)),
                pl.BlockSpec((bt, l, GLA_DK), lambda b: (b, 0, 1)),
                pl.BlockSpec((bt, l, GLA_DV), lambda b: (b, 0, 0)),
                pl.BlockSpec((bt, l, GLA_DV), lambda b: (b, 0, 1)),
                pl.BlockSpec((bt, l, GLA_DK), lambda b: (b, 0, 0)),
                pl.BlockSpec((None,) + state_block, lambda b: (layer, b, 0, 0, 0)),
                pl.BlockSpec((None, 1, GLA_DV_HEAD), lambda b: (layer, 0, 0))]
    inputs = [qk, qk, vg, vg, decay, state, g_norm]
    spec, shape, extra_specs, extra_inputs = _stacked_out_specs(
        prev_state, state.shape[0], layer, state.shape[1:], state_block, lambda b: (b, 0, 0, 0))
    return pl.pallas_call(
        functools.partial(_gla_sample_kernel, layer, prev_state is None),
        grid=(b // bt,),
        in_specs=in_specs + extra_specs,
        out_specs=[pl.BlockSpec((bt, l, GLA_DV), lambda b: (b, 0, 0)), spec],
        out_shape=[jax.ShapeDtypeStruct((b, l, GLA_DV), BF16), shape],
        input_output_aliases={len(inputs): 1} if extra_inputs else {},
        compiler_params=_params("parallel"), name="gla_sample",
    )(*inputs, *extra_inputs)


def kernel(x_prompt, x_sample, cache_k, cache_v, state_gla, norm_mix, norm_ffn, norm_final,
           w_attn_in, w_attn_out, attn_sinks, w_gla_in, w_gla_gate_up, b_gla_gate, gla_out_norm,
           w_gla_out, w_ffn_in, w_ffn_out):
    bp, lp, _ = x_prompt.shape
    bs, ls, _ = x_sample.shape
    depth = norm_mix.shape[0]
    n_attn = cache_k.shape[0]
    yp = x_prompt.reshape(bp * lp, D_MODEL)
    ys = x_sample.reshape(bs * ls, D_MODEL)
    tp = min(TOKEN_TILE, bp * lp)
    ts = min(TOKEN_TILE, bs * ls)
    bias_p, bias_old, bias_new = _attn_bias_tables(ls)

    g_mix = norm_mix.reshape(depth, 1, D_MODEL)
    g_ffn = norm_ffn.reshape(depth, 1, D_MODEL)
    g_fin = norm_final.reshape(1, D_MODEL)
    w_attn_bf = w_attn_in.astype(BF16)
    w_attn_in_bf = jnp.concatenate(
        [_regroup_heads(w_attn_bf[:, :, :Q_DIM], 2, HEAD_DIM), w_attn_bf[:, :, Q_DIM:]], axis=2)
    w_attn_out_bf = _regroup_heads(w_attn_out.astype(BF16), 1, HEAD_DIM)
    sinks = _regroup_heads(attn_sinks, 1, 1)
    w_gla_in_t = jnp.transpose(w_gla_in, (0, 2, 1)).astype(BF16)
    w_fi = w_fo = w_gla_out_bf = None
    b_gate = b_gla_gate.reshape(-1, 1, GLA_DK)
    g_gla = gla_out_norm.reshape(-1, 1, GLA_DV_HEAD)
    cache_kt = jnp.transpose(cache_k, (0, 1, 3, 4, 2)).reshape(n_attn, bs, KV_DIM, WINDOW)
    cache_vt = jnp.transpose(cache_v, (0, 1, 3, 4, 2)).reshape(n_attn, bs, KV_DIM, WINDOW)

    kp_l, vp_l, sp_l = [], [], []
    new_kv, new_state = None, None
    for i in range(depth):
        j = i // 2
        if i % 2 == 0:
            w_mix, lm = w_attn_out_bf, j
            (qkv_p,) = _norm_matmul(yp, g_mix, [w_attn_in_bf], (i, j), tp)
            (qkv_s,) = _norm_matmul(ys, g_mix, [w_attn_in_bf], (i, j), ts)
            qkv_p = qkv_p.reshape(bp, lp, Q_DIM + 2 * KV_DIM)
            qkv_s = qkv_s.reshape(bs, ls, Q_DIM + 2 * KV_DIM)
            cast = [(w_gla_out, j)] + ([(w_ffn_in, 0), (w_ffn_out, 0)] if i == 0 else [])
            op, w_gla_out_bf, *w_ffn0 = _attn_prompt(qkv_p, sinks[j] * LOG2E, bias_p, cast)
            if w_ffn0:
                w_fi, w_fo = w_ffn0
            sink_col = jnp.repeat(sinks[j], ls).reshape(N_HEADS * ls, 1)
            os_, *new_kv = _attn_sample(qkv_s, cache_kt, cache_vt, j, bias_old, bias_new, sink_col,
                                        new_kv)
            kv_shape = (bp, WINDOW, N_KV_HEADS, HEAD_DIM)
            kp_l.append(qkv_p[:, lp - WINDOW:, Q_DIM:Q_DIM + KV_DIM].reshape(kv_shape))
            vp_l.append(qkv_p[:, lp - WINDOW:, Q_DIM + KV_DIM:].reshape(kv_shape))
        else:
            w_mix, lm = w_gla_out_bf[None], 0
            gla_p = _gla_proj(yp, g_mix, w_gla_in_t, w_gla_gate_up, b_gate, (i, j), GLA_CHUNK, tp)
            gla_s = _gla_proj(ys, g_mix, w_gla_in_t, w_gla_gate_up, b_gate, (i, j), ls, ts)
            op, sp = _gla_prompt(*[t.reshape(bp, lp, -1) for t in gla_p], g_gla, j)
            os_, new_state = _gla_sample(*[t.reshape(bs, ls, -1) for t in gla_s], state_gla, g_gla, j,
                                         new_state)
            sp_l.append(sp)
        last = i == depth - 1
        yp, *w_next = _mix_out_ffn(yp, op.reshape(bp * lp, D_MODEL), w_mix, g_ffn, w_fi, w_fo,
                                   g_fin, (lm, i), last, min(FFN_TOKEN_TILE, bp * lp),
                                   [] if last else [(w_ffn_in, i + 1), (w_ffn_out, i + 1)])
        (ys,) = _mix_out_ffn(ys, os_.reshape(bs * ls, D_MODEL), w_mix, g_ffn, w_fi, w_fo,
                             g_fin, (lm, i), last, min(FFN_TOKEN_TILE, bs * ls))
        if w_next:
            w_fi, w_fo = w_next

    def window_major(xt):
        return jnp.transpose(xt.reshape(n_attn, bs, N_KV_HEADS, HEAD_DIM, WINDOW), (0, 1, 4, 2, 3))

    return (yp.reshape(bp, lp, D_MODEL), ys.reshape(bs, ls, D_MODEL),
            jnp.stack(kp_l), jnp.stack(vp_l), jnp.stack(sp_l),
            window_major(new_kv[0]), window_major(new_kv[1]), new_state)
```

```python
import functools

import jax
import jax.numpy as jnp
from jax import lax
from jax.experimental import pallas as pl
from jax.experimental.pallas import tpu as pltpu

F32 = jnp.float32
BF16 = jnp.bfloat16

D_MODEL = 1024
EPS = 1e-6
WINDOW = 128
HEAD_DIM = 64
N_HEADS = 16
N_KV_HEADS = 4
GROUP = 4
Q_DIM = 1024
KV_DIM = 256
GLA_HEADS = 4
GLA_DK = 512
GLA_DV = 1024
GLA_DK_HEAD = 128
GLA_DV_HEAD = 256
GLA_GATE_RANK = 16
GLA_GATE_NORMALIZER = 16.0
GLA_CHUNK = 64
D_FF = 2816
MASKED = -1e30
LOG2E = 1.4426950408889634

VMEM_LIMIT_BYTES = 58 * 1024 * 1024
LANES = 128
BF16_SUBLANES = 16
TOKEN_TILE = 1024
FFN_TOKEN_TILE = 1024
FFN_TILE = 256
ATTN_Q_BLOCKS = 16
GLA_GROUP = 4
GLA_HEADS_PER_STEP = 2
SAMPLE_BATCH_TILE = 8
ATTN_SAMPLE_TILE = 16
NEW_KEY_PAD = 16

NT_DIMS = (((1,), (1,)), ((), ()))
TN_DIMS = (((0,), (0,)), ((), ()))


def _params(*sem):
    return pltpu.CompilerParams(dimension_semantics=sem, vmem_limit_bytes=VMEM_LIMIT_BYTES)


def _rms(x, g):
    r = lax.rsqrt(jnp.mean(x * x, axis=-1, keepdims=True) + EPS)
    return x * r * g


def _silu(x):
    return x / (1.0 + jnp.exp(-x))


def _norm_matmul_kernel(n, x_ref, g_ref, *refs):
    h = _rms(x_ref[...], g_ref[...]).astype(BF16)
    for w_ref, o_ref in zip(refs[:n], refs[n:]):
        o_ref[...] = jnp.dot(h, w_ref[...], preferred_element_type=F32)


def _norm_matmul(x, g, ws, layers, tm):
    t = x.shape[0]
    lg, lw = layers
    in_specs = [pl.BlockSpec((tm, D_MODEL), lambda i: (i, 0)),
                pl.BlockSpec((None, 1, D_MODEL), lambda i: (lg, 0, 0))]
    in_specs += [pl.BlockSpec((None,) + w.shape[1:], lambda i: (lw, 0, 0)) for w in ws]
    out_specs = [pl.BlockSpec((tm, w.shape[2]), lambda i: (i, 0)) for w in ws]
    out_shape = [jax.ShapeDtypeStruct((t, w.shape[2]), F32) for w in ws]
    return pl.pallas_call(
        functools.partial(_norm_matmul_kernel, len(ws)),
        grid=(t // tm,), in_specs=in_specs, out_specs=out_specs, out_shape=out_shape,
        compiler_params=_params("parallel"), name="norm_matmul",
    )(x, g, *ws)


def _slab_rows(total, steps):
    rows = BF16_SUBLANES
    while total % rows or total // rows > steps:
        rows += BF16_SUBLANES
    return rows


def _cast_specs(sources, steps, step_of):
    in_specs, out_specs, out_shape = [], [], []
    for w_all, layer in sources:
        rows = _slab_rows(w_all.shape[1], steps)
        last = w_all.shape[1] // rows - 1
        slab = lambda *g, last=last: (jnp.minimum(step_of(*g), last), 0)
        in_specs.append(pl.BlockSpec((None, rows, w_all.shape[2]),
                                     lambda *g, slab=slab, layer=layer: (layer,) + slab(*g)))
        out_specs.append(pl.BlockSpec((rows, w_all.shape[2]), slab))
        out_shape.append(jax.ShapeDtypeStruct(w_all.shape[1:], BF16))
    return in_specs, out_specs, out_shape


def _cast_slabs(src_refs, dst_refs):
    for src, dst in zip(src_refs, dst_refs):
        dst[...] = src[...].astype(BF16)


def _ffn_kernel(final_norm, n_cast, x_ref, a_ref, wm_ref, g_ref, wi_ref, wo_ref, gf_ref, *rest):
    cast_src, rest = rest[:n_cast], rest[n_cast:]
    o_ref, cast_dst, act_ref = rest[0], rest[1:1 + n_cast], rest[-1]
    _cast_slabs(cast_src, cast_dst)
    y = x_ref[...] + jnp.dot(a_ref[...], wm_ref[...], preferred_element_type=F32)
    h = _rms(y, g_ref[...]).astype(BF16)
    for j in range(D_FF // FFN_TILE):
        gate = jnp.dot(h, wi_ref[:, j * FFN_TILE:(j + 1) * FFN_TILE], preferred_element_type=F32)
        up = jnp.dot(h, wi_ref[:, D_FF + j * FFN_TILE:D_FF + (j + 1) * FFN_TILE],
                     preferred_element_type=F32)
        act_ref[:, j * FFN_TILE:(j + 1) * FFN_TILE] = (_silu(gate) * up).astype(BF16)
    y = y + jnp.dot(act_ref[...], wo_ref[...], preferred_element_type=F32)
    if final_norm:
        y = _rms(y, gf_ref[...])
    o_ref[...] = y


def _resident(block_shape, index_map):
    return pl.BlockSpec(block_shape, index_map, pipeline_mode=pl.Buffered(1))


def _mix_out_ffn(x, a, w_mix, g, w_in, w_out, g_final, layers, final_norm, tm, cast=()):
    t = x.shape[0]
    lm, lf = layers
    steps = t // tm
    row = lambda i: (i, 0)
    cast_in, cast_out, cast_shape = _cast_specs(cast, steps, lambda i: i)
    in_specs = [pl.BlockSpec((tm, D_MODEL), row),
                pl.BlockSpec((tm, D_MODEL), row),
                _resident((None, D_MODEL, D_MODEL), lambda i: (lm, 0, 0)),
                _resident((None, 1, D_MODEL), lambda i: (lf, 0, 0)),
                _resident((D_MODEL, 2 * D_FF), lambda i: (0, 0)),
                _resident((D_FF, D_MODEL), lambda i: (0, 0)),
                _resident((1, D_MODEL), lambda i: (0, 0))]
    return pl.pallas_call(
        functools.partial(_ffn_kernel, final_norm, len(cast)),
        grid=(steps,),
        in_specs=in_specs + cast_in,
        out_specs=[pl.BlockSpec((tm, D_MODEL), row)] + cast_out,
        out_shape=[jax.ShapeDtypeStruct((t, D_MODEL), F32)] + cast_shape,
        scratch_shapes=[pltpu.VMEM((tm, D_FF), BF16)],
        compiler_params=_params("arbitrary"), name="mix_out_ffn",
    )(x, a, w_mix, g, w_in, w_out, g_final, *[w for w, _ in cast])


HALVES = LANES // HEAD_DIM
HEAD_ORDER = tuple((HALVES * (p // (HALVES * GROUP)) + p % HALVES) * GROUP + (p % (HALVES * GROUP)) // HALVES
                   for p in range(N_HEADS))


def _regroup_heads(x, axis, width):
    lead, tail = x.shape[:axis], x.shape[axis + 1:]
    x = x.reshape(lead + (N_KV_HEADS // HALVES, HALVES, GROUP, width) + tail)
    x = jnp.swapaxes(x, axis + 1, axis + 2)
    return x.reshape(lead + (N_HEADS * width,) + tail)


def _half_masks(rows):
    low = lax.broadcasted_iota(jnp.int32, (rows, LANES), 1) < HEAD_DIM
    return low, jnp.logical_not(low)


def _attn_prompt_kernel(n_cast, sink_ref, q_ref, kp_ref, kc_ref, vp_ref, vc_ref, bias_ref, *rest):
    o_ref = rest[n_cast]
    _cast_slabs(rest[:n_cast], rest[n_cast + 1:])
    step = pl.program_id(1)
    q_all = (q_ref[0] * (LOG2E * HEAD_DIM ** -0.5)).astype(BF16)
    k_all = jnp.concatenate([kp_ref[0], kc_ref[0]], axis=0).astype(BF16)
    v_f32 = jnp.concatenate([vp_ref[0], vc_ref[0]], axis=0)
    masks = _half_masks(WINDOW)
    key_masks = _half_masks(2 * WINDOW)
    zero = jnp.zeros((), BF16)
    half_ones = [jnp.where(mask, 1.0, 0.0).astype(BF16) for mask in key_masks]
    for qb in range(ATTN_Q_BLOCKS):
        rows = slice(qb * WINDOW, (qb + 1) * WINDOW)
        keys = slice(qb * WINDOW, (qb + 2) * WINDOW)
        table = 1 if qb > 0 else jnp.minimum(step, 1)
        for blk in range(KV_DIM // LANES):
            lanes = slice(blk * LANES, (blk + 1) * LANES)
            k2 = k_all[keys, lanes]
            v2 = v_f32[keys, lanes]
            v_pair = jnp.concatenate(
                [jnp.concatenate([jnp.where(key_masks[half], v2, 0.0).astype(BF16), half_ones[half]],
                                 axis=1)
                 for half in range(HALVES)], axis=0)
            for c in range(blk * GROUP, (blk + 1) * GROUP):
                qc = q_all[rows, c * LANES:(c + 1) * LANES]
                probs, sink_terms = [], []
                for half in range(HALVES):
                    p = HALVES * c + half
                    qm = jnp.where(masks[half], qc, zero)
                    s = lax.dot_general(qm, k2, NT_DIMS, preferred_element_type=F32)
                    s = s + bias_ref[table, p]
                    sink = sink_ref[p]
                    m = jnp.maximum(jnp.max(s, axis=-1, keepdims=True), sink)
                    probs.append(jnp.exp2(s - m).astype(BF16))
                    sink_terms.append(jnp.exp2(sink - m))
                ov = jnp.dot(jnp.concatenate(probs, axis=1), v_pair,
                             preferred_element_type=F32)
                denom = ov[:, LANES:] + jnp.where(
                    masks[0], *[jnp.broadcast_to(t, (WINDOW, LANES)) for t in sink_terms])
                o_ref[0, rows, c * LANES:(c + 1) * LANES] = (ov[:, :LANES] / denom).astype(BF16)


def _attn_prompt(qkv, sinks, bias, cast=()):
    b, l, _ = qkv.shape
    rows = ATTN_Q_BLOCKS * WINDOW
    steps = l // rows
    kcol, vcol = Q_DIM // KV_DIM, Q_DIM // KV_DIM + 1
    prev = lambda i: jnp.maximum(ATTN_Q_BLOCKS * i - 1, 0)
    cast_in, cast_out, cast_shape = _cast_specs(cast, b * steps, lambda b, i: b * steps + i)
    return pl.pallas_call(
        functools.partial(_attn_prompt_kernel, len(cast)),
        grid=(b, steps),
        in_specs=[pl.BlockSpec(memory_space=pltpu.SMEM),
                  pl.BlockSpec((1, rows, Q_DIM), lambda b, i: (b, i, 0)),
                  pl.BlockSpec((1, WINDOW, KV_DIM), lambda b, i: (b, prev(i), kcol)),
                  pl.BlockSpec((1, rows, KV_DIM), lambda b, i: (b, i, kcol)),
                  pl.BlockSpec((1, WINDOW, KV_DIM), lambda b, i: (b, prev(i), vcol)),
                  pl.BlockSpec((1, rows, KV_DIM), lambda b, i: (b, i, vcol)),
                  pl.BlockSpec(bias.shape, lambda b, i: (0, 0, 0, 0))] + cast_in,
        out_specs=[pl.BlockSpec((1, rows, Q_DIM), lambda b, i: (b, i, 0))] + cast_out,
        out_shape=[jax.ShapeDtypeStruct((b, l, Q_DIM), BF16)] + cast_shape,
        compiler_params=_params("arbitrary", "arbitrary"), name="attn_prompt",
    )(sinks, qkv, qkv, qkv, qkv, qkv, bias, *[w for w, _ in cast])


def _own_slab(ref, owner_of_all, layer):
    if not owner_of_all:
        return ref
    for other in range(ref.shape[0]):
        if other != layer:
            ref[other] = jnp.zeros(ref.shape[1:], ref.dtype)
    return ref.at[layer]


def _stacked_out_specs(prev, n_layers, layer, slab_shape, block, index):
    shape = jax.ShapeDtypeStruct((n_layers,) + slab_shape, F32)
    if prev is None:
        return pl.BlockSpec((n_layers,) + block, lambda b: (0,) + index(b)), shape, [], []
    spec = pl.BlockSpec((None,) + block, lambda b: (layer,) + index(b))
    return spec, shape, [pl.BlockSpec(memory_space=pl.ANY)], [prev]


def _attn_sample_kernel(layer, owner_of_all, q_ref, kn_ref, vn_ref, kt_ref, vt_ref, bo_ref, bn_ref,
                        sink_ref, *rest):
    o_ref, kto_ref, vto_ref = rest[-3:]
    kto_ref = _own_slab(kto_ref, owner_of_all, layer)
    vto_ref = _own_slab(vto_ref, owner_of_all, layer)
    bt, l, _ = q_ref.shape
    masks = _half_masks(l)
    keep = lax.broadcasted_iota(jnp.int32, (KV_DIM, WINDOW), 1) < WINDOW - l
    zero_block = jnp.zeros((l, LANES), F32)
    pad_rows = jnp.zeros((WINDOW - l, KV_DIM), F32)
    pad_keys = jnp.zeros((NEW_KEY_PAD - l, KV_DIM), F32)
    ones_old = jnp.ones((LANES, WINDOW), BF16)
    ones_new = jnp.ones((NEW_KEY_PAD, LANES), BF16)
    bias_old = bo_ref[...]
    bias_new = bn_ref[...]
    sink = sink_ref[...]

    def body(b, carry):
        q = q_ref[b] * (HEAD_DIM ** -0.5)
        pieces = []
        for p in range(N_HEADS):
            c = p // HALVES
            src = jnp.where(masks[p % HALVES], q[:, c * LANES:(c + 1) * LANES], 0.0)
            pieces.append(jnp.concatenate(
                [src, zero_block] if c // GROUP == 0 else [zero_block, src], axis=1))
        qt = jnp.concatenate(pieces, axis=0).astype(BF16)

        k_old, v_old = kt_ref[b], vt_ref[b]
        k_new, v_new = kn_ref[b], vn_ref[b]
        k_pad = jnp.concatenate([k_new, pad_keys], axis=0).astype(BF16)
        v_pad = jnp.concatenate([v_new, pad_keys], axis=0).astype(BF16)
        s_old = jnp.dot(qt, k_old.astype(BF16), preferred_element_type=F32) + bias_old
        s_new = lax.dot_general(qt, k_pad, NT_DIMS, preferred_element_type=F32) + bias_new
        m = jnp.maximum(sink, jnp.maximum(jnp.max(s_old, axis=-1, keepdims=True),
                                          jnp.max(s_new, axis=-1, keepdims=True)))
        p_old = jnp.exp(s_old - m).astype(BF16)
        p_new = jnp.exp(s_new - m).astype(BF16)
        v_old_1 = jnp.concatenate([v_old.astype(BF16), ones_old], axis=0)
        v_new_1 = jnp.concatenate([v_pad, ones_new], axis=1)
        o = lax.dot_general(p_old, v_old_1, NT_DIMS, preferred_element_type=F32)
        o = o + jnp.dot(p_new, v_new_1, preferred_element_type=F32)
        inv = 1.0 / (o[:, KV_DIM:] + jnp.exp(sink - m))
        cols = []
        for c in range(N_HEADS // HALVES):
            lanes = slice((c // GROUP) * LANES, (c // GROUP + 1) * LANES)
            parts = [o[(HALVES * c + half) * l:(HALVES * c + half + 1) * l, lanes]
                     * inv[(HALVES * c + half) * l:(HALVES * c + half + 1) * l, :]
                     for half in range(HALVES)]
            cols.append(jnp.where(masks[0], parts[0], parts[1]))
        o_ref[b] = jnp.concatenate(cols, axis=1).astype(BF16)

        for old, new, out_ref in ((k_old, k_new, kto_ref), (v_old, v_new, vto_ref)):
            shifted = pltpu.roll(old, WINDOW - l, axis=1)
            placed = jnp.concatenate([pad_rows, new], axis=0).T
            out_ref[b] = jnp.where(keep, shifted, placed)
        return carry

    lax.fori_loop(0, bt, body, 0, unroll=8)


def _attn_sample(qkv, cache_kt, cache_vt, layer, bias_old, bias_new, sink_col, prev_kv):
    b, l, _ = qkv.shape
    n_layers = cache_kt.shape[0]
    bt = min(ATTN_SAMPLE_TILE, b)
    kcol, vcol = Q_DIM // KV_DIM, Q_DIM // KV_DIM + 1
    const = lambda b: (0, 0)
    in_specs = [pl.BlockSpec((bt, l, Q_DIM), lambda b: (b, 0, 0)),
                pl.BlockSpec((bt, l, KV_DIM), lambda b: (b, 0, kcol)),
                pl.BlockSpec((bt, l, KV_DIM), lambda b: (b, 0, vcol)),
                pl.BlockSpec((None, bt, KV_DIM, WINDOW), lambda b: (layer, b, 0, 0)),
                pl.BlockSpec((None, bt, KV_DIM, WINDOW), lambda b: (layer, b, 0, 0)),
                pl.BlockSpec(bias_old.shape, const),
                pl.BlockSpec(bias_new.shape, const),
                pl.BlockSpec(sink_col.shape, const)]
    inputs = [qkv, qkv, qkv, cache_kt, cache_vt, bias_old, bias_new, sink_col]
    out_specs = [pl.BlockSpec((bt, l, Q_DIM), lambda b: (b, 0, 0))]
    out_shape = [jax.ShapeDtypeStruct((b, l, Q_DIM), BF16)]
    aliases = {}
    for prev in (prev_kv if prev_kv is not None else (None, None)):
        spec, shape, extra_specs, extra_inputs = _stacked_out_specs(
            prev, n_layers, layer, (b, KV_DIM, WINDOW), (bt, KV_DIM, WINDOW), lambda b: (b, 0, 0))
        if extra_inputs:
            aliases[len(inputs)] = len(out_specs)
        in_specs += extra_specs
        inputs += extra_inputs
        out_specs.append(spec)
        out_shape.append(shape)
    return pl.pallas_call(
        functools.partial(_attn_sample_kernel, layer, prev_kv is None),
        grid=(b // bt,), in_specs=in_specs, out_specs=out_specs, out_shape=out_shape,
        input_output_aliases=aliases,
        compiler_params=_params("parallel"), name="attn_sample",
    )(*inputs)


def _attn_bias_tables(l_sample):
    slopes = jnp.exp2(-8.0 * jnp.arange(1, N_HEADS + 1, dtype=F32) / N_HEADS)
    slopes = slopes[jnp.array(HEAD_ORDER)][:, None, None]
    t = jnp.arange(WINDOW, dtype=jnp.int32)[:, None]
    s = jnp.arange(2 * WINDOW, dtype=jnp.int32)[None, :]
    dist = t + WINDOW - s
    valid = (dist >= 0) & (dist <= WINDOW)
    later = jnp.where(valid[None], -slopes * dist.astype(F32)[None], MASKED)
    first = jnp.where((s >= WINDOW)[None], later, MASKED)
    prompt = jnp.stack([first, later]) * LOG2E
    tq = jnp.arange(l_sample, dtype=jnp.int32)[:, None]
    sk = jnp.arange(WINDOW + NEW_KEY_PAD, dtype=jnp.int32)[None, :]
    dist = tq + WINDOW - sk
    valid = (dist >= 0) & (dist <= WINDOW) & (sk < WINDOW + l_sample)
    sample = jnp.where(valid[None], -slopes * dist.astype(F32)[None], MASKED)
    sample = sample.reshape(N_HEADS * l_sample, WINDOW + NEW_KEY_PAD)
    return prompt, sample[:, :WINDOW], sample[:, WINDOW:]


def _log_sigmoid(x):
    return jnp.minimum(x, 0.0) - jnp.log(1.0 + jnp.exp(-jnp.abs(x)))


def _chunk_cumsum(x, c):
    pos = jnp.bitwise_and(lax.broadcasted_iota(jnp.int32, x.shape, 0), c - 1)
    shift = 1
    while shift < c:
        x = x + jnp.where(pos >= shift, pltpu.roll(x, shift, axis=0), 0.0)
        shift *= 2
    return x


def _gla_proj_kernel(chunk, x_ref, g_ref, wm_ref, wgd_ref, wg_ref, bg_ref, qk_ref, vg_ref, b_ref):
    h = _rms(x_ref[...], g_ref[...]).astype(BF16)
    gd = lax.dot_general(h, wgd_ref[...], NT_DIMS, preferred_element_type=F32)
    pre = jnp.dot(gd.astype(BF16), wg_ref[...].astype(BF16), preferred_element_type=F32) + bg_ref[...]
    b_ref[...] = _chunk_cumsum(_log_sigmoid(pre) / GLA_GATE_NORMALIZER, chunk)
    n_qk, n_qkv = 2 * GLA_DK, 2 * GLA_DK + GLA_DV
    proj = lambda rows: lax.dot_general(h, wm_ref[rows, :], NT_DIMS, preferred_element_type=F32)
    qk_ref[...] = proj(slice(0, n_qk))
    vg_ref[:, :GLA_DV] = proj(slice(n_qk, n_qkv)).astype(BF16)
    vg_ref[:, GLA_DV:] = _silu(proj(slice(n_qkv, n_qkv + GLA_DV))).astype(BF16)


def _gla_proj(x, g, w_in_t, w_gate, b_gate, layers, chunk, tm):
    t = x.shape[0]
    lg, lw = layers
    n_main = 2 * GLA_DK + 2 * GLA_DV
    return pl.pallas_call(
        functools.partial(_gla_proj_kernel, chunk),
        grid=(t // tm,),
        in_specs=[pl.BlockSpec((tm, D_MODEL), lambda i: (i, 0)),
                  pl.BlockSpec((None, 1, D_MODEL), lambda i: (lg, 0, 0)),
                  pl.BlockSpec((None, n_main, D_MODEL), lambda i: (lw, 0, 0)),
                  pl.BlockSpec((None, GLA_GATE_RANK, D_MODEL),
                               lambda i: (lw, n_main // GLA_GATE_RANK, 0)),
                  pl.BlockSpec((None, GLA_GATE_RANK, GLA_DK), lambda i: (lw, 0, 0)),
                  pl.BlockSpec((None, 1, GLA_DK), lambda i: (lw, 0, 0))],
        out_specs=[pl.BlockSpec((tm, 2 * GLA_DK), lambda i: (i, 0)),
                   pl.BlockSpec((tm, 2 * GLA_DV), lambda i: (i, 0)),
                   pl.BlockSpec((tm, GLA_DK), lambda i: (i, 0))],
        out_shape=[jax.ShapeDtypeStruct((t, 2 * GLA_DK), F32),
                   jax.ShapeDtypeStruct((t, 2 * GLA_DV), BF16),
                   jax.ShapeDtypeStruct((t, GLA_DK), F32)],
        compiler_params=_params("parallel"), name="gla_proj",
    )(x, g, w_in_t, w_in_t, w_gate, b_gate)


def _gla_gate_out(o, gate, gn):
    return (_rms(o, gn) * gate).astype(BF16)


def _gla_prompt_kernel(q_ref, k_ref, v_ref, r_ref, b_ref, gn_ref, o_ref, s_ref, st_ref):
    c = GLA_CHUNK
    rows = GLA_GROUP * c
    n_groups = q_ref.shape[1] // rows
    row = lax.broadcasted_iota(jnp.int32, (rows, rows), 0)
    col = lax.broadcasted_iota(jnp.int32, (rows, rows), 1)
    causal = (row >= col) & (col >= row - jnp.bitwise_and(row, c - 1))
    gn = gn_ref[...]

    for hd in range(GLA_HEADS_PER_STEP):
        ks = slice(hd * GLA_DK_HEAD, (hd + 1) * GLA_DK_HEAD)
        vs = slice(hd * GLA_DV_HEAD, (hd + 1) * GLA_DV_HEAD)
        st_ref[...] = jnp.zeros_like(st_ref)

        def body(g, carry):
            sl = pl.ds(pl.multiple_of(g * rows, rows), rows)
            q = q_ref[0, sl, ks] * (GLA_DK_HEAD ** -0.5)
            k = k_ref[0, sl, ks]
            v_bf = v_ref[0, sl, vs]
            b = b_ref[0, sl, ks]
            per_chunk = lambda r0: jnp.concatenate(
                [jnp.broadcast_to(b[j * c + r0:j * c + r0 + 1, :], (c, GLA_DK_HEAD))
                 for j in range(GLA_GROUP)], axis=0)
            b_mid = per_chunk(c // 2)
            b_end = per_chunk(c - 1)
            qa = (q * jnp.exp(b - b_mid)).astype(BF16)
            ka = (k * jnp.exp(b_mid - b)).astype(BF16)
            a = lax.dot_general(qa, ka, NT_DIMS, preferred_element_type=F32)
            a = jnp.where(causal, a, 0.0).astype(BF16)
            o_intra = jnp.dot(a, v_bf, preferred_element_type=F32)
            qb = (q * jnp.exp(b)).astype(BF16)
            k_end = (k * jnp.exp(b_end - b)).astype(BF16)
            st = st_ref[...]
            o_inter = []
            for j in range(GLA_GROUP):
                rs = slice(j * c, (j + 1) * c)
                o_inter.append(lax.dot_general(qb[rs], st.astype(BF16), NT_DIMS,
                                               preferred_element_type=F32))
                upd = lax.dot_general(v_bf[rs], k_end[rs], TN_DIMS, preferred_element_type=F32)
                st = st * jnp.exp(b[(j + 1) * c - 1:(j + 1) * c, :]) + upd
            st_ref[...] = st
            o = o_intra + jnp.concatenate(o_inter, axis=0)
            o_ref[0, sl, vs] = _gla_gate_out(o, r_ref[0, sl, vs], gn)
            return carry

        lax.fori_loop(0, n_groups, body, 0, unroll=8)
        s_ref[0, hd] = st_ref[...].T


def _gla_prompt(qk, vg, decay, g_norm, layer):
    b, l, _ = qk.shape
    hp = GLA_HEADS_PER_STEP
    dk, dv = hp * GLA_DK_HEAD, hp * GLA_DV_HEAD
    kcol = GLA_DK // dk
    gcol = GLA_DV // dv
    return pl.pallas_call(
        _gla_prompt_kernel,
        grid=(b, GLA_HEADS // hp),
        in_specs=[pl.BlockSpec((1, l, dk), lambda b, h: (b, 0, h)),
                  pl.BlockSpec((1, l, dk), lambda b, h: (b, 0, kcol + h)),
                  pl.BlockSpec((1, l, dv), lambda b, h: (b, 0, h)),
                  pl.BlockSpec((1, l, dv), lambda b, h: (b, 0, gcol + h)),
                  pl.BlockSpec((1, l, dk), lambda b, h: (b, 0, h)),
                  pl.BlockSpec((None, 1, GLA_DV_HEAD), lambda b, h: (layer, 0, 0))],
        out_specs=[pl.BlockSpec((1, l, dv), lambda b, h: (b, 0, h)),
                   pl.BlockSpec((1, hp, GLA_DK_HEAD, GLA_DV_HEAD), lambda b, h: (b, h, 0, 0))],
        out_shape=[jax.ShapeDtypeStruct((b, l, GLA_DV), BF16),
                   jax.ShapeDtypeStruct((b, GLA_HEADS, GLA_DK_HEAD, GLA_DV_HEAD), F32)],
        scratch_shapes=[pltpu.VMEM((GLA_DV_HEAD, GLA_DK_HEAD), F32)],
        compiler_params=_params("parallel", "parallel"), name="gla_prompt",
    )(qk, qk, vg, vg, decay, g_norm)


def _gla_sample_kernel(layer, owner_of_all, q_ref, k_ref, v_ref, r_ref, b_ref, s0_ref, gn_ref, *rest):
    o_ref, s_ref = rest[-2:]
    s_ref = _own_slab(s_ref, owner_of_all, layer)
    bt, c, _ = q_ref.shape
    row = lax.broadcasted_iota(jnp.int32, (c, c), 0)
    col = lax.broadcasted_iota(jnp.int32, (c, c), 1)
    causal = row >= col
    gn = gn_ref[...]

    def body(i, carry):
        q = q_ref[i] * (GLA_DK_HEAD ** -0.5)
        k = k_ref[i]
        v = v_ref[i].astype(F32)
        r = r_ref[i]
        b = b_ref[i]
        b_mid = b[c // 2:c // 2 + 1, :]
        b_end = b[c - 1:c, :]
        qa = q * jnp.exp(b - b_mid)
        ka = k * jnp.exp(b_mid - b)
        qb = (q * jnp.exp(b)).astype(BF16)
        k_end_t = (k * jnp.exp(b_end - b)).T
        dec_col = jnp.broadcast_to(jnp.exp(b_end), (c, GLA_DK)).T[:, 0:1]
        for h in range(GLA_HEADS):
            ks = slice(h * GLA_DK_HEAD, (h + 1) * GLA_DK_HEAD)
            vs = slice(h * GLA_DV_HEAD, (h + 1) * GLA_DV_HEAD)
            s0 = s0_ref[i, h]
            a = lax.dot_general(qa[:, ks], ka[:, ks], NT_DIMS, preferred_element_type=F32)
            a = jnp.where(causal, a, 0.0)
            o = jnp.dot(a, v[:, vs], preferred_element_type=F32)
            o = o + jnp.dot(qb[:, ks], s0.astype(BF16), preferred_element_type=F32)
            upd = jnp.dot(k_end_t[ks, :], v[:, vs], preferred_element_type=F32)
            s_ref[i, h] = s0 * dec_col[ks, :] + upd
            o_ref[i, :, vs] = _gla_gate_out(o, r[:, vs], gn)
        return carry

    lax.fori_loop(0, bt, body, 0, unroll=2)


def _gla_sample(qk, vg, decay, state, g_norm, layer, prev_state):
    b, l, _ = qk.shape
    bt = min(SAMPLE_BATCH_TILE, b)
    state_block = (bt, GLA_HEADS, GLA_DK_HEAD, GLA_DV_HEAD)
    in_specs = [pl.BlockSpec((bt, l, GLA_DK), lambda b: (b, 0, 0)),
                pl.BlockSpec((bt, l, GLA_DK), lambda b: (b, 0, 1)),
                pl.BlockSpec((bt, l, GLA_DV), lambda b: (b, 0, 0)),
                pl.BlockSpec((bt, l, GLA_DV), lambda b: (b, 0, 1)),
                pl.BlockSpec((bt, l, GLA_DK), lambda b: (b, 0, 0)),
                pl.BlockSpec((None,) + state_block, lambda b: (layer, b, 0, 0, 0)),
                pl.BlockSpec((None, 1, GLA_DV_HEAD), lambda b: (layer, 0, 0))]
    inputs = [qk, qk, vg, vg, decay, state, g_norm]
    spec, shape, extra_specs, extra_inputs = _stacked_out_specs(
        prev_state, state.shape[0], layer, state.shape[1:], state_block, lambda b: (b, 0, 0, 0))
    return pl.pallas_call(
        functools.partial(_gla_sample_kernel, layer, prev_state is None),
        grid=(b // bt,),
        in_specs=in_specs + extra_specs,
        out_specs=[pl.BlockSpec((bt, l, GLA_DV), lambda b: (b, 0, 0)), spec],
        out_shape=[jax.ShapeDtypeStruct((b, l, GLA_DV), BF16), shape],
        input_output_aliases={len(inputs): 1} if extra_inputs else {},
        compiler_params=_params("parallel"), name="gla_sample",
    )(*inputs, *extra_inputs)


def kernel(x_prompt, x_sample, cache_k, cache_v, state_gla, norm_mix, norm_ffn, norm_final,
           w_attn_in, w_attn_out, attn_sinks, w_gla_in, w_gla_gate_up, b_gla_gate, gla_out_norm,
           w_gla_out, w_ffn_in, w_ffn_out):
    bp, lp, _ = x_prompt.shape
    bs, ls, _ = x_sample.shape
    depth = norm_mix.shape[0]
    n_attn = cache_k.shape[0]
    yp = x_prompt.reshape(bp * lp, D_MODEL)
    ys = x_sample.reshape(bs * ls, D_MODEL)
    tp = min(TOKEN_TILE, bp * lp)
    ts = min(TOKEN_TILE, bs * ls)
    bias_p, bias_old, bias_new = _attn_bias_tables(ls)

    g_mix = norm_mix.reshape(depth, 1, D_MODEL)
    g_ffn = norm_ffn.reshape(depth, 1, D_MODEL)
    g_fin = norm_final.reshape(1, D_MODEL)
    w_attn_bf = w_attn_in.astype(BF16)
    w_attn_in_bf = jnp.concatenate(
        [_regroup_heads(w_attn_bf[:, :, :Q_DIM], 2, HEAD_DIM), w_attn_bf[:, :, Q_DIM:]], axis=2)
    w_attn_out_bf = _regroup_heads(w_attn_out.astype(BF16), 1, HEAD_DIM)
    sinks = _regroup_heads(attn_sinks, 1, 1)
    w_gla_in_t = jnp.transpose(w_gla_in, (0, 2, 1)).astype(BF16)
    w_fi = w_fo = w_gla_out_bf = None
    b_gate = b_gla_gate.reshape(-1, 1, GLA_DK)
    g_gla = gla_out_norm.reshape(-1, 1, GLA_DV_HEAD)
    cache_kt = jnp.transpose(cache_k, (0, 1, 3, 4, 2)).reshape(n_attn, bs, KV_DIM, WINDOW)
    cache_vt = jnp.transpose(cache_v, (0, 1, 3, 4, 2)).reshape(n_attn, bs, KV_DIM, WINDOW)

    kp_l, vp_l, sp_l = [], [], []
    new_kv, new_state = None, None
    for i in range(depth):
        j = i // 2
        if i % 2 == 0:
            w_mix, lm = w_attn_out_bf, j
            (qkv_p,) = _norm_matmul(yp, g_mix, [w_attn_in_bf], (i, j), tp)
            (qkv_s,) = _norm_matmul(ys, g_mix, [w_attn_in_bf], (i, j), ts)
            qkv_p = qkv_p.reshape(bp, lp, Q_DIM + 2 * KV_DIM)
            qkv_s = qkv_s.reshape(bs, ls, Q_DIM + 2 * KV_DIM)
            cast = [(w_gla_out, j)] + ([(w_ffn_in, 0), (w_ffn_out, 0)] if i == 0 else [])
            op, w_gla_out_bf, *w_ffn0 = _attn_prompt(qkv_p, sinks[j] * LOG2E, bias_p, cast)
            if w_ffn0:
                w_fi, w_fo = w_ffn0
            sink_col = jnp.repeat(sinks[j], ls).reshape(N_HEADS * ls, 1)
            os_, *new_kv = _attn_sample(qkv_s, cache_kt, cache_vt, j, bias_old, bias_new, sink_col,
                                        new_kv)
            kv_shape = (bp, WINDOW, N_KV_HEADS, HEAD_DIM)
            kp_l.append(qkv_p[:, lp - WINDOW:, Q_DIM:Q_DIM + KV_DIM].reshape(kv_shape))
            vp_l.append(qkv_p[:, lp - WINDOW:, Q_DIM + KV_DIM:].reshape(kv_shape))
        else:
            w_mix, lm = w_gla_out_bf[None], 0
            gla_p = _gla_proj(yp, g_mix, w_gla_in_t, w_gla_gate_up, b_gate, (i, j), GLA_CHUNK, tp)
            gla_s = _gla_proj(ys, g_mix, w_gla_in_t, w_gla_gate_up, b_gate, (i, j), ls, ts)
            op, sp = _gla_prompt(*[t.reshape(bp, lp, -1) for t in gla_p], g_gla, j)
            os_, new_state = _gla_sample(*[t.reshape(bs, ls, -1) for t in gla_s], state_gla, g_gla, j,
                                         new_state)
            sp_l.append(sp)
        last = i == depth - 1
        yp, *w_next = _mix_out_ffn(yp, op.reshape(bp * lp, D_MODEL), w_mix, g_ffn, w_fi, w_fo,
                                   g_fin, (lm, i), last, min(FFN_TOKEN_TILE, bp * lp),
                                   [] if last else [(w_ffn_in, i + 1), (w_ffn_out, i + 1)])
        (ys,) = _mix_out_ffn(ys, os_.reshape(bs * ls, D_MODEL), w_mix, g_ffn, w_fi, w_fo,
                             g_fin, (lm, i), last, min(FFN_TOKEN_TILE, bs * ls))
        if w_next:
            w_fi, w_fo = w_next

    def window_major(xt):
        return jnp.transpose(xt.reshape(n_attn, bs, N_KV_HEADS, HEAD_DIM, WINDOW), (0, 1, 4, 2, 3))

    return (yp.reshape(bp, lp, D_MODEL), ys.reshape(bs, ls, D_MODEL),
            jnp.stack(kp_l), jnp.stack(vp_l), jnp.stack(sp_l),
            window_major(new_kv[0]), window_major(new_kv[1]), new_state)
```

```python
import functools

import jax
import jax.numpy as jnp
from jax import lax
from jax.experimental import pallas as pl
from jax.experimental.pallas import tpu as pltpu

F32 = jnp.float32
BF16 = jnp.bfloat16

D_MODEL = 1024
EPS = 1e-6
WINDOW = 128
HEAD_DIM = 64
N_HEADS = 16
N_KV_HEADS = 4
GROUP = 4
Q_DIM = 1024
KV_DIM = 256
GLA_HEADS = 4
GLA_DK = 512
GLA_DV = 1024
GLA_DK_HEAD = 128
GLA_DV_HEAD = 256
GLA_GATE_RANK = 16
GLA_GATE_NORMALIZER = 16.0
GLA_CHUNK = 64
D_FF = 2816
MASKED = -1e30
LOG2E = 1.4426950408889634

VMEM_LIMIT_BYTES = 58 * 1024 * 1024
LANES = 128
BF16_SUBLANES = 16
TOKEN_TILE = 1024
FFN_TOKEN_TILE = 1024
FFN_TILE = 256
ATTN_Q_BLOCKS = 8
GLA_GROUP = 4
GLA_HEADS_PER_STEP = 2
SAMPLE_BATCH_TILE = 8
NEW_KEY_PAD = 16

NT_DIMS = (((1,), (1,)), ((), ()))
TN_DIMS = (((0,), (0,)), ((), ()))


def _params(*sem):
    return pltpu.CompilerParams(dimension_semantics=sem, vmem_limit_bytes=VMEM_LIMIT_BYTES)


def _rms(x, g):
    r = lax.rsqrt(jnp.mean(x * x, axis=-1, keepdims=True) + EPS)
    return x * r * g


def _silu(x):
    return x / (1.0 + jnp.exp(-x))


def _norm_matmul_kernel(n, x_ref, g_ref, *refs):
    h = _rms(x_ref[...], g_ref[...]).astype(BF16)
    for w_ref, o_ref in zip(refs[:n], refs[n:]):
        o_ref[...] = jnp.dot(h, w_ref[...], preferred_element_type=F32)


def _norm_matmul(x, g, ws, layers, tm):
    t = x.shape[0]
    lg, lw = layers
    in_specs = [pl.BlockSpec((tm, D_MODEL), lambda i: (i, 0)),
                pl.BlockSpec((None, 1, D_MODEL), lambda i: (lg, 0, 0))]
    in_specs += [pl.BlockSpec((None,) + w.shape[1:], lambda i: (lw, 0, 0)) for w in ws]
    out_specs = [pl.BlockSpec((tm, w.shape[2]), lambda i: (i, 0)) for w in ws]
    out_shape = [jax.ShapeDtypeStruct((t, w.shape[2]), F32) for w in ws]
    return pl.pallas_call(
        functools.partial(_norm_matmul_kernel, len(ws)),
        grid=(t // tm,), in_specs=in_specs, out_specs=out_specs, out_shape=out_shape,
        compiler_params=_params("parallel"), name="norm_matmul",
    )(x, g, *ws)


def _slab_rows(total, steps):
    rows = BF16_SUBLANES
    while total % rows or total // rows > steps:
        rows += BF16_SUBLANES
    return rows


def _cast_specs(sources, steps, step_of):
    in_specs, out_specs, out_shape = [], [], []
    for w_all, layer in sources:
        rows = _slab_rows(w_all.shape[1], steps)
        last = w_all.shape[1] // rows - 1
        slab = lambda *g, last=last: (jnp.minimum(step_of(*g), last), 0)
        in_specs.append(pl.BlockSpec((None, rows, w_all.shape[2]),
                                     lambda *g, slab=slab, layer=layer: (layer,) + slab(*g)))
        out_specs.append(pl.BlockSpec((rows, w_all.shape[2]), slab))
        out_shape.append(jax.ShapeDtypeStruct(w_all.shape[1:], BF16))
    return in_specs, out_specs, out_shape


def _cast_slabs(src_refs, dst_refs):
    for src, dst in zip(src_refs, dst_refs):
        dst[...] = src[...].astype(BF16)


def _ffn_kernel(final_norm, n_cast, x_ref, a_ref, wm_ref, g_ref, wi_hbm, wo_hbm, gf_ref, *rest):
    cast_src, rest = rest[:n_cast], rest[n_cast:]
    o_ref, cast_dst = rest[0], rest[1:1 + n_cast]
    act_ref, wi_ref, wo_ref, sems = rest[1 + n_cast:]
    fetch_in = pltpu.make_async_copy(wi_hbm, wi_ref, sems.at[0])
    fetch_out = pltpu.make_async_copy(wo_hbm, wo_ref, sems.at[1])

    def tile(fetching):
        if fetching:
            fetch_in.start()
            fetch_out.start()
        _cast_slabs(cast_src, cast_dst)
        y = x_ref[...] + jnp.dot(a_ref[...], wm_ref[...], preferred_element_type=F32)
        h = _rms(y, g_ref[...]).astype(BF16)
        if fetching:
            fetch_in.wait()
        for j in range(D_FF // FFN_TILE):
            gate = jnp.dot(h, wi_ref[:, j * FFN_TILE:(j + 1) * FFN_TILE], preferred_element_type=F32)
            up = jnp.dot(h, wi_ref[:, D_FF + j * FFN_TILE:D_FF + (j + 1) * FFN_TILE],
                         preferred_element_type=F32)
            act_ref[:, j * FFN_TILE:(j + 1) * FFN_TILE] = (_silu(gate) * up).astype(BF16)
        if fetching:
            fetch_out.wait()
        y = y + jnp.dot(act_ref[...], wo_ref[...], preferred_element_type=F32)
        if final_norm:
            y = _rms(y, gf_ref[...])
        o_ref[...] = y

    first = pl.program_id(0) == 0
    pl.when(first)(lambda: tile(True))
    pl.when(jnp.logical_not(first))(lambda: tile(False))


def _resident(block_shape, index_map):
    return pl.BlockSpec(block_shape, index_map, pipeline_mode=pl.Buffered(1))


def _mix_out_ffn(x, a, w_mix, g, w_in, w_out, g_final, layers, final_norm, tm, cast=()):
    t = x.shape[0]
    lm, lf = layers
    steps = t // tm
    row = lambda i: (i, 0)
    cast_in, cast_out, cast_shape = _cast_specs(cast, steps, lambda i: i)
    in_specs = [pl.BlockSpec((tm, D_MODEL), row),
                pl.BlockSpec((tm, D_MODEL), row),
                _resident((None, D_MODEL, D_MODEL), lambda i: (lm, 0, 0)),
                _resident((None, 1, D_MODEL), lambda i: (lf, 0, 0)),
                pl.BlockSpec(memory_space=pl.ANY),
                pl.BlockSpec(memory_space=pl.ANY),
                _resident((1, D_MODEL), lambda i: (0, 0))]
    return pl.pallas_call(
        functools.partial(_ffn_kernel, final_norm, len(cast)),
        grid=(steps,),
        in_specs=in_specs + cast_in,
        out_specs=[pl.BlockSpec((tm, D_MODEL), row)] + cast_out,
        out_shape=[jax.ShapeDtypeStruct((t, D_MODEL), F32)] + cast_shape,
        scratch_shapes=[pltpu.VMEM((tm, D_FF), BF16),
                        pltpu.VMEM((D_MODEL, 2 * D_FF), BF16),
                        pltpu.VMEM((D_FF, D_MODEL), BF16),
                        pltpu.SemaphoreType.DMA((2,))],
        compiler_params=_params("arbitrary"), name="mix_out_ffn",
    )(x, a, w_mix, g, w_in, w_out, g_final, *[w for w, _ in cast])


HALVES = LANES // HEAD_DIM
HEAD_ORDER = tuple((HALVES * (p // (HALVES * GROUP)) + p % HALVES) * GROUP + (p % (HALVES * GROUP)) // HALVES
                   for p in range(N_HEADS))


def _regroup_heads(x, axis, width):
    lead, tail = x.shape[:axis], x.shape[axis + 1:]
    x = x.reshape(lead + (N_KV_HEADS // HALVES, HALVES, GROUP, width) + tail)
    x = jnp.swapaxes(x, axis + 1, axis + 2)
    return x.reshape(lead + (N_HEADS * width,) + tail)


def _half_masks(rows):
    low = lax.broadcasted_iota(jnp.int32, (rows, LANES), 1) < HEAD_DIM
    return low, jnp.logical_not(low)


def _attn_prompt_kernel(n_cast, sink_ref, q_ref, kp_ref, kc_ref, vp_ref, vc_ref, bias_ref, *rest):
    o_ref = rest[n_cast]
    _cast_slabs(rest[:n_cast], rest[n_cast + 1:])
    step = pl.program_id(1)
    q_all = (q_ref[0] * (LOG2E * HEAD_DIM ** -0.5)).astype(BF16)
    k_all = jnp.concatenate([kp_ref[0], kc_ref[0]], axis=0).astype(BF16)
    v_f32 = jnp.concatenate([vp_ref[0], vc_ref[0]], axis=0)
    masks = _half_masks(WINDOW)
    key_masks = _half_masks(2 * WINDOW)
    zero = jnp.zeros((), BF16)
    half_ones = [jnp.where(mask, 1.0, 0.0).astype(BF16) for mask in key_masks]
    for qb in range(ATTN_Q_BLOCKS):
        rows = slice(qb * WINDOW, (qb + 1) * WINDOW)
        keys = slice(qb * WINDOW, (qb + 2) * WINDOW)
        table = 1 if qb > 0 else jnp.minimum(step, 1)
        for blk in range(KV_DIM // LANES):
            lanes = slice(blk * LANES, (blk + 1) * LANES)
            k2 = k_all[keys, lanes]
            v2 = v_f32[keys, lanes]
            v_pair = jnp.concatenate(
                [jnp.concatenate([jnp.where(key_masks[half], v2, 0.0).astype(BF16), half_ones[half]],
                                 axis=1)
                 for half in range(HALVES)], axis=0)
            for c in range(blk * GROUP, (blk + 1) * GROUP):
                qc = q_all[rows, c * LANES:(c + 1) * LANES]
                probs, sink_terms = [], []
                for half in range(HALVES):
                    p = HALVES * c + half
                    qm = jnp.where(masks[half], qc, zero)
                    s = lax.dot_general(qm, k2, NT_DIMS, preferred_element_type=F32)
                    s = s + bias_ref[table, p]
                    sink = sink_ref[p]
                    m = jnp.maximum(jnp.max(s, axis=-1, keepdims=True), sink)
                    probs.append(jnp.exp2(s - m).astype(BF16))
                    sink_terms.append(jnp.exp2(sink - m))
                ov = jnp.dot(jnp.concatenate(probs, axis=1), v_pair,
                             preferred_element_type=F32)
                denom = ov[:, LANES:] + jnp.where(
                    masks[0], *[jnp.broadcast_to(t, (WINDOW, LANES)) for t in sink_terms])
                o_ref[0, rows, c * LANES:(c + 1) * LANES] = (ov[:, :LANES] / denom).astype(BF16)


def _attn_prompt(qkv, sinks, bias, cast=()):
    b, l, _ = qkv.shape
    rows = ATTN_Q_BLOCKS * WINDOW
    steps = l // rows
    kcol, vcol = Q_DIM // KV_DIM, Q_DIM // KV_DIM + 1
    prev = lambda i: jnp.maximum(ATTN_Q_BLOCKS * i - 1, 0)
    cast_in, cast_out, cast_shape = _cast_specs(cast, b * steps, lambda b, i: b * steps + i)
    return pl.pallas_call(
        functools.partial(_attn_prompt_kernel, len(cast)),
        grid=(b, steps),
        in_specs=[pl.BlockSpec(memory_space=pltpu.SMEM),
                  pl.BlockSpec((1, rows, Q_DIM), lambda b, i: (b, i, 0)),
                  pl.BlockSpec((1, WINDOW, KV_DIM), lambda b, i: (b, prev(i), kcol)),
                  pl.BlockSpec((1, rows, KV_DIM), lambda b, i: (b, i, kcol)),
                  pl.BlockSpec((1, WINDOW, KV_DIM), lambda b, i: (b, prev(i), vcol)),
                  pl.BlockSpec((1, rows, KV_DIM), lambda b, i: (b, i, vcol)),
                  pl.BlockSpec(bias.shape, lambda b, i: (0, 0, 0, 0))] + cast_in,
        out_specs=[pl.BlockSpec((1, rows, Q_DIM), lambda b, i: (b, i, 0))] + cast_out,
        out_shape=[jax.ShapeDtypeStruct((b, l, Q_DIM), BF16)] + cast_shape,
        compiler_params=_params("arbitrary", "arbitrary"), name="attn_prompt",
    )(sinks, qkv, qkv, qkv, qkv, qkv, bias, *[w for w, _ in cast])


def _own_slab(ref, owner_of_all, layer):
    if not owner_of_all:
        return ref
    for other in range(ref.shape[0]):
        if other != layer:
            ref[other] = jnp.zeros(ref.shape[1:], ref.dtype)
    return ref.at[layer]


def _stacked_out_specs(prev, n_layers, layer, slab_shape, block, index):
    shape = jax.ShapeDtypeStruct((n_layers,) + slab_shape, F32)
    if prev is None:
        return pl.BlockSpec((n_layers,) + block, lambda b: (0,) + index(b)), shape, [], []
    spec = pl.BlockSpec((None,) + block, lambda b: (layer,) + index(b))
    return spec, shape, [pl.BlockSpec(memory_space=pl.ANY)], [prev]


def _attn_sample_kernel(layer, owner_of_all, q_ref, kn_ref, vn_ref, kt_ref, vt_ref, bo_ref, bn_ref,
                        sink_ref, *rest):
    o_ref, kto_ref, vto_ref = rest[-3:]
    kto_ref = _own_slab(kto_ref, owner_of_all, layer)
    vto_ref = _own_slab(vto_ref, owner_of_all, layer)
    bt, l, _ = q_ref.shape
    masks = _half_masks(l)
    keep = lax.broadcasted_iota(jnp.int32, (KV_DIM, WINDOW), 1) < WINDOW - l
    zero_block = jnp.zeros((l, LANES), F32)
    pad_rows = jnp.zeros((WINDOW - l, KV_DIM), F32)
    pad_keys = jnp.zeros((NEW_KEY_PAD - l, KV_DIM), F32)
    ones_old = jnp.ones((LANES, WINDOW), BF16)
    ones_new = jnp.ones((NEW_KEY_PAD, LANES), BF16)
    bias_old = bo_ref[...]
    bias_new = bn_ref[...]
    sink = sink_ref[...]

    def body(b, carry):
        q = q_ref[b] * (HEAD_DIM ** -0.5)
        pieces = []
        for p in range(N_HEADS):
            c = p // HALVES
            src = jnp.where(masks[p % HALVES], q[:, c * LANES:(c + 1) * LANES], 0.0)
            pieces.append(jnp.concatenate(
                [src, zero_block] if c // GROUP == 0 else [zero_block, src], axis=1))
        qt = jnp.concatenate(pieces, axis=0).astype(BF16)

        k_old, v_old = kt_ref[b], vt_ref[b]
        k_new, v_new = kn_ref[b], vn_ref[b]
        k_pad = jnp.concatenate([k_new, pad_keys], axis=0).astype(BF16)
        v_pad = jnp.concatenate([v_new, pad_keys], axis=0).astype(BF16)
        s_old = jnp.dot(qt, k_old.astype(BF16), preferred_element_type=F32) + bias_old
        s_new = lax.dot_general(qt, k_pad, NT_DIMS, preferred_element_type=F32) + bias_new
        m = jnp.maximum(sink, jnp.maximum(jnp.max(s_old, axis=-1, keepdims=True),
                                          jnp.max(s_new, axis=-1, keepdims=True)))
        p_old = jnp.exp(s_old - m).astype(BF16)
        p_new = jnp.exp(s_new - m).astype(BF16)
        v_old_1 = jnp.concatenate([v_old.astype(BF16), ones_old], axis=0)
        v_new_1 = jnp.concatenate([v_pad, ones_new], axis=1)
        o = lax.dot_general(p_old, v_old_1, NT_DIMS, preferred_element_type=F32)
        o = o + jnp.dot(p_new, v_new_1, preferred_element_type=F32)
        inv = 1.0 / (o[:, KV_DIM:] + jnp.exp(sink - m))
        cols = []
        for c in range(N_HEADS // HALVES):
            lanes = slice((c // GROUP) * LANES, (c // GROUP + 1) * LANES)
            parts = [o[(HALVES * c + half) * l:(HALVES * c + half + 1) * l, lanes]
                     * inv[(HALVES * c + half) * l:(HALVES * c + half + 1) * l, :]
                     for half in range(HALVES)]
            cols.append(jnp.where(masks[0], parts[0], parts[1]))
        o_ref[b] = jnp.concatenate(cols, axis=1).astype(BF16)

        for old, new, out_ref in ((k_old, k_new, kto_ref), (v_old, v_new, vto_ref)):
            shifted = pltpu.roll(old, WINDOW - l, axis=1)
            placed = jnp.concatenate([pad_rows, new], axis=0).T
            out_ref[b] = jnp.where(keep, shifted, placed)
        return carry

    lax.fori_loop(0, bt, body, 0, unroll=8)


def _attn_sample(qkv, cache_kt, cache_vt, layer, bias_old, bias_new, sink_col, prev_kv):
    b, l, _ = qkv.shape
    n_layers = cache_kt.shape[0]
    bt = min(SAMPLE_BATCH_TILE, b)
    kcol, vcol = Q_DIM // KV_DIM, Q_DIM // KV_DIM + 1
    const = lambda b: (0, 0)
    in_specs = [pl.BlockSpec((bt, l, Q_DIM), lambda b: (b, 0, 0)),
                pl.BlockSpec((bt, l, KV_DIM), lambda b: (b, 0, kcol)),
                pl.BlockSpec((bt, l, KV_DIM), lambda b: (b, 0, vcol)),
                pl.BlockSpec((None, bt, KV_DIM, WINDOW), lambda b: (layer, b, 0, 0)),
                pl.BlockSpec((None, bt, KV_DIM, WINDOW), lambda b: (layer, b, 0, 0)),
                pl.BlockSpec(bias_old.shape, const),
                pl.BlockSpec(bias_new.shape, const),
                pl.BlockSpec(sink_col.shape, const)]
    inputs = [qkv, qkv, qkv, cache_kt, cache_vt, bias_old, bias_new, sink_col]
    out_specs = [pl.BlockSpec((bt, l, Q_DIM), lambda b: (b, 0, 0))]
    out_shape = [jax.ShapeDtypeStruct((b, l, Q_DIM), BF16)]
    aliases = {}
    for prev in (prev_kv if prev_kv is not None else (None, None)):
        spec, shape, extra_specs, extra_inputs = _stacked_out_specs(
            prev, n_layers, layer, (b, KV_DIM, WINDOW), (bt, KV_DIM, WINDOW), lambda b: (b, 0, 0))
        if extra_inputs:
            aliases[len(inputs)] = len(out_specs)
        in_specs += extra_specs
        inputs += extra_inputs
        out_specs.append(spec)
        out_shape.append(shape)
    return pl.pallas_call(
        functools.partial(_attn_sample_kernel, layer, prev_kv is None),
        grid=(b // bt,), in_specs=in_specs, out_specs=out_specs, out_shape=out_shape,
        input_output_aliases=aliases,
        compiler_params=_params("parallel"), name="attn_sample",
    )(*inputs)


def _attn_bias_tables(l_sample):
    slopes = jnp.exp2(-8.0 * jnp.arange(1, N_HEADS + 1, dtype=F32) / N_HEADS)
    slopes = slopes[jnp.array(HEAD_ORDER)][:, None, None]
    t = jnp.arange(WINDOW, dtype=jnp.int32)[:, None]
    s = jnp.arange(2 * WINDOW, dtype=jnp.int32)[None, :]
    dist = t + WINDOW - s
    valid = (dist >= 0) & (dist <= WINDOW)
    later = jnp.where(valid[None], -slopes * dist.astype(F32)[None], MASKED)
    first = jnp.where((s >= WINDOW)[None], later, MASKED)
    prompt = jnp.stack([first, later]) * LOG2E
    tq = jnp.arange(l_sample, dtype=jnp.int32)[:, None]
    sk = jnp.arange(WINDOW + NEW_KEY_PAD, dtype=jnp.int32)[None, :]
    dist = tq + WINDOW - sk
    valid = (dist >= 0) & (dist <= WINDOW) & (sk < WINDOW + l_sample)
    sample = jnp.where(valid[None], -slopes * dist.astype(F32)[None], MASKED)
    sample = sample.reshape(N_HEADS * l_sample, WINDOW + NEW_KEY_PAD)
    return prompt, sample[:, :WINDOW], sample[:, WINDOW:]


def _log_sigmoid(x):
    return jnp.minimum(x, 0.0) - jnp.log(1.0 + jnp.exp(-jnp.abs(x)))


def _chunk_cumsum(x, c):
    pos = jnp.bitwise_and(lax.broadcasted_iota(jnp.int32, x.shape, 0), c - 1)
    shift = 1
    while shift < c:
        x = x + jnp.where(pos >= shift, pltpu.roll(x, shift, axis=0), 0.0)
        shift *= 2
    return x


def _gla_proj_kernel(chunk, x_ref, g_ref, wm_ref, wgd_ref, wg_ref, bg_ref, qk_ref, vg_ref, b_ref):
    h = _rms(x_ref[...], g_ref[...]).astype(BF16)
    gd = lax.dot_general(h, wgd_ref[...], NT_DIMS, preferred_element_type=F32)
    pre = jnp.dot(gd.astype(BF16), wg_ref[...].astype(BF16), preferred_element_type=F32) + bg_ref[...]
    b_ref[...] = _chunk_cumsum(_log_sigmoid(pre) / GLA_GATE_NORMALIZER, chunk)
    n_qk, n_qkv = 2 * GLA_DK, 2 * GLA_DK + GLA_DV
    proj = lambda rows: lax.dot_general(h, wm_ref[rows, :], NT_DIMS, preferred_element_type=F32)
    qk_ref[...] = proj(slice(0, n_qk))
    vg_ref[:, :GLA_DV] = proj(slice(n_qk, n_qkv)).astype(BF16)
    vg_ref[:, GLA_DV:] = _silu(proj(slice(n_qkv, n_qkv + GLA_DV))).astype(BF16)


def _gla_proj(x, g, w_in_t, w_gate, b_gate, layers, chunk, tm):
    t = x.shape[0]
    lg, lw = layers
    n_main = 2 * GLA_DK + 2 * GLA_DV
    return pl.pallas_call(
        functools.partial(_gla_proj_kernel, chunk),
        grid=(t // tm,),
        in_specs=[pl.BlockSpec((tm, D_MODEL), lambda i: (i, 0)),
                  pl.BlockSpec((None, 1, D_MODEL), lambda i: (lg, 0, 0)),
                  pl.BlockSpec((None, n_main, D_MODEL), lambda i: (lw, 0, 0)),
                  pl.BlockSpec((None, GLA_GATE_RANK, D_MODEL),
                               lambda i: (lw, n_main // GLA_GATE_RANK, 0)),
                  pl.BlockSpec((None, GLA_GATE_RANK, GLA_DK), lambda i: (lw, 0, 0)),
                  pl.BlockSpec((None, 1, GLA_DK), lambda i: (lw, 0, 0))],
        out_specs=[pl.BlockSpec((tm, 2 * GLA_DK), lambda i: (i, 0)),
                   pl.BlockSpec((tm, 2 * GLA_DV), lambda i: (i, 0)),
                   pl.BlockSpec((tm, GLA_DK), lambda i: (i, 0))],
        out_shape=[jax.ShapeDtypeStruct((t, 2 * GLA_DK), F32),
                   jax.ShapeDtypeStruct((t, 2 * GLA_DV), BF16),
                   jax.ShapeDtypeStruct((t, GLA_DK), F32)],
        compiler_params=_params("parallel"), name="gla_proj",
    )(x, g, w_in_t, w_in_t, w_gate, b_gate)


def _gla_gate_out(o, gate, gn):
    return (_rms(o, gn) * gate).astype(BF16)


def _gla_prompt_kernel(q_ref, k_ref, v_ref, r_ref, b_ref, gn_ref, o_ref, s_ref, st_ref):
    c = GLA_CHUNK
    rows = GLA_GROUP * c
    n_groups = q_ref.shape[1] // rows
    row = lax.broadcasted_iota(jnp.int32, (rows, rows), 0)
    col = lax.broadcasted_iota(jnp.int32, (rows, rows), 1)
    causal = (row >= col) & (col >= row - jnp.bitwise_and(row, c - 1))
    gn = gn_ref[...]

    for hd in range(GLA_HEADS_PER_STEP):
        ks = slice(hd * GLA_DK_HEAD, (hd + 1) * GLA_DK_HEAD)
        vs = slice(hd * GLA_DV_HEAD, (hd + 1) * GLA_DV_HEAD)
        st_ref[...] = jnp.zeros_like(st_ref)

        def body(g, carry):
            sl = pl.ds(pl.multiple_of(g * rows, rows), rows)
            q = q_ref[0, sl, ks] * (GLA_DK_HEAD ** -0.5)
            k = k_ref[0, sl, ks]
            v_bf = v_ref[0, sl, vs]
            b = b_ref[0, sl, ks]
            per_chunk = lambda r0: jnp.concatenate(
                [jnp.broadcast_to(b[j * c + r0:j * c + r0 + 1, :], (c, GLA_DK_HEAD))
                 for j in range(GLA_GROUP)], axis=0)
            b_mid = per_chunk(c // 2)
            b_end = per_chunk(c - 1)
            qa = (q * jnp.exp(b - b_mid)).astype(BF16)
            ka = (k * jnp.exp(b_mid - b)).astype(BF16)
            a = lax.dot_general(qa, ka, NT_DIMS, preferred_element_type=F32)
            a = jnp.where(causal, a, 0.0).astype(BF16)
            o_intra = jnp.dot(a, v_bf, preferred_element_type=F32)
            qb = (q * jnp.exp(b)).astype(BF16)
            k_end = (k * jnp.exp(b_end - b)).astype(BF16)
            st = st_ref[...]
            o_inter = []
            for j in range(GLA_GROUP):
                rs = slice(j * c, (j + 1) * c)
                o_inter.append(lax.dot_general(qb[rs], st.astype(BF16), NT_DIMS,
                                               preferred_element_type=F32))
                upd = lax.dot_general(v_bf[rs], k_end[rs], TN_DIMS, preferred_element_type=F32)
                st = st * jnp.exp(b[(j + 1) * c - 1:(j + 1) * c, :]) + upd
            st_ref[...] = st
            o = o_intra + jnp.concatenate(o_inter, axis=0)
            o_ref[0, sl, vs] = _gla_gate_out(o, r_ref[0, sl, vs], gn)
            return carry

        lax.fori_loop(0, n_groups, body, 0, unroll=8)
        s_ref[0, hd] = st_ref[...].T


def _gla_prompt(qk, vg, decay, g_norm, layer):
    b, l, _ = qk.shape
    hp = GLA_HEADS_PER_STEP
    dk, dv = hp * GLA_DK_HEAD, hp * GLA_DV_HEAD
    kcol = GLA_DK // dk
    gcol = GLA_DV // dv
    return pl.pallas_call(
        _gla_prompt_kernel,
        grid=(b, GLA_HEADS // hp),
        in_specs=[pl.BlockSpec((1, l, dk), lambda b, h: (b, 0, h)),
                  pl.BlockSpec((1, l, dk), lambda b, h: (b, 0, kcol + h)),
                  pl.BlockSpec((1, l, dv), lambda b, h: (b, 0, h)),
                  pl.BlockSpec((1, l, dv), lambda b, h: (b, 0, gcol + h)),
                  pl.BlockSpec((1, l, dk), lambda b, h: (b, 0, h)),
                  pl.BlockSpec((None, 1, GLA_DV_HEAD), lambda b, h: (layer, 0, 0))],
        out_specs=[pl.BlockSpec((1, l, dv), lambda b, h: (b, 0, h)),
                   pl.BlockSpec((1, hp, GLA_DK_HEAD, GLA_DV_HEAD), lambda b, h: (b, h, 0, 0))],
        out_shape=[jax.ShapeDtypeStruct((b, l, GLA_DV), BF16),
                   jax.ShapeDtypeStruct((b, GLA_HEADS, GLA_DK_HEAD, GLA_DV_HEAD), F32)],
        scratch_shapes=[pltpu.VMEM((GLA_DV_HEAD, GLA_DK_HEAD), F32)],
        compiler_params=_params("parallel", "parallel"), name="gla_prompt",
    )(qk, qk, vg, vg, decay, g_norm)


def _gla_sample_kernel(layer, owner_of_all, q_ref, k_ref, v_ref, r_ref, b_ref, s0_ref, gn_ref, *rest):
    o_ref, s_ref = rest[-2:]
    s_ref = _own_slab(s_ref, owner_of_all, layer)
    bt, c, _ = q_ref.shape
    row = lax.broadcasted_iota(jnp.int32, (c, c), 0)
    col = lax.broadcasted_iota(jnp.int32, (c, c), 1)
    causal = row >= col
    gn = gn_ref[...]

    def body(i, carry):
        q = q_ref[i] * (GLA_DK_HEAD ** -0.5)
        k = k_ref[i]
        v = v_ref[i].astype(F32)
        r = r_ref[i]
        b = b_ref[i]
        b_mid = b[c // 2:c // 2 + 1, :]
        b_end = b[c - 1:c, :]
        qa = q * jnp.exp(b - b_mid)
        ka = k * jnp.exp(b_mid - b)
        qb = (q * jnp.exp(b)).astype(BF16)
        k_end_t = (k * jnp.exp(b_end - b)).T
        dec_col = jnp.broadcast_to(jnp.exp(b_end), (c, GLA_DK)).T[:, 0:1]
        for h in range(GLA_HEADS):
            ks = slice(h * GLA_DK_HEAD, (h + 1) * GLA_DK_HEAD)
            vs = slice(h * GLA_DV_HEAD, (h + 1) * GLA_DV_HEAD)
            s0 = s0_ref[i, h]
            a = lax.dot_general(qa[:, ks], ka[:, ks], NT_DIMS, preferred_element_type=F32)
            a = jnp.where(causal, a, 0.0)
            o = jnp.dot(a, v[:, vs], preferred_element_type=F32)
            o = o + jnp.dot(qb[:, ks], s0.astype(BF16), preferred_element_type=F32)
            upd = jnp.dot(k_end_t[ks, :], v[:, vs], preferred_element_type=F32)
            s_ref[i, h] = s0 * dec_col[ks, :] + upd
            o_ref[i, :, vs] = _gla_gate_out(o, r[:, vs], gn)
        return carry

    lax.fori_loop(0, bt, body, 0, unroll=2)


def _gla_sample(qk, vg, decay, state, g_norm, layer, prev_state):
    b, l, _ = qk.shape
    bt = min(SAMPLE_BATCH_TILE, b)
    state_block = (bt, GLA_HEADS, GLA_DK_HEAD, GLA_DV_HEAD)
    in_specs = [pl.BlockSpec((bt, l, GLA_DK), lambda b: (b, 0, 0)),
                pl.BlockSpec((bt, l, GLA_DK), lambda b: (b, 0, 1)),
                pl.BlockSpec((bt, l, GLA_DV), lambda b: (b, 0, 0)),
                pl.BlockSpec((bt, l, GLA_DV), lambda b: (b, 0, 1)),
                pl.BlockSpec((bt, l, GLA_DK), lambda b: (b, 0, 0)),
                pl.BlockSpec((None,) + state_block, lambda b: (layer, b, 0, 0, 0)),
                pl.BlockSpec((None, 1, GLA_DV_HEAD), lambda b: (layer, 0, 0))]
    inputs = [qk, qk, vg, vg, decay, state, g_norm]
    spec, shape, extra_specs, extra_inputs = _stacked_out_specs(
        prev_state, state.shape[0], layer, state.shape[1:], state_block, lambda b: (b, 0, 0, 0))
    return pl.pallas_call(
        functools.partial(_gla_sample_kernel, layer, prev_state is None),
        grid=(b // bt,),
        in_specs=in_specs + extra_specs,
        out_specs=[pl.BlockSpec((bt, l, GLA_DV), lambda b: (b, 0, 0)), spec],
        out_shape=[jax.ShapeDtypeStruct((b, l, GLA_DV), BF16), shape],
        input_output_aliases={len(inputs): 1} if extra_inputs else {},
        compiler_params=_params("parallel"), name="gla_sample",
    )(*inputs, *extra_inputs)


def kernel(x_prompt, x_sample, cache_k, cache_v, state_gla, norm_mix, norm_ffn, norm_final,
           w_attn_in, w_attn_out, attn_sinks, w_gla_in, w_gla_gate_up, b_gla_gate, gla_out_norm,
           w_gla_out, w_ffn_in, w_ffn_out):
    bp, lp, _ = x_prompt.shape
    bs, ls, _ = x_sample.shape
    depth = norm_mix.shape[0]
    n_attn = cache_k.shape[0]
    yp = x_prompt.reshape(bp * lp, D_MODEL)
    ys = x_sample.reshape(bs * ls, D_MODEL)
    tp = min(TOKEN_TILE, bp * lp)
    ts = min(TOKEN_TILE, bs * ls)
    bias_p, bias_old, bias_new = _attn_bias_tables(ls)

    g_mix = norm_mix.reshape(depth, 1, D_MODEL)
    g_ffn = norm_ffn.reshape(depth, 1, D_MODEL)
    g_fin = norm_final.reshape(1, D_MODEL)
    w_attn_bf = w_attn_in.astype(BF16)
    w_attn_in_bf = jnp.concatenate(
        [_regroup_heads(w_attn_bf[:, :, :Q_DIM], 2, HEAD_DIM), w_attn_bf[:, :, Q_DIM:]], axis=2)
    w_attn_out_bf = _regroup_heads(w_attn_out.astype(BF16), 1, HEAD_DIM)
    sinks = _regroup_heads(attn_sinks, 1, 1)
    w_gla_in_t = jnp.transpose(w_gla_in, (0, 2, 1)).astype(BF16)
    w_fi = w_fo = w_gla_out_bf = None
    b_gate = b_gla_gate.reshape(-1, 1, GLA_DK)
    g_gla = gla_out_norm.reshape(-1, 1, GLA_DV_HEAD)
    cache_kt = jnp.transpose(cache_k, (0, 1, 3, 4, 2)).reshape(n_attn, bs, KV_DIM, WINDOW)
    cache_vt = jnp.transpose(cache_v, (0, 1, 3, 4, 2)).reshape(n_attn, bs, KV_DIM, WINDOW)

    kp_l, vp_l, sp_l = [], [], []
    new_kv, new_state = None, None
    for i in range(depth):
        j = i // 2
        if i % 2 == 0:
            w_mix, lm = w_attn_out_bf, j
            (qkv_p,) = _norm_matmul(yp, g_mix, [w_attn_in_bf], (i, j), tp)
            (qkv_s,) = _norm_matmul(ys, g_mix, [w_attn_in_bf], (i, j), ts)
            qkv_p = qkv_p.reshape(bp, lp, Q_DIM + 2 * KV_DIM)
            qkv_s = qkv_s.reshape(bs, ls, Q_DIM + 2 * KV_DIM)
            cast = [(w_gla_out, j)] + ([(w_ffn_in, 0), (w_ffn_out, 0)] if i == 0 else [])
            op, w_gla_out_bf, *w_ffn0 = _attn_prompt(qkv_p, sinks[j] * LOG2E, bias_p, cast)
            if w_ffn0:
                w_fi, w_fo = w_ffn0
            sink_col = jnp.repeat(sinks[j], ls).reshape(N_HEADS * ls, 1)
            os_, *new_kv = _attn_sample(qkv_s, cache_kt, cache_vt, j, bias_old, bias_new, sink_col,
                                        new_kv)
            kv_shape = (bp, WINDOW, N_KV_HEADS, HEAD_DIM)
            kp_l.append(qkv_p[:, lp - WINDOW:, Q_DIM:Q_DIM + KV_DIM].reshape(kv_shape))
            vp_l.append(qkv_p[:, lp - WINDOW:, Q_DIM + KV_DIM:].reshape(kv_shape))
        else:
            w_mix, lm = w_gla_out_bf[None], 0
            gla_p = _gla_proj(yp, g_mix, w_gla_in_t, w_gla_gate_up, b_gate, (i, j), GLA_CHUNK, tp)
            gla_s = _gla_proj(ys, g_mix, w_gla_in_t, w_gla_gate_up, b_gate, (i, j), ls, ts)
            op, sp = _gla_prompt(*[t.reshape(bp, lp, -1) for t in gla_p], g_gla, j)
            os_, new_state = _gla_sample(*[t.reshape(bs, ls, -1) for t in gla_s], state_gla, g_gla, j,
                                         new_state)
            sp_l.append(sp)
        last = i == depth - 1
        yp, *w_next = _mix_out_ffn(yp, op.reshape(bp * lp, D_MODEL), w_mix, g_ffn, w_fi, w_fo,
                                   g_fin, (lm, i), last, min(FFN_TOKEN_TILE, bp * lp),
                                   [] if last else [(w_ffn_in, i + 1), (w_ffn_out, i + 1)])
        (ys,) = _mix_out_ffn(ys, os_.reshape(bs * ls, D_MODEL), w_mix, g_ffn, w_fi, w_fo,
                             g_fin, (lm, i), last, min(FFN_TOKEN_TILE, bs * ls))
        if w_next:
            w_fi, w_fo = w_next

    def window_major(xt):
        return jnp.transpose(xt.reshape(n_attn, bs, N_KV_HEADS, HEAD_DIM, WINDOW), (0, 1, 4, 2, 3))

    return (yp.reshape(bp, lp, D_MODEL), ys.reshape(bs, ls, D_MODEL),
            jnp.stack(kp_l), jnp.stack(vp_l), jnp.stack(sp_l),
            window_major(new_kv[0]), window_major(new_kv[1]), new_state)
```

```python
import functools

import jax
import jax.numpy as jnp
from jax import lax
from jax.experimental import pallas as pl
from jax.experimental.pallas import tpu as pltpu

F32 = jnp.float32
BF16 = jnp.bfloat16

D_MODEL = 1024
EPS = 1e-6
WINDOW = 128
HEAD_DIM = 64
N_HEADS = 16
N_KV_HEADS = 4
GROUP = 4
Q_DIM = 1024
KV_DIM = 256
GLA_HEADS = 4
GLA_DK = 512
GLA_DV = 1024
GLA_DK_HEAD = 128
GLA_DV_HEAD = 256
GLA_GATE_RANK = 16
GLA_GATE_NORMALIZER = 16.0
GLA_CHUNK = 64
D_FF = 2816
MASKED = -1e30
LOG2E = 1.4426950408889634

VMEM_LIMIT_BYTES = 58 * 1024 * 1024
LANES = 128
BF16_SUBLANES = 16
TOKEN_TILE = 1024
FFN_TOKEN_TILE = 1024
FFN_TILE = 256
ATTN_Q_BLOCKS = 8
GLA_GROUP = 4
GLA_HEADS_PER_STEP = 2
SAMPLE_BATCH_TILE = 8
STATE_RING_SLOTS = 3
NEW_KEY_PAD = 16

NT_DIMS = (((1,), (1,)), ((), ()))
TN_DIMS = (((0,), (0,)), ((), ()))


def _params(*sem):
    return pltpu.CompilerParams(dimension_semantics=sem, vmem_limit_bytes=VMEM_LIMIT_BYTES)


def _rms(x, g):
    r = lax.rsqrt(jnp.mean(x * x, axis=-1, keepdims=True) + EPS)
    return x * r * g


def _silu(x):
    return x / (1.0 + jnp.exp(-x))


def _norm_matmul_kernel(n, x_ref, g_ref, *refs):
    h = _rms(x_ref[...], g_ref[...]).astype(BF16)
    for w_ref, o_ref in zip(refs[:n], refs[n:]):
        o_ref[...] = jnp.dot(h, w_ref[...], preferred_element_type=F32)


def _norm_matmul(x, g, ws, layers, tm):
    t = x.shape[0]
    lg, lw = layers
    in_specs = [pl.BlockSpec((tm, D_MODEL), lambda i: (i, 0)),
                pl.BlockSpec((None, 1, D_MODEL), lambda i: (lg, 0, 0))]
    in_specs += [pl.BlockSpec((None,) + w.shape[1:], lambda i: (lw, 0, 0)) for w in ws]
    out_specs = [pl.BlockSpec((tm, w.shape[2]), lambda i: (i, 0)) for w in ws]
    out_shape = [jax.ShapeDtypeStruct((t, w.shape[2]), F32) for w in ws]
    return pl.pallas_call(
        functools.partial(_norm_matmul_kernel, len(ws)),
        grid=(t // tm,), in_specs=in_specs, out_specs=out_specs, out_shape=out_shape,
        compiler_params=_params("parallel"), name="norm_matmul",
    )(x, g, *ws)


def _slab_rows(total, steps):
    rows = BF16_SUBLANES
    while total % rows or total // rows > steps:
        rows += BF16_SUBLANES
    return rows


def _cast_specs(sources, steps, step_of):
    in_specs, out_specs, out_shape = [], [], []
    for w_all, layer in sources:
        rows = _slab_rows(w_all.shape[1], steps)
        last = w_all.shape[1] // rows - 1
        slab = lambda *g, last=last: (jnp.minimum(step_of(*g), last), 0)
        in_specs.append(pl.BlockSpec((None, rows, w_all.shape[2]),
                                     lambda *g, slab=slab, layer=layer: (layer,) + slab(*g)))
        out_specs.append(pl.BlockSpec((rows, w_all.shape[2]), slab))
        out_shape.append(jax.ShapeDtypeStruct(w_all.shape[1:], BF16))
    return in_specs, out_specs, out_shape


def _cast_slabs(src_refs, dst_refs):
    for src, dst in zip(src_refs, dst_refs):
        dst[...] = src[...].astype(BF16)


def _ffn_kernel(final_norm, n_cast, x_ref, a_ref, wm_ref, g_ref, wi_hbm, wo_hbm, gf_ref, *rest):
    cast_src, rest = rest[:n_cast], rest[n_cast:]
    o_ref, cast_dst = rest[0], rest[1:1 + n_cast]
    act_ref, wi_ref, wo_ref, sems = rest[1 + n_cast:]
    fetch_in = pltpu.make_async_copy(wi_hbm, wi_ref, sems.at[0])
    fetch_out = pltpu.make_async_copy(wo_hbm, wo_ref, sems.at[1])

    def tile(fetching):
        if fetching:
            fetch_in.start()
            fetch_out.start()
        _cast_slabs(cast_src, cast_dst)
        y = x_ref[...] + jnp.dot(a_ref[...], wm_ref[...], preferred_element_type=F32)
        h = _rms(y, g_ref[...]).astype(BF16)
        if fetching:
            fetch_in.wait()
        for j in range(D_FF // FFN_TILE):
            gate = jnp.dot(h, wi_ref[:, j * FFN_TILE:(j + 1) * FFN_TILE], preferred_element_type=F32)
            up = jnp.dot(h, wi_ref[:, D_FF + j * FFN_TILE:D_FF + (j + 1) * FFN_TILE],
                         preferred_element_type=F32)
            act_ref[:, j * FFN_TILE:(j + 1) * FFN_TILE] = (_silu(gate) * up).astype(BF16)
        if fetching:
            fetch_out.wait()
        y = y + jnp.dot(act_ref[...], wo_ref[...], preferred_element_type=F32)
        if final_norm:
            y = _rms(y, gf_ref[...])
        o_ref[...] = y

    first = pl.program_id(0) == 0
    pl.when(first)(lambda: tile(True))
    pl.when(jnp.logical_not(first))(lambda: tile(False))


def _resident(block_shape, index_map):
    return pl.BlockSpec(block_shape, index_map, pipeline_mode=pl.Buffered(1))


def _mix_out_ffn(x, a, w_mix, g, w_in, w_out, g_final, layers, final_norm, tm, cast=()):
    t = x.shape[0]
    lm, lf = layers
    steps = t // tm
    row = lambda i: (i, 0)
    cast_in, cast_out, cast_shape = _cast_specs(cast, steps, lambda i: i)
    in_specs = [pl.BlockSpec((tm, D_MODEL), row),
                pl.BlockSpec((tm, D_MODEL), row),
                _resident((None, D_MODEL, D_MODEL), lambda i: (lm, 0, 0)),
                _resident((None, 1, D_MODEL), lambda i: (lf, 0, 0)),
                pl.BlockSpec(memory_space=pl.ANY),
                pl.BlockSpec(memory_space=pl.ANY),
                _resident((1, D_MODEL), lambda i: (0, 0))]
    return pl.pallas_call(
        functools.partial(_ffn_kernel, final_norm, len(cast)),
        grid=(steps,),
        in_specs=in_specs + cast_in,
        out_specs=[pl.BlockSpec((tm, D_MODEL), row)] + cast_out,
        out_shape=[jax.ShapeDtypeStruct((t, D_MODEL), F32)] + cast_shape,
        scratch_shapes=[pltpu.VMEM((tm, D_FF), BF16),
                        pltpu.VMEM((D_MODEL, 2 * D_FF), BF16),
                        pltpu.VMEM((D_FF, D_MODEL), BF16),
                        pltpu.SemaphoreType.DMA((2,))],
        compiler_params=_params("arbitrary"), name="mix_out_ffn",
    )(x, a, w_mix, g, w_in, w_out, g_final, *[w for w, _ in cast])


HALVES = LANES // HEAD_DIM
HEAD_ORDER = tuple((HALVES * (p // (HALVES * GROUP)) + p % HALVES) * GROUP + (p % (HALVES * GROUP)) // HALVES
                   for p in range(N_HEADS))


def _regroup_heads(x, axis, width):
    lead, tail = x.shape[:axis], x.shape[axis + 1:]
    x = x.reshape(lead + (N_KV_HEADS // HALVES, HALVES, GROUP, width) + tail)
    x = jnp.swapaxes(x, axis + 1, axis + 2)
    return x.reshape(lead + (N_HEADS * width,) + tail)


def _half_masks(rows):
    low = lax.broadcasted_iota(jnp.int32, (rows, LANES), 1) < HEAD_DIM
    return low, jnp.logical_not(low)


def _attn_prompt_kernel(n_cast, sink_ref, q_ref, kp_ref, kc_ref, vp_ref, vc_ref, bias_ref, *rest):
    o_ref = rest[n_cast]
    _cast_slabs(rest[:n_cast], rest[n_cast + 1:])
    step = pl.program_id(1)
    q_all = (q_ref[0] * (LOG2E * HEAD_DIM ** -0.5)).astype(BF16)
    k_all = jnp.concatenate([kp_ref[0], kc_ref[0]], axis=0).astype(BF16)
    v_f32 = jnp.concatenate([vp_ref[0], vc_ref[0]], axis=0)
    masks = _half_masks(WINDOW)
    key_masks = _half_masks(2 * WINDOW)
    zero = jnp.zeros((), BF16)
    half_ones = [jnp.where(mask, 1.0, 0.0).astype(BF16) for mask in key_masks]
    for qb in range(ATTN_Q_BLOCKS):
        rows = slice(qb * WINDOW, (qb + 1) * WINDOW)
        keys = slice(qb * WINDOW, (qb + 2) * WINDOW)
        table = 1 if qb > 0 else jnp.minimum(step, 1)
        for blk in range(KV_DIM // LANES):
            lanes = slice(blk * LANES, (blk + 1) * LANES)
            k2 = k_all[keys, lanes]
            v2 = v_f32[keys, lanes]
            v_pair = jnp.concatenate(
                [jnp.concatenate([jnp.where(key_masks[half], v2, 0.0).astype(BF16), half_ones[half]],
                                 axis=1)
                 for half in range(HALVES)], axis=0)
            for c in range(blk * GROUP, (blk + 1) * GROUP):
                qc = q_all[rows, c * LANES:(c + 1) * LANES]
                probs, sink_terms = [], []
                for half in range(HALVES):
                    p = HALVES * c + half
                    qm = jnp.where(masks[half], qc, zero)
                    s = lax.dot_general(qm, k2, NT_DIMS, preferred_element_type=F32)
                    s = s + bias_ref[table, p]
                    sink = sink_ref[p]
                    m = jnp.maximum(jnp.max(s, axis=-1, keepdims=True), sink)
                    probs.append(jnp.exp2(s - m).astype(BF16))
                    sink_terms.append(jnp.exp2(sink - m))
                ov = jnp.dot(jnp.concatenate(probs, axis=1), v_pair,
                             preferred_element_type=F32)
                denom = ov[:, LANES:] + jnp.where(
                    masks[0], *[jnp.broadcast_to(t, (WINDOW, LANES)) for t in sink_terms])
                o_ref[0, rows, c * LANES:(c + 1) * LANES] = (ov[:, :LANES] / denom).astype(BF16)


def _attn_prompt(qkv, sinks, bias, cast=()):
    b, l, _ = qkv.shape
    rows = ATTN_Q_BLOCKS * WINDOW
    steps = l // rows
    kcol, vcol = Q_DIM // KV_DIM, Q_DIM // KV_DIM + 1
    prev = lambda i: jnp.maximum(ATTN_Q_BLOCKS * i - 1, 0)
    cast_in, cast_out, cast_shape = _cast_specs(cast, b * steps, lambda b, i: b * steps + i)
    return pl.pallas_call(
        functools.partial(_attn_prompt_kernel, len(cast)),
        grid=(b, steps),
        in_specs=[pl.BlockSpec(memory_space=pltpu.SMEM),
                  pl.BlockSpec((1, rows, Q_DIM), lambda b, i: (b, i, 0)),
                  pl.BlockSpec((1, WINDOW, KV_DIM), lambda b, i: (b, prev(i), kcol)),
                  pl.BlockSpec((1, rows, KV_DIM), lambda b, i: (b, i, kcol)),
                  pl.BlockSpec((1, WINDOW, KV_DIM), lambda b, i: (b, prev(i), vcol)),
                  pl.BlockSpec((1, rows, KV_DIM), lambda b, i: (b, i, vcol)),
                  pl.BlockSpec(bias.shape, lambda b, i: (0, 0, 0, 0))] + cast_in,
        out_specs=[pl.BlockSpec((1, rows, Q_DIM), lambda b, i: (b, i, 0))] + cast_out,
        out_shape=[jax.ShapeDtypeStruct((b, l, Q_DIM), BF16)] + cast_shape,
        compiler_params=_params("arbitrary", "arbitrary"), name="attn_prompt",
    )(sinks, qkv, qkv, qkv, qkv, qkv, bias, *[w for w, _ in cast])


def _own_slab(ref, owner_of_all, layer):
    if not owner_of_all:
        return ref
    for other in range(ref.shape[0]):
        if other != layer:
            ref[other] = jnp.zeros(ref.shape[1:], ref.dtype)
    return ref.at[layer]


def _stacked_out_specs(prev, n_layers, layer, slab_shape, block, index):
    shape = jax.ShapeDtypeStruct((n_layers,) + slab_shape, F32)
    if prev is None:
        return pl.BlockSpec((n_layers,) + block, lambda b: (0,) + index(b)), shape, [], []
    spec = pl.BlockSpec((None,) + block, lambda b: (layer,) + index(b))
    return spec, shape, [pl.BlockSpec(memory_space=pl.ANY)], [prev]


def _attn_sample_kernel(layer, owner_of_all, q_ref, kn_ref, vn_ref, kt_ref, vt_ref, bo_ref, bn_ref,
                        sink_ref, *rest):
    o_ref, kto_ref, vto_ref = rest[-3:]
    kto_ref = _own_slab(kto_ref, owner_of_all, layer)
    vto_ref = _own_slab(vto_ref, owner_of_all, layer)
    bt, l, _ = q_ref.shape
    masks = _half_masks(l)
    keep = lax.broadcasted_iota(jnp.int32, (KV_DIM, WINDOW), 1) < WINDOW - l
    zero_block = jnp.zeros((l, LANES), F32)
    pad_rows = jnp.zeros((WINDOW - l, KV_DIM), F32)
    pad_keys = jnp.zeros((NEW_KEY_PAD - l, KV_DIM), F32)
    ones_old = jnp.ones((LANES, WINDOW), BF16)
    ones_new = jnp.ones((NEW_KEY_PAD, LANES), BF16)
    bias_old = bo_ref[...]
    bias_new = bn_ref[...]
    sink = sink_ref[...]

    def body(b, carry):
        q = q_ref[b] * (HEAD_DIM ** -0.5)
        pieces = []
        for p in range(N_HEADS):
            c = p // HALVES
            src = jnp.where(masks[p % HALVES], q[:, c * LANES:(c + 1) * LANES], 0.0)
            pieces.append(jnp.concatenate(
                [src, zero_block] if c // GROUP == 0 else [zero_block, src], axis=1))
        qt = jnp.concatenate(pieces, axis=0).astype(BF16)

        k_old, v_old = kt_ref[b], vt_ref[b]
        k_new, v_new = kn_ref[b], vn_ref[b]
        k_pad = jnp.concatenate([k_new, pad_keys], axis=0).astype(BF16)
        v_pad = jnp.concatenate([v_new, pad_keys], axis=0).astype(BF16)
        s_old = jnp.dot(qt, k_old.astype(BF16), preferred_element_type=F32) + bias_old
        s_new = lax.dot_general(qt, k_pad, NT_DIMS, preferred_element_type=F32) + bias_new
        m = jnp.maximum(sink, jnp.maximum(jnp.max(s_old, axis=-1, keepdims=True),
                                          jnp.max(s_new, axis=-1, keepdims=True)))
        p_old = jnp.exp(s_old - m).astype(BF16)
        p_new = jnp.exp(s_new - m).astype(BF16)
        v_old_1 = jnp.concatenate([v_old.astype(BF16), ones_old], axis=0)
        v_new_1 = jnp.concatenate([v_pad, ones_new], axis=1)
        o = lax.dot_general(p_old, v_old_1, NT_DIMS, preferred_element_type=F32)
        o = o + jnp.dot(p_new, v_new_1, preferred_element_type=F32)
        inv = 1.0 / (o[:, KV_DIM:] + jnp.exp(sink - m))
        cols = []
        for c in range(N_HEADS // HALVES):
            lanes = slice((c // GROUP) * LANES, (c // GROUP + 1) * LANES)
            parts = [o[(HALVES * c + half) * l:(HALVES * c + half + 1) * l, lanes]
                     * inv[(HALVES * c + half) * l:(HALVES * c + half + 1) * l, :]
                     for half in range(HALVES)]
            cols.append(jnp.where(masks[0], parts[0], parts[1]))
        o_ref[b] = jnp.concatenate(cols, axis=1).astype(BF16)

        for old, new, out_ref in ((k_old, k_new, kto_ref), (v_old, v_new, vto_ref)):
            shifted = pltpu.roll(old, WINDOW - l, axis=1)
            placed = jnp.concatenate([pad_rows, new], axis=0).T
            out_ref[b] = jnp.where(keep, shifted, placed)
        return carry

    lax.fori_loop(0, bt, body, 0, unroll=8)


def _attn_sample(qkv, cache_kt, cache_vt, layer, bias_old, bias_new, sink_col, prev_kv):
    b, l, _ = qkv.shape
    n_layers = cache_kt.shape[0]
    bt = min(SAMPLE_BATCH_TILE, b)
    kcol, vcol = Q_DIM // KV_DIM, Q_DIM // KV_DIM + 1
    const = lambda b: (0, 0)
    in_specs = [pl.BlockSpec((bt, l, Q_DIM), lambda b: (b, 0, 0)),
                pl.BlockSpec((bt, l, KV_DIM), lambda b: (b, 0, kcol)),
                pl.BlockSpec((bt, l, KV_DIM), lambda b: (b, 0, vcol)),
                pl.BlockSpec((None, bt, KV_DIM, WINDOW), lambda b: (layer, b, 0, 0)),
                pl.BlockSpec((None, bt, KV_DIM, WINDOW), lambda b: (layer, b, 0, 0)),
                pl.BlockSpec(bias_old.shape, const),
                pl.BlockSpec(bias_new.shape, const),
                pl.BlockSpec(sink_col.shape, const)]
    inputs = [qkv, qkv, qkv, cache_kt, cache_vt, bias_old, bias_new, sink_col]
    out_specs = [pl.BlockSpec((bt, l, Q_DIM), lambda b: (b, 0, 0))]
    out_shape = [jax.ShapeDtypeStruct((b, l, Q_DIM), BF16)]
    aliases = {}
    for prev in (prev_kv if prev_kv is not None else (None, None)):
        spec, shape, extra_specs, extra_inputs = _stacked_out_specs(
            prev, n_layers, layer, (b, KV_DIM, WINDOW), (bt, KV_DIM, WINDOW), lambda b: (b, 0, 0))
        if extra_inputs:
            aliases[len(inputs)] = len(out_specs)
        in_specs += extra_specs
        inputs += extra_inputs
        out_specs.append(spec)
        out_shape.append(shape)
    return pl.pallas_call(
        functools.partial(_attn_sample_kernel, layer, prev_kv is None),
        grid=(b // bt,), in_specs=in_specs, out_specs=out_specs, out_shape=out_shape,
        input_output_aliases=aliases,
        compiler_params=_params("parallel"), name="attn_sample",
    )(*inputs)


def _attn_bias_tables(l_sample):
    slopes = jnp.exp2(-8.0 * jnp.arange(1, N_HEADS + 1, dtype=F32) / N_HEADS)
    slopes = slopes[jnp.array(HEAD_ORDER)][:, None, None]
    t = jnp.arange(WINDOW, dtype=jnp.int32)[:, None]
    s = jnp.arange(2 * WINDOW, dtype=jnp.int32)[None, :]
    dist = t + WINDOW - s
    valid = (dist >= 0) & (dist <= WINDOW)
    later = jnp.where(valid[None], -slopes * dist.astype(F32)[None], MASKED)
    first = jnp.where((s >= WINDOW)[None], later, MASKED)
    prompt = jnp.stack([first, later]) * LOG2E
    tq = jnp.arange(l_sample, dtype=jnp.int32)[:, None]
    sk = jnp.arange(WINDOW + NEW_KEY_PAD, dtype=jnp.int32)[None, :]
    dist = tq + WINDOW - sk
    valid = (dist >= 0) & (dist <= WINDOW) & (sk < WINDOW + l_sample)
    sample = jnp.where(valid[None], -slopes * dist.astype(F32)[None], MASKED)
    sample = sample.reshape(N_HEADS * l_sample, WINDOW + NEW_KEY_PAD)
    return prompt, sample[:, :WINDOW], sample[:, WINDOW:]


def _log_sigmoid(x):
    return jnp.minimum(x, 0.0) - jnp.log(1.0 + jnp.exp(-jnp.abs(x)))


def _chunk_cumsum(x, c):
    pos = jnp.bitwise_and(lax.broadcasted_iota(jnp.int32, x.shape, 0), c - 1)
    shift = 1
    while shift < c:
        x = x + jnp.where(pos >= shift, pltpu.roll(x, shift, axis=0), 0.0)
        shift *= 2
    return x


def _gla_proj_kernel(chunk, x_ref, g_ref, wm_ref, wgd_ref, wg_ref, bg_ref, qk_ref, vg_ref, b_ref):
    h = _rms(x_ref[...], g_ref[...]).astype(BF16)
    gd = lax.dot_general(h, wgd_ref[...], NT_DIMS, preferred_element_type=F32)
    pre = jnp.dot(gd.astype(BF16), wg_ref[...].astype(BF16), preferred_element_type=F32) + bg_ref[...]
    b_ref[...] = _chunk_cumsum(_log_sigmoid(pre) / GLA_GATE_NORMALIZER, chunk)
    n_qk, n_qkv = 2 * GLA_DK, 2 * GLA_DK + GLA_DV
    proj = lambda rows: lax.dot_general(h, wm_ref[rows, :], NT_DIMS, preferred_element_type=F32)
    qk_ref[...] = proj(slice(0, n_qk))
    vg_ref[:, :GLA_DV] = proj(slice(n_qk, n_qkv)).astype(BF16)
    vg_ref[:, GLA_DV:] = _silu(proj(slice(n_qkv, n_qkv + GLA_DV))).astype(BF16)


def _gla_proj(x, g, w_in_t, w_gate, b_gate, layers, chunk, tm):
    t = x.shape[0]
    lg, lw = layers
    n_main = 2 * GLA_DK + 2 * GLA_DV
    return pl.pallas_call(
        functools.partial(_gla_proj_kernel, chunk),
        grid=(t // tm,),
        in_specs=[pl.BlockSpec((tm, D_MODEL), lambda i: (i, 0)),
                  pl.BlockSpec((None, 1, D_MODEL), lambda i: (lg, 0, 0)),
                  pl.BlockSpec((None, n_main, D_MODEL), lambda i: (lw, 0, 0)),
                  pl.BlockSpec((None, GLA_GATE_RANK, D_MODEL),
                               lambda i: (lw, n_main // GLA_GATE_RANK, 0)),
                  pl.BlockSpec((None, GLA_GATE_RANK, GLA_DK), lambda i: (lw, 0, 0)),
                  pl.BlockSpec((None, 1, GLA_DK), lambda i: (lw, 0, 0))],
        out_specs=[pl.BlockSpec((tm, 2 * GLA_DK), lambda i: (i, 0)),
                   pl.BlockSpec((tm, 2 * GLA_DV), lambda i: (i, 0)),
                   pl.BlockSpec((tm, GLA_DK), lambda i: (i, 0))],
        out_shape=[jax.ShapeDtypeStruct((t, 2 * GLA_DK), F32),
                   jax.ShapeDtypeStruct((t, 2 * GLA_DV), BF16),
                   jax.ShapeDtypeStruct((t, GLA_DK), F32)],
        compiler_params=_params("parallel"), name="gla_proj",
    )(x, g, w_in_t, w_in_t, w_gate, b_gate)


def _gla_gate_out(o, gate, gn):
    return (_rms(o, gn) * gate).astype(BF16)


def _gla_prompt_kernel(q_ref, k_ref, v_ref, r_ref, b_ref, gn_ref, o_ref, s_ref, st_ref):
    c = GLA_CHUNK
    rows = GLA_GROUP * c
    n_groups = q_ref.shape[1] // rows
    row = lax.broadcasted_iota(jnp.int32, (rows, rows), 0)
    col = lax.broadcasted_iota(jnp.int32, (rows, rows), 1)
    causal = (row >= col) & (col >= row - jnp.bitwise_and(row, c - 1))
    gn = gn_ref[...]

    for hd in range(GLA_HEADS_PER_STEP):
        ks = slice(hd * GLA_DK_HEAD, (hd + 1) * GLA_DK_HEAD)
        vs = slice(hd * GLA_DV_HEAD, (hd + 1) * GLA_DV_HEAD)
        st_ref[...] = jnp.zeros_like(st_ref)

        def body(g, carry):
            sl = pl.ds(pl.multiple_of(g * rows, rows), rows)
            q = q_ref[0, sl, ks] * (GLA_DK_HEAD ** -0.5)
            k = k_ref[0, sl, ks]
            v_bf = v_ref[0, sl, vs]
            b = b_ref[0, sl, ks]
            per_chunk = lambda r0: jnp.concatenate(
                [jnp.broadcast_to(b[j * c + r0:j * c + r0 + 1, :], (c, GLA_DK_HEAD))
                 for j in range(GLA_GROUP)], axis=0)
            b_mid = per_chunk(c // 2)
            b_end = per_chunk(c - 1)
            qa = (q * jnp.exp(b - b_mid)).astype(BF16)
            ka = (k * jnp.exp(b_mid - b)).astype(BF16)
            a = lax.dot_general(qa, ka, NT_DIMS, preferred_element_type=F32)
            a = jnp.where(causal, a, 0.0).astype(BF16)
            o_intra = jnp.dot(a, v_bf, preferred_element_type=F32)
            qb = (q * jnp.exp(b)).astype(BF16)
            k_end = (k * jnp.exp(b_end - b)).astype(BF16)
            st = st_ref[...]
            o_inter = []
            for j in range(GLA_GROUP):
                rs = slice(j * c, (j + 1) * c)
                o_inter.append(lax.dot_general(qb[rs], st.astype(BF16), NT_DIMS,
                                               preferred_element_type=F32))
                upd = lax.dot_general(v_bf[rs], k_end[rs], TN_DIMS, preferred_element_type=F32)
                st = st * jnp.exp(b[(j + 1) * c - 1:(j + 1) * c, :]) + upd
            st_ref[...] = st
            o = o_intra + jnp.concatenate(o_inter, axis=0)
            o_ref[0, sl, vs] = _gla_gate_out(o, r_ref[0, sl, vs], gn)
            return carry

        lax.fori_loop(0, n_groups, body, 0, unroll=8)
        s_ref[0, hd] = st_ref[...].T


def _gla_prompt(qk, vg, decay, g_norm, layer):
    b, l, _ = qk.shape
    hp = GLA_HEADS_PER_STEP
    dk, dv = hp * GLA_DK_HEAD, hp * GLA_DV_HEAD
    kcol = GLA_DK // dk
    gcol = GLA_DV // dv
    return pl.pallas_call(
        _gla_prompt_kernel,
        grid=(b, GLA_HEADS // hp),
        in_specs=[pl.BlockSpec((1, l, dk), lambda b, h: (b, 0, h)),
                  pl.BlockSpec((1, l, dk), lambda b, h: (b, 0, kcol + h)),
                  pl.BlockSpec((1, l, dv), lambda b, h: (b, 0, h)),
                  pl.BlockSpec((1, l, dv), lambda b, h: (b, 0, gcol + h)),
                  pl.BlockSpec((1, l, dk), lambda b, h: (b, 0, h)),
                  pl.BlockSpec((None, 1, GLA_DV_HEAD), lambda b, h: (layer, 0, 0))],
        out_specs=[pl.BlockSpec((1, l, dv), lambda b, h: (b, 0, h)),
                   pl.BlockSpec((1, hp, GLA_DK_HEAD, GLA_DV_HEAD), lambda b, h: (b, h, 0, 0))],
        out_shape=[jax.ShapeDtypeStruct((b, l, GLA_DV), BF16),
                   jax.ShapeDtypeStruct((b, GLA_HEADS, GLA_DK_HEAD, GLA_DV_HEAD), F32)],
        scratch_shapes=[pltpu.VMEM((GLA_DV_HEAD, GLA_DK_HEAD), F32)],
        compiler_params=_params("parallel", "parallel"), name="gla_prompt",
    )(qk, qk, vg, vg, decay, g_norm)


def _gla_sample_kernel(layer, owner_of_all, q_ref, k_ref, v_ref, r_ref, b_ref, s0_hbm, gn_ref, *rest):
    o_ref, s_ref, ring_ref, sems = rest[-4:]
    s_ref = _own_slab(s_ref, owner_of_all, layer)
    bt, c, _ = q_ref.shape

    step, n_steps = pl.program_id(0), pl.num_programs(0)
    ahead = STATE_RING_SLOTS - 1

    def fetch(s):
        slot = lax.rem(s, STATE_RING_SLOTS)
        return pltpu.make_async_copy(s0_hbm.at[layer, pl.ds(s * bt, bt)], ring_ref.at[slot],
                                     sems.at[slot])

    @pl.when(step == 0)
    def _():
        for s in range(ahead):
            fetch(s).start()

    @pl.when(step + ahead < n_steps)
    def _():
        fetch(step + ahead).start()

    fetch(step).wait()
    s0_ref = ring_ref.at[lax.rem(step, STATE_RING_SLOTS)]
    row = lax.broadcasted_iota(jnp.int32, (c, c), 0)
    col = lax.broadcasted_iota(jnp.int32, (c, c), 1)
    causal = row >= col
    gn = gn_ref[...]

    def body(i, carry):
        q = q_ref[i] * (GLA_DK_HEAD ** -0.5)
        k = k_ref[i]
        v = v_ref[i].astype(F32)
        r = r_ref[i]
        b = b_ref[i]
        b_mid = b[c // 2:c // 2 + 1, :]
        b_end = b[c - 1:c, :]
        qa = q * jnp.exp(b - b_mid)
        ka = k * jnp.exp(b_mid - b)
        qb = (q * jnp.exp(b)).astype(BF16)
        k_end_t = (k * jnp.exp(b_end - b)).T
        dec_col = jnp.broadcast_to(jnp.exp(b_end), (c, GLA_DK)).T[:, 0:1]
        for h in range(GLA_HEADS):
            ks = slice(h * GLA_DK_HEAD, (h + 1) * GLA_DK_HEAD)
            vs = slice(h * GLA_DV_HEAD, (h + 1) * GLA_DV_HEAD)
            s0 = s0_ref[i, h]
            a = lax.dot_general(qa[:, ks], ka[:, ks], NT_DIMS, preferred_element_type=F32)
            a = jnp.where(causal, a, 0.0)
            o = jnp.dot(a, v[:, vs], preferred_element_type=F32)
            o = o + jnp.dot(qb[:, ks], s0.astype(BF16), preferred_element_type=F32)
            upd = jnp.dot(k_end_t[ks, :], v[:, vs], preferred_element_type=F32)
            s_ref[i, h] = s0 * dec_col[ks, :] + upd
            o_ref[i, :, vs] = _gla_gate_out(o, r[:, vs], gn)
        return carry

    lax.fori_loop(0, bt, body, 0, unroll=2)


def _gla_sample(qk, vg, decay, state, g_norm, layer, prev_state):
    b, l, _ = qk.shape
    bt = min(SAMPLE_BATCH_TILE, b)
    state_block = (bt, GLA_HEADS, GLA_DK_HEAD, GLA_DV_HEAD)
    in_specs = [pl.BlockSpec((bt, l, GLA_DK), lambda b: (b, 0, 0)),
                pl.BlockSpec((bt, l, GLA_DK), lambda b: (b, 0, 1)),
                pl.BlockSpec((bt, l, GLA_DV), lambda b: (b, 0, 0)),
                pl.BlockSpec((bt, l, GLA_DV), lambda b: (b, 0, 1)),
                pl.BlockSpec((bt, l, GLA_DK), lambda b: (b, 0, 0)),
                pl.BlockSpec(memory_space=pl.ANY),
                pl.BlockSpec((None, 1, GLA_DV_HEAD), lambda b: (layer, 0, 0))]
    inputs = [qk, qk, vg, vg, decay, state, g_norm]
    assert b // bt >= STATE_RING_SLOTS - 1
    spec, shape, extra_specs, extra_inputs = _stacked_out_specs(
        prev_state, state.shape[0], layer, state.shape[1:], state_block, lambda b: (b, 0, 0, 0))
    return pl.pallas_call(
        functools.partial(_gla_sample_kernel, layer, prev_state is None),
        grid=(b // bt,),
        in_specs=in_specs + extra_specs,
        out_specs=[pl.BlockSpec((bt, l, GLA_DV), lambda b: (b, 0, 0)), spec],
        out_shape=[jax.ShapeDtypeStruct((b, l, GLA_DV), BF16), shape],
        input_output_aliases={len(inputs): 1} if extra_inputs else {},
        scratch_shapes=[pltpu.VMEM((STATE_RING_SLOTS,) + state_block, F32),
                        pltpu.SemaphoreType.DMA((STATE_RING_SLOTS,))],
        compiler_params=_params("arbitrary"), name="gla_sample",
    )(*inputs, *extra_inputs)


def kernel(x_prompt, x_sample, cache_k, cache_v, state_gla, norm_mix, norm_ffn, norm_final,
           w_attn_in, w_attn_out, attn_sinks, w_gla_in, w_gla_gate_up, b_gla_gate, gla_out_norm,
           w_gla_out, w_ffn_in, w_ffn_out):
    bp, lp, _ = x_prompt.shape
    bs, ls, _ = x_sample.shape
    depth = norm_mix.shape[0]
    n_attn = cache_k.shape[0]
    yp = x_prompt.reshape(bp * lp, D_MODEL)
    ys = x_sample.reshape(bs * ls, D_MODEL)
    tp = min(TOKEN_TILE, bp * lp)
    ts = min(TOKEN_TILE, bs * ls)
    bias_p, bias_old, bias_new = _attn_bias_tables(ls)

    g_mix = norm_mix.reshape(depth, 1, D_MODEL)
    g_ffn = norm_ffn.reshape(depth, 1, D_MODEL)
    g_fin = norm_final.reshape(1, D_MODEL)
    w_attn_bf = w_attn_in.astype(BF16)
    w_attn_in_bf = jnp.concatenate(
        [_regroup_heads(w_attn_bf[:, :, :Q_DIM], 2, HEAD_DIM), w_attn_bf[:, :, Q_DIM:]], axis=2)
    w_attn_out_bf = _regroup_heads(w_attn_out.astype(BF16), 1, HEAD_DIM)
    sinks = _regroup_heads(attn_sinks, 1, 1)
    w_gla_in_t = jnp.transpose(w_gla_in, (0, 2, 1)).astype(BF16)
    w_fi = w_fo = w_gla_out_bf = None
    b_gate = b_gla_gate.reshape(-1, 1, GLA_DK)
    g_gla = gla_out_norm.reshape(-1, 1, GLA_DV_HEAD)
    cache_kt = jnp.transpose(cache_k, (0, 1, 3, 4, 2)).reshape(n_attn, bs, KV_DIM, WINDOW)
    cache_vt = jnp.transpose(cache_v, (0, 1, 3, 4, 2)).reshape(n_attn, bs, KV_DIM, WINDOW)

    kp_l, vp_l, sp_l = [], [], []
    new_kv, new_state = None, None
    for i in range(depth):
        j = i // 2
        if i % 2 == 0:
            w_mix, lm = w_attn_out_bf, j
            (qkv_p,) = _norm_matmul(yp, g_mix, [w_attn_in_bf], (i, j), tp)
            (qkv_s,) = _norm_matmul(ys, g_mix, [w_attn_in_bf], (i, j), ts)
            qkv_p = qkv_p.reshape(bp, lp, Q_DIM + 2 * KV_DIM)
            qkv_s = qkv_s.reshape(bs, ls, Q_DIM + 2 * KV_DIM)
            cast = [(w_gla_out, j)] + ([(w_ffn_in, 0), (w_ffn_out, 0)] if i == 0 else [])
            op, w_gla_out_bf, *w_ffn0 = _attn_prompt(qkv_p, sinks[j] * LOG2E, bias_p, cast)
            if w_ffn0:
                w_fi, w_fo = w_ffn0
            sink_col = jnp.repeat(sinks[j], ls).reshape(N_HEADS * ls, 1)
            os_, *new_kv = _attn_sample(qkv_s, cache_kt, cache_vt, j, bias_old, bias_new, sink_col,
                                        new_kv)
            kv_shape = (bp, WINDOW, N_KV_HEADS, HEAD_DIM)
            kp_l.append(qkv_p[:, lp - WINDOW:, Q_DIM:Q_DIM + KV_DIM].reshape(kv_shape))
            vp_l.append(qkv_p[:, lp - WINDOW:, Q_DIM + KV_DIM:].reshape(kv_shape))
        else:
            w_mix, lm = w_gla_out_bf[None], 0
            gla_p = _gla_proj(yp, g_mix, w_gla_in_t, w_gla_gate_up, b_gate, (i, j), GLA_CHUNK, tp)
            gla_s = _gla_proj(ys, g_mix, w_gla_in_t, w_gla_gate_up, b_gate, (i, j), ls, ts)
            op, sp = _gla_prompt(*[t.reshape(bp, lp, -1) for t in gla_p], g_gla, j)
            os_, new_state = _gla_sample(*[t.reshape(bs, ls, -1) for t in gla_s], state_gla, g_gla, j,
                                         new_state)
            sp_l.append(sp)
        last = i == depth - 1
        yp, *w_next = _mix_out_ffn(yp, op.reshape(bp * lp, D_MODEL), w_mix, g_ffn, w_fi, w_fo,
                                   g_fin, (lm, i), last, min(FFN_TOKEN_TILE, bp * lp),
                                   [] if last else [(w_ffn_in, i + 1), (w_ffn_out, i + 1)])
        (ys,) = _mix_out_ffn(ys, os_.reshape(bs * ls, D_MODEL), w_mix, g_ffn, w_fi, w_fo,
                             g_fin, (lm, i), last, min(FFN_TOKEN_TILE, bs * ls))
        if w_next:
            w_fi, w_fo = w_next

    def window_major(xt):
        return jnp.transpose(xt.reshape(n_attn, bs, N_KV_HEADS, HEAD_DIM, WINDOW), (0, 1, 4, 2, 3))

    return (yp.reshape(bp, lp, D_MODEL), ys.reshape(bs, ls, D_MODEL),
            jnp.stack(kp_l), jnp.stack(vp_l), jnp.stack(sp_l),
            window_major(new_kv[0]), window_major(new_kv[1]), new_state)
```
